```python
import math
import jax, jax.numpy as jnp
from jax import lax
import numpy as np

D_MODEL = 1024
BATCH = 8
SEQ = 2048
DEPTH = 2
DEC_BATCH = 128
DEC_SEQ = 8
PAST_LEN = 16384
PAGE_SIZE = 128

HEAD_DIM = 64
H_RET = 6
H_DN = 6
H_GLA = 4
W_RET = H_RET * HEAD_DIM
W_DN = H_DN * HEAD_DIM
W_GLA = H_GLA * HEAD_DIM
MIX_WIDTH = W_RET + W_DN + W_GLA
CONV_W = 4
DN_CONV_CH = 3 * W_DN
GLA_RANK = 16
GLA_TAU = 16.0
D_FF = 4 * D_MODEL
N_MEM = 256
MEM_HEADS = 4
MEM_HEAD_DIM = D_MODEL // MEM_HEADS
RET_CHUNK = 64
DN_CHUNK = 64
GLA_CHUNK = 16
ROPE_BASE = 10000.0
EPS = 1e-6
IN_SPLITS = (W_RET, W_RET, W_RET, W_RET, DN_CONV_CH, W_DN, H_DN, H_DN, W_GLA, W_GLA, W_GLA, W_GLA, GLA_RANK)
IN_COLS = 4 * W_RET + DN_CONV_CH + W_DN + 2 * H_DN + 4 * W_GLA + GLA_RANK

kernel_name = 'hybrid_ret_gdn_gla_memxattn_step'

F32 = jnp.float32


def rmsnorm(x, w):
    xf = x.astype(F32)
    y = xf * lax.rsqrt(jnp.mean(xf * xf, axis=-1, keepdims=True) + EPS)
    return (y * w.astype(F32)).astype(x.dtype)


def head_layernorm(o, w):
    oc = o - jnp.mean(o, axis=-1, keepdims=True)
    return oc * lax.rsqrt(jnp.mean(oc * oc, axis=-1, keepdims=True) + EPS) * w.astype(F32)


def head_rmsnorm(o, w):
    return o * lax.rsqrt(jnp.mean(o * o, axis=-1, keepdims=True) + EPS) * w.astype(F32)


def l2norm(t):
    return t * lax.rsqrt(jnp.sum(t * t, axis=-1, keepdims=True) + EPS)


def to_heads(t, n_heads):
    b, l, _ = t.shape
    return t.reshape(b, l, n_heads, HEAD_DIM).transpose(0, 2, 1, 3)


def from_heads(o):
    b, h, l, d = o.shape
    return o.transpose(0, 2, 1, 3).reshape(b, l, h * d)


def split_cols(proj):
    parts, start = [], 0
    for size in IN_SPLITS:
        parts.append(proj[..., start:start + size])
        start += size
    return parts


def rotary(t, pos):
    half = HEAD_DIM // 2
    inv_freq = ROPE_BASE ** (-jnp.arange(half, dtype=F32) / half)
    ang = pos.astype(F32)[:, None] * inv_freq[None, :]
    cos, sin = jnp.cos(ang), jnp.sin(ang)
    t1, t2 = t[..., :half], t[..., half:]
    return jnp.concatenate([t1 * cos - t2 * sin, t1 * sin + t2 * cos], axis=-1)


def causal_conv_silu(x, buf, w):
    l = x.shape[1]
    xp = jnp.concatenate([buf.astype(x.dtype), x], axis=1)
    acc = xp[:, 0:l, :] * w[0]
    for j in range(1, CONV_W):
        acc = acc + xp[:, j:j + l, :] * w[j]
    return jax.nn.silu(acc), xp[:, -(CONV_W - 1):, :]


def retention_chunked(q, k, v, s0, chunk):
    b, h, l, d = q.shape
    c = chunk
    n = l // c
    qc, kc, vc = (t.reshape(b, h, n, c, d) for t in (q, k, v))
    lg = jnp.log1p(-jnp.exp2(-5.0 - jnp.arange(h, dtype=F32)))
    idx = jnp.arange(c, dtype=F32)
    diff = idx[:, None] - idx[None, :]
    dmask = jnp.where(diff >= 0, jnp.exp(lg[:, None, None] * jnp.maximum(diff, 0.0)), 0.0)
    scores = jnp.einsum('bhnid,bhnjd->bhnij', qc, kc) * dmask[None, :, None]
    o_intra = jnp.einsum('bhnij,bhnje->bhnie', scores, vc)
    xi = jnp.exp(lg[:, None] * (idx + 1.0))
    zeta = jnp.exp(lg[:, None] * (c - 1.0 - idx))
    kv = jnp.einsum('bhnjd,bhnje->bhnde', kc * zeta[None, :, None, :, None], vc)
    g_chunk = jnp.exp(lg * c)[None, :, None, None]

    def step(s, inp):
        q_i, kv_i = inp
        o_i = jnp.einsum('bhid,bhde->bhie', q_i, s)
        return s * g_chunk + kv_i, o_i

    s_fin, o_inter = lax.scan(step, s0.astype(F32), (jnp.moveaxis(qc, 2, 0), jnp.moveaxis(kv, 2, 0)))
    o_inter = jnp.moveaxis(o_inter, 0, 2) * xi[None, :, None, :, None]
    return (o_intra + o_inter).reshape(b, h, l, d), s_fin


def gated_delta_chunked(q, k, v, beta, g, s0, chunk):
    b_, h, l, dk = q.shape
    dv = v.shape[-1]
    c = chunk
    n = l // c
    qc, kc = q.reshape(b_, h, n, c, dk), k.reshape(b_, h, n, c, dk)
    vc = v.reshape(b_, h, n, c, dv)
    bc, gc = beta.reshape(b_, h, n, c), g.reshape(b_, h, n, c)
    cum = jnp.cumsum(gc, axis=-1)
    incl = jnp.tril(jnp.ones((c, c), bool))
    strict = jnp.tril(jnp.ones((c, c), bool), -1)
    gam = jnp.exp(jnp.where(incl, cum[..., :, None] - cum[..., None, :], -jnp.inf))
    kb = kc * bc[..., None]
    a = jnp.where(strict, jnp.einsum('bhnid,bhnjd->bhnij', kb, kc) * gam, 0.0)
    eye = jnp.eye(c, dtype=F32)
    t = lax.linalg.triangular_solve(eye + a, jnp.broadcast_to(eye, a.shape), left_side=True, lower=True)
    u = jnp.einsum('bhnij,bhnje->bhnie', t, vc * bc[..., None])
    w = jnp.einsum('bhnij,bhnjd->bhnid', t, kb * jnp.exp(cum)[..., None])
    attn = jnp.where(incl, jnp.einsum('bhnid,bhnjd->bhnij', qc, kc) * gam, 0.0)
    q_dec = qc * jnp.exp(cum)[..., None]
    last = cum[..., -1:]
    k_dec = kc * jnp.exp(last - cum)[..., None]
    dec = jnp.exp(last)[..., None]

    def step(s, inp):
        u_i, w_i, attn_i, q_i, k_i, d_i = inp
        v_new = u_i - jnp.einsum('bhid,bhde->bhie', w_i, s)
        o_i = jnp.einsum('bhid,bhde->bhie', q_i, s) + jnp.einsum('bhij,bhje->bhie', attn_i, v_new)
        s = s * d_i + jnp.einsum('bhjd,bhje->bhde', k_i, v_new)
        return s, o_i

    xs = tuple(jnp.moveaxis(z, 2, 0) for z in (u, w, attn, q_dec, k_dec, dec))
    s_fin, o = lax.scan(step, s0.astype(F32), xs)
    return jnp.moveaxis(o, 0, 2).reshape(b_, h, l, dv), s_fin


def gla_chunked(q, k, v, g, s0, chunk):
    b_, h, l, d = q.shape
    c = chunk
    n = l // c
    qc, kc, vc, gc = (t.reshape(b_, h, n, c, d) for t in (q, k, v, g))
    cum = jnp.cumsum(gc, axis=3)
    incl = jnp.tril(jnp.ones((c, c), bool))
    rel = jnp.where(incl[:, :, None], cum[:, :, :, :, None, :] - cum[:, :, :, None, :, :], -jnp.inf)
    scores = jnp.einsum('bhnid,bhnjd,bhnijd->bhnij', qc, kc, jnp.exp(rel))
    o_intra = jnp.einsum('bhnij,bhnje->bhnie', scores, vc)
    last = cum[:, :, :, -1:, :]
    q_dec = qc * jnp.exp(cum)
    kv = jnp.einsum('bhnjd,bhnje->bhnde', kc * jnp.exp(last - cum), vc)
    dec = jnp.exp(last[:, :, :, 0, :])[..., None]

    def step(s, inp):
        q_i, kv_i, d_i = inp
        o_i = jnp.einsum('bhid,bhde->bhie', q_i, s)
        return s * d_i + kv_i, o_i

    xs = tuple(jnp.moveaxis(z, 2, 0) for z in (q_dec, kv, dec))
    s_fin, o_inter = lax.scan(step, s0.astype(F32), xs)
    return (o_intra + jnp.moveaxis(o_inter, 0, 2)).reshape(b_, h, l, d), s_fin


def token_mixers(h, pos, s_ret, s_dn, s_conv, s_gla, w_in, ret_norm_w, dn_conv_w, dn_a_log, dn_dt_bias,
                 dn_norm_w, gla_w_a2, gla_b_a, gla_norm_w, w_out):
    l = h.shape[1]
    (r_q, r_k, r_v, r_g, d_qkv, d_g, d_b, d_a, g_q, g_k, g_v, g_g, g_a1) = split_cols(h @ w_in)
    q = rotary(to_heads(r_q, H_RET).astype(F32), pos)
    k = rotary(to_heads(r_k, H_RET).astype(F32), pos) * HEAD_DIM ** -0.5
    v = to_heads(r_v, H_RET).astype(F32)
    o_ret, s_ret_new = retention_chunked(q, k, v, s_ret, math.gcd(l, RET_CHUNK))
    o_ret = from_heads(head_layernorm(o_ret, ret_norm_w)) * jax.nn.silu(r_g.astype(F32))
    qkv, s_conv_new = causal_conv_silu(d_qkv, s_conv, dn_conv_w)
    dq, dk, dv = qkv[..., :W_DN], qkv[..., W_DN:2 * W_DN], qkv[..., 2 * W_DN:]
    q = l2norm(to_heads(dq, H_DN).astype(F32)) * HEAD_DIM ** -0.5
    k = l2norm(to_heads(dk, H_DN).astype(F32))
    v = to_heads(dv, H_DN).astype(F32)
    beta = jax.nn.sigmoid(d_b.astype(F32)).transpose(0, 2, 1)
    g = (-jnp.exp(dn_a_log.astype(F32)) * jax.nn.softplus(d_a.astype(F32) + dn_dt_bias.astype(F32))).transpose(0, 2, 1)
    o_dn, s_dn_new = gated_delta_chunked(q, k, v, beta, g, s_dn, math.gcd(l, DN_CHUNK))
    o_dn = from_heads(head_rmsnorm(o_dn, dn_norm_w)) * jax.nn.silu(d_g.astype(F32))
    q = to_heads(g_q, H_GLA).astype(F32) * HEAD_DIM ** -0.5
    k = to_heads(g_k, H_GLA).astype(F32)
    v = to_heads(g_v, H_GLA).astype(F32)
    gate_logit = (g_a1 @ gla_w_a2 + gla_b_a).astype(F32)
    g = to_heads(jax.nn.log_sigmoid(gate_logit) / GLA_TAU, H_GLA)
    o_gla, s_gla_new = gla_chunked(q, k, v, g, s_gla, math.gcd(l, GLA_CHUNK))
    o_gla = from_heads(head_rmsnorm(o_gla, gla_norm_w)) * jax.nn.silu(g_g.astype(F32))
    o = jnp.concatenate([o_ret, o_dn, o_gla], axis=-1).astype(h.dtype)
    return (o @ w_out, s_ret_new.astype(s_ret.dtype), s_dn_new.astype(s_dn.dtype),
            s_conv_new.astype(s_conv.dtype), s_gla_new.astype(s_gla.dtype))


def memory_kv(mem, norm_w, w_k, w_v):
    b, m, _ = mem.shape
    mn = rmsnorm(mem, norm_w)
    return ((mn @ w_k).reshape(b, m, MEM_HEADS, MEM_HEAD_DIM), (mn @ w_v).reshape(b, m, MEM_HEADS, MEM_HEAD_DIM))


def memory_cross_attention(h, mem_k, mem_v, w_q, w_o):
    b, l, _ = h.shape
    q = (h @ w_q).reshape(b, l, MEM_HEADS, MEM_HEAD_DIM)
    s = jnp.einsum('blhd,bmhd->bhlm', q.astype(F32), mem_k.astype(F32)) * MEM_HEAD_DIM ** -0.5
    p = jax.nn.softmax(s, axis=-1)
    o = jnp.einsum('bhlm,bmhd->blhd', p, mem_v.astype(F32)).reshape(b, l, D_MODEL).astype(h.dtype)
    return o @ w_o


def sq_relu_mlp(h, w1, w2):
    return jnp.square(jax.nn.relu(h @ w1)) @ w2


def run_trunk(x, pos, s_ret, s_dn, s_conv, s_gla, mem_k, mem_v, norm_mix, w_in, ret_norm_w, dn_conv_w,
              dn_a_log, dn_dt_bias, dn_norm_w, gla_w_a2, gla_b_a, gla_norm_w, w_out, norm_cross, w_mq, w_mo,
              norm_ffn, w_ff1, w_ff2, norm_final):
    new_ret, new_dn, new_conv, new_gla = [], [], [], []
    for li in range(DEPTH):
        mixed, r_s, d_s, c_s, g_s = token_mixers(
            rmsnorm(x, norm_mix[li]), pos, s_ret[li], s_dn[li], s_conv[li], s_gla[li], w_in[li],
            ret_norm_w[li], dn_conv_w[li], dn_a_log[li], dn_dt_bias[li], dn_norm_w[li], gla_w_a2[li],
            gla_b_a[li], gla_norm_w[li], w_out[li])
        x = x + mixed
        x = x + memory_cross_attention(rmsnorm(x, norm_cross[li]), mem_k[li], mem_v[li], w_mq[li], w_mo[li])
        x = x + sq_relu_mlp(rmsnorm(x, norm_ffn[li]), w_ff1[li], w_ff2[li])
        new_ret.append(r_s)
        new_dn.append(d_s)
        new_conv.append(c_s)
        new_gla.append(g_s)
    return (rmsnorm(x, norm_final), jnp.stack(new_ret), jnp.stack(new_dn), jnp.stack(new_conv),
            jnp.stack(new_gla))


def setup_inputs(seed: int = 0) -> dict:
    key = jax.random.key(seed)
    ks = jax.random.split(key, 32)

    def nrm(i, shape, scale):
        return jax.random.normal(ks[i], shape, F32) * scale

    def gain(i, shape):
        return 1.0 + nrm(i, shape, 0.02)

    dt = jnp.exp(jax.random.uniform(ks[13], (DEPTH, H_DN), F32, math.log(1e-3), math.log(1e-1)))
    return {
        'x_prompt': nrm(0, (BATCH, SEQ, D_MODEL), 1.0),
        'x_sample': nrm(1, (DEC_BATCH, DEC_SEQ, D_MODEL), 1.0),
        'state_ret': nrm(2, (DEPTH, DEC_BATCH, H_RET, HEAD_DIM, HEAD_DIM), 0.3),
        'state_dn': nrm(3, (DEPTH, DEC_BATCH, H_DN, HEAD_DIM, HEAD_DIM), 0.3),
        'state_dn_conv': nrm(4, (DEPTH, DEC_BATCH, CONV_W - 1, DN_CONV_CH), 1.0),
        'state_gla': nrm(5, (DEPTH, DEC_BATCH, H_GLA, HEAD_DIM, HEAD_DIM), 0.3),
        'cache_mem_k': nrm(6, (DEPTH, DEC_BATCH, N_MEM, MEM_HEADS, MEM_HEAD_DIM), 1.0),
        'cache_mem_v': nrm(7, (DEPTH, DEC_BATCH, N_MEM, MEM_HEADS, MEM_HEAD_DIM), 1.0),
        'mem_prompt': nrm(8, (BATCH, N_MEM, D_MODEL), 1.0),
        'norm_mix': gain(9, (DEPTH, D_MODEL)),
        'w_in': nrm(10, (DEPTH, D_MODEL, IN_COLS), D_MODEL ** -0.5),
        'ret_norm_w': gain(11, (DEPTH, HEAD_DIM)),
        'dn_conv_w': nrm(12, (DEPTH, CONV_W, DN_CONV_CH), CONV_W ** -0.5),
        'dn_a_log': jnp.log(jax.random.uniform(ks[14], (DEPTH, H_DN), F32, 1.0, 16.0)),
        'dn_dt_bias': dt + jnp.log(-jnp.expm1(-dt)),
        'dn_norm_w': gain(15, (DEPTH, HEAD_DIM)),
        'gla_w_a2': nrm(16, (DEPTH, GLA_RANK, W_GLA), GLA_RANK ** -0.5),
        'gla_b_a': nrm(17, (DEPTH, W_GLA), 0.1),
        'gla_norm_w': gain(18, (DEPTH, HEAD_DIM)),
        'w_out': nrm(19, (DEPTH, MIX_WIDTH, D_MODEL), MIX_WIDTH ** -0.5),
        'norm_cross': gain(20, (DEPTH, D_MODEL)),
        'norm_mem': gain(21, (DEPTH, D_MODEL)),
        'w_mq': nrm(22, (DEPTH, D_MODEL, D_MODEL), D_MODEL ** -0.5),
        'w_mk': nrm(23, (DEPTH, D_MODEL, D_MODEL), D_MODEL ** -0.5),
        'w_mv': nrm(24, (DEPTH, D_MODEL, D_MODEL), D_MODEL ** -0.5),
        'w_mo': nrm(25, (DEPTH, D_MODEL, D_MODEL), D_MODEL ** -0.5),
        'norm_ffn': gain(26, (DEPTH, D_MODEL)),
        'w_ff1': nrm(27, (DEPTH, D_MODEL, D_FF), D_MODEL ** -0.5),
        'w_ff2': nrm(28, (DEPTH, D_FF, D_MODEL), D_FF ** -0.5),
        'norm_final': gain(29, (D_MODEL,)),
    }


def reference(x_prompt, x_sample, state_ret, state_dn, state_dn_conv, state_gla, cache_mem_k, cache_mem_v,
              mem_prompt, norm_mix, w_in, ret_norm_w, dn_conv_w, dn_a_log, dn_dt_bias, dn_norm_w, gla_w_a2,
              gla_b_a, gla_norm_w, w_out, norm_cross, norm_mem, w_mq, w_mk, w_mv, w_mo, norm_ffn, w_ff1, w_ff2,
              norm_final):
    b_p, l_p, _ = x_prompt.shape
    dt = x_prompt.dtype
    mk, mv = [], []
    for li in range(DEPTH):
        k_l, v_l = memory_kv(mem_prompt, norm_mem[li], w_mk[li], w_mv[li])
        mk.append(k_l)
        mv.append(v_l)
    mem_k_p = jnp.stack(mk)
    mem_v_p = jnp.stack(mv)
    z_ret = jnp.zeros((DEPTH, b_p, H_RET, HEAD_DIM, HEAD_DIM), dt)
    z_dn = jnp.zeros((DEPTH, b_p, H_DN, HEAD_DIM, HEAD_DIM), dt)
    z_conv = jnp.zeros((DEPTH, b_p, CONV_W - 1, DN_CONV_CH), dt)
    z_gla = jnp.zeros((DEPTH, b_p, H_GLA, HEAD_DIM, HEAD_DIM), dt)
    pos_p = jnp.arange(l_p, dtype=jnp.int32)
    y_prompt, p_ret, p_dn, p_conv, p_gla = run_trunk(
        x_prompt, pos_p, z_ret, z_dn, z_conv, z_gla, mem_k_p, mem_v_p, norm_mix, w_in, ret_norm_w, dn_conv_w,
        dn_a_log, dn_dt_bias, dn_norm_w, gla_w_a2, gla_b_a, gla_norm_w, w_out, norm_cross, w_mq, w_mo,
        norm_ffn, w_ff1, w_ff2, norm_final)
    pos_s = PAST_LEN + jnp.arange(x_sample.shape[1], dtype=jnp.int32)
    y_sample, s_ret, s_dn, s_conv, s_gla = run_trunk(
        x_sample, pos_s, state_ret, state_dn, state_dn_conv, state_gla, cache_mem_k, cache_mem_v, norm_mix,
        w_in, ret_norm_w, dn_conv_w, dn_a_log, dn_dt_bias, dn_norm_w, gla_w_a2, gla_b_a, gla_norm_w, w_out,
        norm_cross, w_mq, w_mo, norm_ffn, w_ff1, w_ff2, norm_final)
    return (y_prompt, y_sample, p_ret, p_dn, p_conv, p_gla, mem_k_p, mem_v_p, s_ret, s_dn, s_conv, s_gla)
```

```python
import functools
import math

import jax
import jax.numpy as jnp
from jax import lax
from jax.experimental import pallas as pl
from jax.experimental.pallas import tpu as pltpu

F32 = jnp.float32
BF16 = jnp.bfloat16

HEAD_DIM = 64
PAIR = 2 * HEAD_DIM
H_RET, H_DN, H_GLA = 6, 6, 4
W_RET, W_DN, W_GLA = H_RET * HEAD_DIM, H_DN * HEAD_DIM, H_GLA * HEAD_DIM
CONV_W = 4
DN_CONV_CH = 3 * W_DN
GLA_RANK = 16
GLA_TAU = 16.0
MEM_HEADS = 4
DN_CHUNK = 64
GLA_CHUNK = 16
ROPE_BASE = 10000.0
EPS = 1e-6
PAST_LEN = 16384

DN_COLS = DN_CONV_CH + W_DN + PAIR
GLA_COLS = 4 * W_GLA + PAIR
RET_COLS = 4 * W_RET
DN_A_LANE = H_DN

VMEM_LIMIT = 56 * 1024 * 1024


def _cparams(sem):
    return pltpu.CompilerParams(dimension_semantics=sem, vmem_limit_bytes=VMEM_LIMIT)


def _dot(a, b):
    return jnp.dot(a, b, preferred_element_type=F32)


def _dot_nt(a, b):
    return lax.dot_general(a, b, (((1,), (1,)), ((), ())), preferred_element_type=F32)


def _dot_tn(a, b):
    return lax.dot_general(a, b, (((0,), (0,)), ((), ())), preferred_element_type=F32)


def _split2(a):
    hi = a.astype(BF16)
    lo = (a - hi.astype(F32)).astype(BF16)
    return hi, lo


def _split3(a):
    hi = a.astype(BF16)
    r = a - hi.astype(F32)
    mid = r.astype(BF16)
    lo = (r - mid.astype(F32)).astype(BF16)
    return hi, mid, lo


def _dot_exact_rhs(a, b_bf16):
    hi, mid, lo = _split3(a)
    return _dot(hi, b_bf16) + _dot(mid, b_bf16) + _dot(lo, b_bf16)


def _dot_exact_lhs(a_bf16, b):
    hi, mid, lo = _split3(b)
    return _dot(a_bf16, hi) + _dot(a_bf16, mid) + _dot(a_bf16, lo)


def _dot3(a, b):
    ah, al = _split2(a)
    bh, bl = _split2(b)
    return _dot(ah, bh) + _dot(ah, bl) + _dot(al, bh)


def _silu(x):
    return x * jax.nn.sigmoid(x)


def _softplus(x):
    return jnp.maximum(x, 0.0) + jnp.log1p(jnp.exp(-jnp.abs(x)))


def _lane_iota(shape):
    return lax.broadcasted_iota(jnp.int32, shape, len(shape) - 1)


def _row_iota(shape):
    return lax.broadcasted_iota(jnp.int32, shape, 0)


def _head0_lanes():
    return _lane_iota((1, PAIR)) < HEAD_DIM


def _stack(x, m0):
    return jnp.concatenate([jnp.where(m0, x, 0.0), jnp.where(m0, 0.0, x)], axis=0)


def _block_diag_mask():
    r = _row_iota((PAIR, PAIR))
    c = _lane_iota((PAIR, PAIR))
    return (r >> 6) == (c >> 6)


def _group_mean_matrix(bm):
    return jnp.where(bm, 1.0 / HEAD_DIM, 0.0).astype(BF16)


def _load_pair_state(S_ref, s0_ref, n_pairs):
    S_ref[...] = jnp.zeros_like(S_ref)
    for p in range(n_pairs):
        S_ref[p, 0:HEAD_DIM, 0:HEAD_DIM] = s0_ref[0, 2 * p]
        S_ref[p, HEAD_DIM:PAIR, HEAD_DIM:PAIR] = s0_ref[0, 2 * p + 1]


def _store_pair_state(sout_ref, S_ref, n_pairs):
    for p in range(n_pairs):
        sout_ref[0, 2 * p] = S_ref[p, 0:HEAD_DIM, 0:HEAD_DIM]
        sout_ref[0, 2 * p + 1] = S_ref[p, HEAD_DIM:PAIR, HEAD_DIM:PAIR]


def _rms_matmul_kernel(x_ref, g_ref, w_ref, o_ref):
    x = x_ref[...]
    xn = x * lax.rsqrt(jnp.mean(x * x, axis=-1, keepdims=True) + EPS) * g_ref[...]
    o_ref[...] = _dot(xn.astype(BF16), w_ref[...])


def rms_matmul(x, g, w_bf16, tm):
    t, d = x.shape
    n = w_bf16.shape[1]
    return pl.pallas_call(
        _rms_matmul_kernel,
        grid=(t // tm,),
        in_specs=[pl.BlockSpec((tm, d), lambda i: (i, 0)),
                  pl.BlockSpec((1, d), lambda i: (0, 0)),
                  pl.BlockSpec((d, n), lambda i: (0, 0))],
        out_specs=pl.BlockSpec((tm, n), lambda i: (i, 0)),
        out_shape=jax.ShapeDtypeStruct((t, n), F32),
        compiler_params=_cparams(("parallel",)),
        name="rms_matmul",
    )(x, g.reshape(1, d), w_bf16)


def _matmul_res_kernel(*refs, widths):
    n_in = len(widths)
    a_refs = refs[:n_in]
    w_ref, r_ref, o_ref = refs[n_in:]
    acc = r_ref[...]
    off = 0
    for a_ref, wk in zip(a_refs, widths):
        acc = acc + _dot(a_ref[...].astype(BF16), w_ref[off:off + wk, :])
        off += wk
    o_ref[...] = acc


def matmul_res(a_list, w_bf16, res, tm):
    t, n = res.shape
    widths = tuple(a.shape[1] for a in a_list)
    k = w_bf16.shape[0]
    assert sum(widths) == k
    in_specs = [pl.BlockSpec((tm, wk), lambda i: (i, 0)) for wk in widths]
    in_specs += [pl.BlockSpec((k, n), lambda i: (0, 0)), pl.BlockSpec((tm, n), lambda i: (i, 0))]
    return pl.pallas_call(
        functools.partial(_matmul_res_kernel, widths=widths),
        grid=(t // tm,),
        in_specs=in_specs,
        out_specs=pl.BlockSpec((tm, n), lambda i: (i, 0)),
        out_shape=jax.ShapeDtypeStruct((t, n), F32),
        compiler_params=_cparams(("parallel",)),
        name="matmul_res",
    )(*a_list, w_bf16, res)


def _ffn_kernel(x_ref, g_ref, w1_ref, w2_ref, gf_ref, o_ref, xn_ref, acc_ref, *, final_norm):
    j = pl.program_id(1)

    @pl.when(j == 0)
    def _():
        x = x_ref[...]
        xn = x * lax.rsqrt(jnp.mean(x * x, axis=-1, keepdims=True) + EPS) * g_ref[...]
        xn_ref[...] = xn.astype(BF16)
        acc_ref[...] = x

    h = _dot(xn_ref[...], w1_ref[...])
    h = jnp.square(jnp.maximum(h, 0.0))
    acc_ref[...] += _dot(h.astype(BF16), w2_ref[...])

    @pl.when(j == pl.num_programs(1) - 1)
    def _():
        y = acc_ref[...]
        if final_norm:
            y = y * lax.rsqrt(jnp.mean(y * y, axis=-1, keepdims=True) + EPS) * gf_ref[...]
        o_ref[...] = y


def ffn(x, g, w1_bf16, w2_bf16, g_final, final_norm, tm, tf):
    t, d = x.shape
    f = w1_bf16.shape[1]
    return pl.pallas_call(
        functools.partial(_ffn_kernel, final_norm=final_norm),
        grid=(t // tm, f // tf),
        in_specs=[pl.BlockSpec((tm, d), lambda i, j: (i, 0)),
                  pl.BlockSpec((1, d), lambda i, j: (0, 0)),
                  pl.BlockSpec((d, tf), lambda i, j: (0, j)),
                  pl.BlockSpec((tf, d), lambda i, j: (j, 0)),
                  pl.BlockSpec((1, d), lambda i, j: (0, 0))],
        out_specs=pl.BlockSpec((tm, d), lambda i, j: (i, 0)),
        out_shape=jax.ShapeDtypeStruct((t, d), F32),
        scratch_shapes=[pltpu.VMEM((tm, d), BF16), pltpu.VMEM((tm, d), F32)],
        compiler_params=_cparams(("parallel", "arbitrary")),
        name="ffn",
    )(x, g.reshape(1, d), w1_bf16, w2_bf16, g_final.reshape(1, d))


def _attn_kernel(q_ref, k_ref, v_ref, o_ref, *, n_heads, dh):
    scale = dh ** -0.5
    for h in range(n_heads):
        sl = slice(h * dh, (h + 1) * dh)
        q = q_ref[0, :, sl].astype(BF16)
        k = k_ref[0, :, sl].astype(BF16)
        v = v_ref[0, :, sl].astype(BF16)
        s = _dot_nt(q, k) * scale
        m = jnp.max(s, axis=-1, keepdims=True)
        e = jnp.exp(s - m)
        l = jnp.sum(e, axis=-1, keepdims=True)
        o_ref[0, :, sl] = _dot(e.astype(BF16), v) / l


def mem_attention(q, mem_k, mem_v, tq):
    b, l, d = q.shape
    m = mem_k.shape[1]
    return pl.pallas_call(
        functools.partial(_attn_kernel, n_heads=MEM_HEADS, dh=d // MEM_HEADS),
        grid=(b, l // tq),
        in_specs=[pl.BlockSpec((1, tq, d), lambda i, j: (i, j, 0)),
                  pl.BlockSpec((1, m, d), lambda i, j: (i, 0, 0)),
                  pl.BlockSpec((1, m, d), lambda i, j: (i, 0, 0))],
        out_specs=pl.BlockSpec((1, tq, d), lambda i, j: (i, j, 0)),
        out_shape=jax.ShapeDtypeStruct((b, l, d), F32),
        compiler_params=_cparams(("parallel", "arbitrary")),
        name="mem_attention",
    )(q, mem_k, mem_v)


def _ret_kernel(proj_ref, cos_ref, sin_ref, s0_ref, dm_ref, xi_ref, zeta_ref, gc_ref, lnw_ref,
                o_ref, sout_ref, S_ref, *, C, n_pairs):
    t = pl.program_id(1)

    @pl.when(t == 0)
    def _():
        _load_pair_state(S_ref, s0_ref, n_pairs)

    m0 = _head0_lanes()
    first_half = (_lane_iota((1, PAIR)) & (HEAD_DIM - 1)) < HEAD_DIM // 2
    bm = _block_diag_mask()
    pmean = _group_mean_matrix(bm)
    cos = cos_ref[...]
    sin = sin_ref[...]
    lnw = lnw_ref[...]

    def rot(x):
        sw = jnp.where(first_half, pltpu.roll(x, PAIR - HEAD_DIM // 2, 1), pltpu.roll(x, HEAD_DIM // 2, 1))
        return x * cos + sw * sin

    for p in range(n_pairs):
        lanes = slice(p * PAIR, (p + 1) * PAIR)
        q = rot(proj_ref[0, :, p * PAIR:(p + 1) * PAIR])
        k = rot(proj_ref[0, :, W_RET + p * PAIR:W_RET + (p + 1) * PAIR]) * HEAD_DIM ** -0.5
        v = proj_ref[0, :, 2 * W_RET + p * PAIR:2 * W_RET + (p + 1) * PAIR]
        gate = proj_ref[0, :, 3 * W_RET + p * PAIR:3 * W_RET + (p + 1) * PAIR]
        kb = k.astype(BF16)
        vb = v.astype(BF16)
        sc = _dot_nt(_stack(q, m0).astype(BF16), kb) * dm_ref[p]
        o2 = _dot(sc.astype(BF16), vb)
        o_intra = jnp.where(m0, o2[:C], o2[C:])
        S = S_ref[p]
        o_inter = _dot(q.astype(BF16), S.astype(BF16)) * xi_ref[p]
        kv = _dot_tn((k * zeta_ref[p]).astype(BF16), vb)
        S_ref[p] = S * gc_ref[p][0:1, :] + jnp.where(bm, kv, 0.0)
        o = o_intra + o_inter
        oc = o - _dot(o.astype(BF16), pmean)
        var = _dot((oc * oc).astype(BF16), pmean)
        o_ref[0, :, lanes] = oc * lax.rsqrt(var + EPS) * lnw * _silu(gate)

    @pl.when(t == pl.num_programs(1) - 1)
    def _():
        _store_pair_state(sout_ref, S_ref, n_pairs)


def _ret_consts(C):
    lg = jnp.log1p(-jnp.exp2(-5.0 - jnp.arange(H_RET, dtype=F32)))
    idx = jnp.arange(C, dtype=F32)
    diff = idx[:, None] - idx[None, :]
    dmask = jnp.where(diff >= 0, jnp.exp(lg[:, None, None] * jnp.maximum(diff, 0.0)), 0.0)
    dm = dmask.reshape(H_RET // 2, 2 * C, C)
    xi = jnp.exp(lg[:, None] * (idx + 1.0))
    zeta = jnp.exp(lg[:, None] * (C - 1.0 - idx))
    gch = jnp.exp(lg * C)

    def pair_lanes(a):
        r = a.shape[1]
        a = jnp.repeat(a[:, :, None], HEAD_DIM, axis=2)
        a = a.reshape(H_RET // 2, 2, r, HEAD_DIM).transpose(0, 2, 1, 3)
        return a.reshape(H_RET // 2, r, PAIR)

    gc = pair_lanes(jnp.broadcast_to(gch[:, None], (H_RET, 8)))
    return dm, pair_lanes(xi), pair_lanes(zeta), gc


def _rope_tables(pos):
    half = HEAD_DIM // 2
    inv_freq = ROPE_BASE ** (-jnp.arange(half, dtype=F32) / half)
    ang = pos.astype(F32)[:, None] * inv_freq[None, :]
    cos, sin = jnp.cos(ang), jnp.sin(ang)
    cos_t = jnp.concatenate([cos, cos, cos, cos], axis=-1)
    sin_t = jnp.concatenate([-sin, sin, -sin, sin], axis=-1)
    return cos_t, sin_t


def retention(proj, pos, s0, ln_w, C):
    b, l, _ = proj.shape
    n_pairs = H_RET // 2
    dm, xi, zeta, gc = _ret_consts(C)
    cos_t, sin_t = _rope_tables(pos)
    lnw = jnp.concatenate([ln_w, ln_w]).reshape(1, PAIR)
    const3 = lambda shape: pl.BlockSpec(shape, lambda i, j: (0, 0, 0))
    return pl.pallas_call(
        functools.partial(_ret_kernel, C=C, n_pairs=n_pairs),
        grid=(b, l // C),
        in_specs=[pl.BlockSpec((1, C, RET_COLS), lambda i, j: (i, j, 0)),
                  pl.BlockSpec((C, PAIR), lambda i, j: (j, 0)),
                  pl.BlockSpec((C, PAIR), lambda i, j: (j, 0)),
                  pl.BlockSpec((1, H_RET, HEAD_DIM, HEAD_DIM), lambda i, j: (i, 0, 0, 0)),
                  const3((n_pairs, 2 * C, C)),
                  const3((n_pairs, C, PAIR)),
                  const3((n_pairs, C, PAIR)),
                  const3((n_pairs, 8, PAIR)),
                  pl.BlockSpec((1, PAIR), lambda i, j: (0, 0))],
        out_specs=[pl.BlockSpec((1, C, W_RET), lambda i, j: (i, j, 0)),
                   pl.BlockSpec((1, H_RET, HEAD_DIM, HEAD_DIM), lambda i, j: (i, 0, 0, 0))],
        out_shape=[jax.ShapeDtypeStruct((b, l, W_RET), F32),
                   jax.ShapeDtypeStruct((b, H_RET, HEAD_DIM, HEAD_DIM), F32)],
        scratch_shapes=[pltpu.VMEM((n_pairs, PAIR, PAIR), F32)],
        compiler_params=_cparams(("parallel", "arbitrary")),
        name="retention",
    )(proj, cos_t, sin_t, s0, dm, xi, zeta, gc, lnw)


def _neumann_inverse(a, c):
    n = a.shape[0]
    eye = (_row_iota((n, n)) == _lane_iota((n, n))).astype(F32)
    x = -a
    t = eye + x
    for _ in range(max(int(math.ceil(math.log2(c))) - 1, 0)):
        x = _dot3(x, x)
        t = t + _dot3(t, x)
    return t


def _dn_kernel(proj_ref, convw_ref, sconv_ref, alog_ref, dtb_ref, s0_ref, nw_ref,
               o_ref, sout_ref, convout_ref, S_ref, tail_ref, *, TT, c, n_pairs):
    t = pl.program_id(1)

    @pl.when(t == 0)
    def _():
        _load_pair_state(S_ref, s0_ref, n_pairs)
        tail_ref[...] = sconv_ref[0]

    m0 = _head0_lanes()
    bm = _block_diag_mask()
    pmean = _group_mean_matrix(bm)
    nw = nw_ref[...]

    x = proj_ref[0, :, 0:DN_CONV_CH]
    tail = tail_ref[...]
    rows8 = _row_iota((8, DN_CONV_CH))
    acc = x * convw_ref[CONV_W - 1:CONV_W, :]
    for s in range(1, CONV_W):
        xs = pltpu.roll(x, s, 0)
        top = jnp.where(rows8 < s, pltpu.roll(tail, s, 0), xs[0:8])
        xs = top if TT == 8 else jnp.concatenate([top, xs[8:]], axis=0)
        acc = acc + xs * convw_ref[CONV_W - 1 - s:CONV_W - s, :]
    new_tail = x[TT - 8:TT]
    tail_ref[...] = new_tail
    qkv = _silu(acc)

    misc = proj_ref[0, :, DN_CONV_CH + W_DN:DN_COLS]
    sig = jax.nn.sigmoid(misc)
    glog = -jnp.exp(alog_ref[...]) * _softplus(misc + dtb_ref[...])

    n2 = 2 * c
    r2 = _row_iota((n2, n2))
    c2 = _lane_iota((n2, n2))
    same = (r2 // c) == (c2 // c) if (c & (c - 1)) else (r2 >> int(math.log2(c))) == (c2 >> int(math.log2(c)))
    incl = same & (c2 <= r2)
    strict = same & (c2 < r2)
    tril = jnp.where(incl, 1.0, 0.0).astype(BF16)
    pick0 = jnp.where(_lane_iota((n2, PAIR)) == 0, 1.0, 0.0).astype(BF16)

    for p in range(n_pairs):
        lanes = slice(p * PAIR, (p + 1) * PAIR)
        q_all = qkv[:, p * PAIR:(p + 1) * PAIR]
        k_all = qkv[:, W_DN + p * PAIR:W_DN + (p + 1) * PAIR]
        v_all = qkv[:, 2 * W_DN + p * PAIR:2 * W_DN + (p + 1) * PAIR]
        q_all = q_all * lax.rsqrt(_dot((q_all * q_all).astype(BF16), pmean) * HEAD_DIM + EPS) * HEAD_DIM ** -0.5
        k_all = k_all * lax.rsqrt(_dot((k_all * k_all).astype(BF16), pmean) * HEAD_DIM + EPS)
        S = S_ref[p]
        outs = []
        for ci in range(TT // c):
            rows = slice(ci * c, (ci + 1) * c)
            q, k, v = q_all[rows], k_all[rows], v_all[rows]
            h0, h1 = 2 * p, 2 * p + 1
            beta = jnp.where(m0, jnp.broadcast_to(sig[rows, h0:h0 + 1], (c, PAIR)),
                             jnp.broadcast_to(sig[rows, h1:h1 + 1], (c, PAIR)))
            g_st = jnp.concatenate(
                [jnp.broadcast_to(glog[rows, DN_A_LANE + h0:DN_A_LANE + h0 + 1], (c, PAIR)),
                 jnp.broadcast_to(glog[rows, DN_A_LANE + h1:DN_A_LANE + h1 + 1], (c, PAIR))], axis=0)
            cum = _dot_exact_lhs(tril, g_st)
            cum_row = _dot_nt_exact(pick0, cum)
            diff = cum[:, 0:n2] - cum_row
            gam = jnp.where(incl, jnp.exp(jnp.minimum(diff, 0.0)), 0.0)
            k_st = _stack(k, m0)
            kb_st = _stack(k * beta, m0)
            q_st = _stack(q, m0)
            k_stb = k_st.astype(BF16)
            a = jnp.where(strict, _dot_nt(kb_st.astype(BF16), k_stb) * gam, 0.0)
            attn = _dot_nt(q_st.astype(BF16), k_stb) * gam
            tinv = _neumann_inverse(a, c).astype(BF16)
            ecum = jnp.exp(cum)
            u = _dot(tinv, _stack(v * beta, m0).astype(BF16))
            w = _dot(tinv, (kb_st * ecum).astype(BF16))
            q_dec = q_st * ecum
            last0 = cum[c - 1:c, :]
            last1 = cum[n2 - 1:n2, :]
            last = jnp.concatenate([jnp.broadcast_to(last0, (c, PAIR)), jnp.broadcast_to(last1, (c, PAIR))], axis=0)
            k_dec = k_st * jnp.exp(last - cum)
            dec = jnp.where(m0, jnp.exp(last0), jnp.exp(last1))
            Sb = S.astype(BF16)
            v_new = u - _dot(w.astype(BF16), Sb)
            v_newb = v_new.astype(BF16)
            o2 = _dot(q_dec.astype(BF16), Sb) + _dot(attn.astype(BF16), v_newb)
            S = S * dec + _dot_tn(k_dec.astype(BF16), v_newb)
            outs.append(o2[:c] + o2[c:])
        S_ref[p] = S
        o = outs[0] if len(outs) == 1 else jnp.concatenate(outs, axis=0)
        gate = proj_ref[0, :, DN_CONV_CH + p * PAIR:DN_CONV_CH + (p + 1) * PAIR]
        ms = _dot((o * o).astype(BF16), pmean)
        o_ref[0, :, lanes] = o * lax.rsqrt(ms + EPS) * nw * _silu(gate)

    @pl.when(t == pl.num_programs(1) - 1)
    def _():
        _store_pair_state(sout_ref, S_ref, n_pairs)
        convout_ref[0] = new_tail


def _dot_nt_exact(a_bf16, b):
    hi, mid, lo = _split3(b)
    return _dot_nt(a_bf16, hi) + _dot_nt(a_bf16, mid) + _dot_nt(a_bf16, lo)


def delta_net(proj, conv_w, s_conv, a_log, dt_bias, s0, norm_w, TT, c):
    b, l, _ = proj.shape
    n_pairs = H_DN // 2
    sconv8 = jnp.pad(s_conv, ((0, 0), (8 - (CONV_W - 1), 0), (0, 0)))
    alog = jnp.zeros((1, PAIR), F32).at[0, DN_A_LANE:DN_A_LANE + H_DN].set(a_log)
    dtb = jnp.zeros((1, PAIR), F32).at[0, DN_A_LANE:DN_A_LANE + H_DN].set(dt_bias)
    nw = jnp.concatenate([norm_w, norm_w]).reshape(1, PAIR)
    o, s_new, conv8 = pl.pallas_call(
        functools.partial(_dn_kernel, TT=TT, c=c, n_pairs=n_pairs),
        grid=(b, l // TT),
        in_specs=[pl.BlockSpec((1, TT, DN_COLS), lambda i, j: (i, j, 0)),
                  pl.BlockSpec((CONV_W, DN_CONV_CH), lambda i, j: (0, 0)),
                  pl.BlockSpec((1, 8, DN_CONV_CH), lambda i, j: (i, 0, 0)),
                  pl.BlockSpec((1, PAIR), lambda i, j: (0, 0)),
                  pl.BlockSpec((1, PAIR), lambda i, j: (0, 0)),
                  pl.BlockSpec((1, H_DN, HEAD_DIM, HEAD_DIM), lambda i, j: (i, 0, 0, 0)),
                  pl.BlockSpec((1, PAIR), lambda i, j: (0, 0))],
        out_specs=[pl.BlockSpec((1, TT, W_DN), lambda i, j: (i, j, 0)),
                   pl.BlockSpec((1, H_DN, HEAD_DIM, HEAD_DIM), lambda i, j: (i, 0, 0, 0)),
                   pl.BlockSpec((1, 8, DN_CONV_CH), lambda i, j: (i, 0, 0))],
        out_shape=[jax.ShapeDtypeStruct((b, l, W_DN), F32),
                   jax.ShapeDtypeStruct((b, H_DN, HEAD_DIM, HEAD_DIM), F32),
                   jax.ShapeDtypeStruct((b, 8, DN_CONV_CH), F32)],
        scratch_shapes=[pltpu.VMEM((n_pairs, PAIR, PAIR), F32), pltpu.VMEM((8, DN_CONV_CH), F32)],
        compiler_params=_cparams(("parallel", "arbitrary")),
        name="delta_net",
    )(proj, conv_w, sconv8, alog, dtb, s0, nw)
    return o, s_new, conv8[:, 8 - (CONV_W - 1):, :]


def _gla_kernel(proj_ref, wa2_ref, ba_ref, s0_ref, nw_ref, o_ref, sout_ref, ST_ref, *, TT, m, cs, n_pairs):
    t = pl.program_id(1)
    bm = _block_diag_mask()

    @pl.when(t == 0)
    def _():
        _load_pair_state(ST_ref, s0_ref, n_pairs)
        for p in range(n_pairs):
            ST_ref[p] = ST_ref[p].T

    m0 = _head0_lanes()
    pmean = _group_mean_matrix(bm)
    nw = nw_ref[...]
    ng = m // cs
    n2 = 2 * m
    sh = int(math.log2(cs))
    r2 = _row_iota((n2, n2))
    c2 = _lane_iota((n2, n2))
    incl2 = ((r2 >> sh) == (c2 >> sh)) & (c2 <= r2)
    r1 = _row_iota((m, m))
    c1 = _lane_iota((m, m))
    same1 = (r1 >> sh) == (c1 >> sh)
    tril = jnp.where(same1 & (c1 <= r1), 1.0, 0.0).astype(BF16)
    ones_blk = jnp.where(same1, 1.0, 0.0).astype(BF16)
    chunk_of_row = _row_iota((m, PAIR)) >> sh

    a1 = proj_ref[0, :, 4 * W_GLA:GLA_COLS].astype(BF16)
    for p in range(n_pairs):
        lanes = slice(p * PAIR, (p + 1) * PAIR)
        logit = _dot(a1, wa2_ref[:, lanes]) + ba_ref[:, lanes]
        g_all = -_softplus(-logit) * (1.0 / GLA_TAU)
        q_all = proj_ref[0, :, p * PAIR:(p + 1) * PAIR] * HEAD_DIM ** -0.5
        k_all = proj_ref[0, :, W_GLA + p * PAIR:W_GLA + (p + 1) * PAIR]
        v_all = proj_ref[0, :, 2 * W_GLA + p * PAIR:2 * W_GLA + (p + 1) * PAIR]
        ST = ST_ref[p]
        outs = []
        for gi in range(TT // m):
            rows = slice(gi * m, (gi + 1) * m)
            q, k, v, g = q_all[rows], k_all[rows], v_all[rows], g_all[rows]
            cum = _dot_exact_lhs(tril, g)
            last = _dot_exact_lhs(ones_blk, g)
            qt = q * jnp.exp(cum)
            kt = k * jnp.exp(-cum)
            kdec = (k * jnp.exp(last - cum)).astype(BF16)
            sc = jnp.where(incl2, _dot_nt(_stack(qt, m0).astype(BF16), _stack(kt, m0).astype(BF16)), 0.0)
            o2 = _dot(sc.astype(BF16), _stack(v, m0).astype(BF16))
            o_intra = o2[:m] + o2[m:]
            qtb = qt.astype(BF16)
            if ng == 1:
                kd_big, q_big = kdec, qtb
            else:
                kd_big = jnp.concatenate([jnp.where(chunk_of_row == n, kdec, 0) for n in range(ng)], axis=1)
                q_big = jnp.concatenate([jnp.where(chunk_of_row == n, qtb, 0) for n in range(ng)], axis=1)
            kvt = _dot_tn(v.astype(BF16), kd_big)
            sts = []
            for n in range(ng):
                sts.append(ST.astype(BF16))
                dec = jnp.exp(last[n * cs:n * cs + 1, :])
                ST = ST * dec + jnp.where(bm, kvt[:, n * PAIR:(n + 1) * PAIR], 0.0)
            st_all = sts[0] if ng == 1 else jnp.concatenate(sts, axis=1)
            outs.append(o_intra + _dot_nt(q_big, st_all))
        ST_ref[p] = ST
        o = outs[0] if len(outs) == 1 else jnp.concatenate(outs, axis=0)
        gate = proj_ref[0, :, 3 * W_GLA + p * PAIR:3 * W_GLA + (p + 1) * PAIR]
        ms = _dot((o * o).astype(BF16), pmean)
        o_ref[0, :, lanes] = o * lax.rsqrt(ms + EPS) * nw * _silu(gate)

    @pl.when(t == pl.num_programs(1) - 1)
    def _():
        for p in range(n_pairs):
            s = ST_ref[p].T
            sout_ref[0, 2 * p] = s[0:HEAD_DIM, 0:HEAD_DIM]
            sout_ref[0, 2 * p + 1] = s[HEAD_DIM:PAIR, HEAD_DIM:PAIR]


def gla(proj, w_a2, b_a, s0, norm_w, TT, m, cs):
    b, l, _ = proj.shape
    n_pairs = H_GLA // 2
    wa2 = jnp.zeros((PAIR, W_GLA), F32).at[:GLA_RANK].set(w_a2).astype(BF16)
    nw = jnp.concatenate([norm_w, norm_w]).reshape(1, PAIR)
    return pl.pallas_call(
        functools.partial(_gla_kernel, TT=TT, m=m, cs=cs, n_pairs=n_pairs),
        grid=(b, l // TT),
        in_specs=[pl.BlockSpec((1, TT, GLA_COLS), lambda i, j: (i, j, 0)),
                  pl.BlockSpec((PAIR, W_GLA), lambda i, j: (0, 0)),
                  pl.BlockSpec((1, W_GLA), lambda i, j: (0, 0)),
                  pl.BlockSpec((1, H_GLA, HEAD_DIM, HEAD_DIM), lambda i, j: (i, 0, 0, 0)),
                  pl.BlockSpec((1, PAIR), lambda i, j: (0, 0))],
        out_specs=[pl.BlockSpec((1, TT, W_GLA), lambda i, j: (i, j, 0)),
                   pl.BlockSpec((1, H_GLA, HEAD_DIM, HEAD_DIM), lambda i, j: (i, 0, 0, 0))],
        out_shape=[jax.ShapeDtypeStruct((b, l, W_GLA), F32),
                   jax.ShapeDtypeStruct((b, H_GLA, HEAD_DIM, HEAD_DIM), F32)],
        scratch_shapes=[pltpu.VMEM((n_pairs, PAIR, PAIR), F32)],
        compiler_params=_cparams(("parallel", "arbitrary")),
        name="gla",
    )(proj, wa2, b_a.reshape(1, W_GLA), s0, nw)


def _prep_layer_weights(li, w_in, w_out, w_mq, w_mo, w_ff1, w_ff2):
    wi = w_in[li]
    d = wi.shape[0]
    c0 = RET_COLS
    c1 = c0 + DN_CONV_CH + W_DN + 2 * H_DN
    w_ret = wi[:, :c0]
    w_dn = jnp.concatenate([wi[:, c0:c1], jnp.zeros((d, DN_COLS - (c1 - c0)), F32)], axis=1)
    w_gla = jnp.concatenate([wi[:, c1:], jnp.zeros((d, GLA_COLS - (wi.shape[1] - c1)), F32)], axis=1)
    cast = lambda a: a.astype(BF16)
    return dict(w_ret=cast(w_ret), w_dn=cast(w_dn), w_gla=cast(w_gla), w_out=cast(w_out[li]),
                w_mq=cast(w_mq[li]), w_mo=cast(w_mo[li]), w_ff1=cast(w_ff1[li]), w_ff2=cast(w_ff2[li]))


def _run_trunk(x, pos, s_ret, s_dn, s_conv, s_gla, mem_k, mem_v, lw, p, cfg):
    b, l, d = x.shape
    t = b * l
    depth = len(lw)
    xf = x.reshape(t, d)
    new_ret, new_dn, new_conv, new_gla = [], [], [], []
    for li in range(depth):
        w = lw[li]
        g = p['norm_mix'][li]
        pr = rms_matmul(xf, g, w['w_ret'], cfg['tm']).reshape(b, l, RET_COLS)
        pd = rms_matmul(xf, g, w['w_dn'], cfg['tm']).reshape(b, l, DN_COLS)
        pg = rms_matmul(xf, g, w['w_gla'], cfg['tm']).reshape(b, l, GLA_COLS)
        o_ret, r_s = retention(pr, pos, s_ret[li], p['ret_norm_w'][li], cfg['ret_c'])
        o_dn, d_s, c_s = delta_net(pd, p['dn_conv_w'][li], s_conv[li], p['dn_a_log'][li], p['dn_dt_bias'][li],
                                   s_dn[li], p['dn_norm_w'][li], cfg['dn_tt'], cfg['dn_c'])
        o_gla, g_s = gla(pg, p['gla_w_a2'][li], p['gla_b_a'][li], s_gla[li], p['gla_norm_w'][li],
                         cfg['gla_tt'], cfg['gla_m'], cfg['gla_cs'])
        xf = matmul_res([o_ret.reshape(t, W_RET), o_dn.reshape(t, W_DN), o_gla.reshape(t, W_GLA)],
                        w['w_out'], xf, cfg['tm'])
        qm = rms_matmul(xf, p['norm_cross'][li], w['w_mq'], cfg['tm']).reshape(b, l, d)
        att = mem_attention(qm, mem_k[li], mem_v[li], cfg['tq'])
        xf = matmul_res([att.reshape(t, d)], w['w_mo'], xf, cfg['tm'])
        xf = ffn(xf, p['norm_ffn'][li], w['w_ff1'], w['w_ff2'], p['norm_final'], li == depth - 1,
                 cfg['ffn_tm'], cfg['ffn_tf'])
        new_ret.append(r_s)
        new_dn.append(d_s)
        new_conv.append(c_s)
        new_gla.append(g_s)
    return (xf.reshape(b, l, d), jnp.stack(new_ret), jnp.stack(new_dn), jnp.stack(new_conv), jnp.stack(new_gla))


def kernel(x_prompt, x_sample, state_ret, state_dn, state_dn_conv, state_gla, cache_mem_k, cache_mem_v, mem_prompt, norm_mix, w_in, ret_norm_w, dn_conv_w, dn_a_log, dn_dt_bias, dn_norm_w, gla_w_a2, gla_b_a, gla_norm_w, w_out, norm_cross, norm_mem, w_mq, w_mk, w_mv, w_mo, norm_ffn, w_ff1, w_ff2, norm_final):
    b_p, l_p, d = x_prompt.shape
    b_s, l_s, _ = x_sample.shape
    depth = w_in.shape[0]
    n_mem = mem_prompt.shape[1]
    p = dict(norm_mix=norm_mix, ret_norm_w=ret_norm_w, dn_conv_w=dn_conv_w, dn_a_log=dn_a_log,
             dn_dt_bias=dn_dt_bias, dn_norm_w=dn_norm_w, gla_w_a2=gla_w_a2, gla_b_a=gla_b_a,
             gla_norm_w=gla_norm_w, norm_cross=norm_cross, norm_ffn=norm_ffn, norm_final=norm_final)
    lw = [_prep_layer_weights(li, w_in, w_out, w_mq, w_mo, w_ff1, w_ff2) for li in range(depth)]

    memf = mem_prompt.reshape(b_p * n_mem, d)
    mk = jnp.stack([rms_matmul(memf, norm_mem[li], w_mk[li].astype(BF16), 512) for li in range(depth)])
    mv = jnp.stack([rms_matmul(memf, norm_mem[li], w_mv[li].astype(BF16), 512) for li in range(depth)])
    mk = mk.reshape(depth, b_p, n_mem, d)
    mv = mv.reshape(depth, b_p, n_mem, d)

    zeros = lambda *s: jnp.zeros(s, F32)
    cfg_p = dict(tm=512, ret_c=256, dn_tt=256, dn_c=math.gcd(l_p, DN_CHUNK), gla_tt=256, gla_m=64,
                 gla_cs=math.gcd(l_p, GLA_CHUNK), tq=512, ffn_tm=1024, ffn_tf=512)
    y_p, p_ret, p_dn, p_conv, p_gla = _run_trunk(
        x_prompt, jnp.arange(l_p, dtype=jnp.int32),
        zeros(depth, b_p, H_RET, HEAD_DIM, HEAD_DIM), zeros(depth, b_p, H_DN, HEAD_DIM, HEAD_DIM),
        zeros(depth, b_p, CONV_W - 1, DN_CONV_CH), zeros(depth, b_p, H_GLA, HEAD_DIM, HEAD_DIM),
        mk, mv, lw, p, cfg_p)

    cfg_s = dict(tm=512, ret_c=l_s, dn_tt=l_s, dn_c=math.gcd(l_s, DN_CHUNK), gla_tt=l_s,
                 gla_m=math.gcd(l_s, GLA_CHUNK), gla_cs=math.gcd(l_s, GLA_CHUNK), tq=l_s, ffn_tm=1024, ffn_tf=512)
    y_s, s_ret, s_dn, s_conv, s_gla = _run_trunk(
        x_sample, PAST_LEN + jnp.arange(l_s, dtype=jnp.int32), state_ret, state_dn, state_dn_conv, state_gla,
        cache_mem_k.reshape(depth, b_s, n_mem, d), cache_mem_v.reshape(depth, b_s, n_mem, d), lw, p, cfg_s)

    mem_shape = (depth, b_p, n_mem, MEM_HEADS, d // MEM_HEADS)
    return (y_p, y_s, p_ret, p_dn, p_conv, p_gla, mk.reshape(mem_shape), mv.reshape(mem_shape),
            s_ret, s_dn, s_conv, s_gla)
```

```python
import functools
import math

import jax
import jax.numpy as jnp
from jax import lax
from jax.experimental import pallas as pl
from jax.experimental.pallas import tpu as pltpu

F32 = jnp.float32
BF16 = jnp.bfloat16

HEAD_DIM = 64
PAIR = 2 * HEAD_DIM
H_RET, H_DN, H_GLA = 6, 6, 4
W_RET, W_DN, W_GLA = H_RET * HEAD_DIM, H_DN * HEAD_DIM, H_GLA * HEAD_DIM
CONV_W = 4
DN_CONV_CH = 3 * W_DN
GLA_RANK = 16
GLA_TAU = 16.0
MEM_HEADS = 4
DN_CHUNK = 64
GLA_CHUNK = 16
ROPE_BASE = 10000.0
EPS = 1e-6
PAST_LEN = 16384

DN_COLS = DN_CONV_CH + W_DN + PAIR
GLA_COLS = 4 * W_GLA + PAIR
RET_COLS = 4 * W_RET
DN_A_LANE = H_DN
TRI_BASE = 16

VMEM_LIMIT = 56 * 1024 * 1024


def _cparams(sem):
    return pltpu.CompilerParams(dimension_semantics=sem, vmem_limit_bytes=VMEM_LIMIT)


def _log2(n):
    l = int(math.log2(n))
    assert (1 << l) == n, n
    return l


def _dot(a, b):
    return jnp.dot(a, b, preferred_element_type=F32)


def _dot_nt(a, b):
    return lax.dot_general(a, b, (((1,), (1,)), ((), ())), preferred_element_type=F32)


def _dot_tn(a, b):
    return lax.dot_general(a, b, (((0,), (0,)), ((), ())), preferred_element_type=F32)


def _split2(a):
    hi = a.astype(BF16)
    lo = (a - hi.astype(F32)).astype(BF16)
    return hi, lo


def _split3(a):
    hi = a.astype(BF16)
    r = a - hi.astype(F32)
    mid = r.astype(BF16)
    lo = (r - mid.astype(F32)).astype(BF16)
    return hi, mid, lo


def _dot_exact_lhs(a_bf16, b):
    hi, mid, lo = _split3(b)
    return _dot(a_bf16, hi) + _dot(a_bf16, mid) + _dot(a_bf16, lo)


def _dot_nt_exact(a_bf16, b):
    hi, mid, lo = _split3(b)
    return _dot_nt(a_bf16, hi) + _dot_nt(a_bf16, mid) + _dot_nt(a_bf16, lo)


def _silu(x):
    return x * jax.nn.sigmoid(x)


def _softplus(x):
    return jnp.maximum(x, 0.0) + jnp.log1p(jnp.exp(-jnp.abs(x)))


def _lane_iota(shape):
    return lax.broadcasted_iota(jnp.int32, shape, len(shape) - 1)


def _row_iota(shape):
    return lax.broadcasted_iota(jnp.int32, shape, 0)


def _head0_lanes():
    return _lane_iota((1, PAIR)) < HEAD_DIM


def _stack(x, lo):
    return jnp.concatenate([jnp.where(lo, x, 0.0), jnp.where(lo, 0.0, x)], axis=0)


def _block_diag_mask():
    r = _row_iota((PAIR, PAIR))
    c = _lane_iota((PAIR, PAIR))
    return (r >> 6) == (c >> 6)


def _group_mean_matrix(bm):
    return jnp.where(bm, 1.0 / HEAD_DIM, 0.0).astype(BF16)


def _load_pair_state(S_ref, s0_ref, n_pairs):
    S_ref[...] = jnp.zeros_like(S_ref)
    if s0_ref is not None:
        for p in range(n_pairs):
            S_ref[p, 0:HEAD_DIM, 0:HEAD_DIM] = s0_ref[0, 2 * p]
            S_ref[p, HEAD_DIM:PAIR, HEAD_DIM:PAIR] = s0_ref[0, 2 * p + 1]


def _store_pair_state(sout_ref, S_ref, n_pairs):
    for p in range(n_pairs):
        sout_ref[0, 2 * p] = S_ref[p, 0:HEAD_DIM, 0:HEAD_DIM]
        sout_ref[0, 2 * p + 1] = S_ref[p, HEAD_DIM:PAIR, HEAD_DIM:PAIR]


def _split_refs(refs, n_in, has_state, n_prev):
    ins = refs[:n_in]
    refs = refs[n_in:]
    return ins, refs[n_prev:]


def _rms_matmul_kernel(x_ref, g_ref, w_ref, o_ref):
    x = x_ref[...]
    xn = x * lax.rsqrt(jnp.mean(x * x, axis=-1, keepdims=True) + EPS) * g_ref[...]
    o_ref[...] = _dot(xn.astype(BF16), w_ref[...])


def rms_matmul(x, g, w_bf16, tm):
    t, d = x.shape
    n = w_bf16.shape[1]
    return pl.pallas_call(
        _rms_matmul_kernel,
        grid=(t // tm,),
        in_specs=[pl.BlockSpec((tm, d), lambda i: (i, 0)),
                  pl.BlockSpec((1, d), lambda i: (0, 0)),
                  pl.BlockSpec((d, n), lambda i: (0, 0))],
        out_specs=pl.BlockSpec((tm, n), lambda i: (i, 0)),
        out_shape=jax.ShapeDtypeStruct((t, n), F32),
        compiler_params=_cparams(("parallel",)),
        name="rms_matmul",
    )(x, g.reshape(1, d), w_bf16)


def _matmul_res_kernel(*refs, widths):
    n_in = len(widths)
    a_refs = refs[:n_in]
    w_ref, r_ref, o_ref = refs[n_in:]
    acc = r_ref[...]
    off = 0
    for a_ref, wk in zip(a_refs, widths):
        acc = acc + _dot(a_ref[...].astype(BF16), w_ref[off:off + wk, :])
        off += wk
    o_ref[...] = acc


def matmul_res(a_list, w_bf16, res, tm):
    t, n = res.shape
    widths = tuple(a.shape[1] for a in a_list)
    k = w_bf16.shape[0]
    assert sum(widths) == k
    in_specs = [pl.BlockSpec((tm, wk), lambda i: (i, 0)) for wk in widths]
    in_specs += [pl.BlockSpec((k, n), lambda i: (0, 0)), pl.BlockSpec((tm, n), lambda i: (i, 0))]
    return pl.pallas_call(
        functools.partial(_matmul_res_kernel, widths=widths),
        grid=(t // tm,),
        in_specs=in_specs,
        out_specs=pl.BlockSpec((tm, n), lambda i: (i, 0)),
        out_shape=jax.ShapeDtypeStruct((t, n), F32),
        compiler_params=_cparams(("parallel",)),
        name="matmul_res",
    )(*a_list, w_bf16, res)


def _ffn_kernel(x_ref, g_ref, w1_ref, w2_ref, gf_ref, o_ref, xn_ref, acc_ref, *, final_norm):
    j = pl.program_id(1)

    @pl.when(j == 0)
    def _():
        x = x_ref[...]
        xn = x * lax.rsqrt(jnp.mean(x * x, axis=-1, keepdims=True) + EPS) * g_ref[...]
        xn_ref[...] = xn.astype(BF16)
        acc_ref[...] = x

    h = _dot(xn_ref[...], w1_ref[...])
    h = jnp.square(jnp.maximum(h, 0.0))
    acc_ref[...] += _dot(h.astype(BF16), w2_ref[...])

    @pl.when(j == pl.num_programs(1) - 1)
    def _():
        y = acc_ref[...]
        if final_norm:
            y = y * lax.rsqrt(jnp.mean(y * y, axis=-1, keepdims=True) + EPS) * gf_ref[...]
        o_ref[...] = y


def ffn(x, g, w1_bf16, w2_bf16, g_final, final_norm, tm, tf):
    t, d = x.shape
    f = w1_bf16.shape[1]
    return pl.pallas_call(
        functools.partial(_ffn_kernel, final_norm=final_norm),
        grid=(t // tm, f // tf),
        in_specs=[pl.BlockSpec((tm, d), lambda i, j: (i, 0)),
                  pl.BlockSpec((1, d), lambda i, j: (0, 0)),
                  pl.BlockSpec((d, tf), lambda i, j: (0, j)),
                  pl.BlockSpec((tf, d), lambda i, j: (j, 0)),
                  pl.BlockSpec((1, d), lambda i, j: (0, 0))],
        out_specs=pl.BlockSpec((tm, d), lambda i, j: (i, 0)),
        out_shape=jax.ShapeDtypeStruct((t, d), F32),
        scratch_shapes=[pltpu.VMEM((tm, d), BF16), pltpu.VMEM((tm, d), F32)],
        compiler_params=_cparams(("parallel", "arbitrary")),
        name="ffn",
    )(x, g.reshape(1, d), w1_bf16, w2_bf16, g_final.reshape(1, d))


def _attn_kernel(q_ref, k_ref, v_ref, o_ref, *, n_heads, dh, split_heads):
    scale = dh ** -0.5
    for h in range(n_heads):
        sl = slice(h * dh, (h + 1) * dh)
        q = q_ref[0, :, sl].astype(BF16)
        if split_heads:
            k = k_ref[:, h, :].astype(BF16)
            v = v_ref[:, h, :].astype(BF16)
        else:
            k = k_ref[0, :, sl].astype(BF16)
            v = v_ref[0, :, sl].astype(BF16)
        s = _dot_nt(q, k) * scale
        m = jnp.max(s, axis=-1, keepdims=True)
        e = jnp.exp(s - m)
        l = jnp.sum(e, axis=-1, keepdims=True)
        o_ref[0, :, sl] = _dot(e.astype(BF16), v) / l


def mem_attention(q, mem_k, mem_v, tq, layer=None):
    b, l, d = q.shape
    dh = d // MEM_HEADS
    if layer is None:
        m = mem_k.shape[1]
        mem_spec = pl.BlockSpec((1, m, d), lambda i, j: (i, 0, 0))
    else:
        m = mem_k.shape[2]
        mem_spec = pl.BlockSpec((None, None, m, MEM_HEADS, dh), lambda i, j: (layer, i, 0, 0, 0))
    return pl.pallas_call(
        functools.partial(_attn_kernel, n_heads=MEM_HEADS, dh=dh, split_heads=layer is not None),
        grid=(b, l // tq),
        in_specs=[pl.BlockSpec((1, tq, d), lambda i, j: (i, j, 0)), mem_spec, mem_spec],
        out_specs=pl.BlockSpec((1, tq, d), lambda i, j: (i, j, 0)),
        out_shape=jax.ShapeDtypeStruct((b, l, d), F32),
        compiler_params=_cparams(("parallel", "arbitrary")),
        name="mem_attention",
    )(q, mem_k, mem_v)


def _state_specs(shape_tail, layer, nb):
    nd = len(shape_tail)
    return pl.BlockSpec((None, nb) + tuple(shape_tail), lambda i, j: (layer, i) + (0,) * nd)


_ANY_SPEC = pl.BlockSpec(memory_space=pl.ANY)


def _batch_view(ref, i):
    return None if ref is None else ref.at[pl.ds(i, 1)]


def _ret_kernel(*refs, C, n_pairs, has_state, n_prev, nb):
    n_in = 8 + (1 if has_state else 0)
    (proj_ref, cos_ref, sin_ref, dm_ref, xi_ref, zeta_ref, gc_ref, lnw_ref, *st), rest = \
        _split_refs(refs, n_in, has_state, n_prev)
    s0_ref = st[0] if has_state else None
    o_ref, sout_ref, S_ref = rest
    t = pl.program_id(1)

    @pl.when(t == 0)
    def _():
        for i in range(nb):
            _load_pair_state(S_ref.at[i], _batch_view(s0_ref, i), n_pairs)

    for i in range(nb):
        _ret_body(_batch_view(proj_ref, i), cos_ref, sin_ref, dm_ref, xi_ref, zeta_ref, gc_ref, lnw_ref,
                  _batch_view(o_ref, i), S_ref.at[i], C=C, n_pairs=n_pairs)

    @pl.when(t == pl.num_programs(1) - 1)
    def _():
        for i in range(nb):
            _store_pair_state(_batch_view(sout_ref, i), S_ref.at[i], n_pairs)


def _ret_body(proj_ref, cos_ref, sin_ref, dm_ref, xi_ref, zeta_ref, gc_ref, lnw_ref, o_ref, S_ref, *, C, n_pairs):
    m0 = _head0_lanes()
    first_half = (_lane_iota((1, PAIR)) & (HEAD_DIM - 1)) < HEAD_DIM // 2
    bm = _block_diag_mask()
    pmean = _group_mean_matrix(bm)
    cos = cos_ref[...]
    sin = sin_ref[...]
    lnw = lnw_ref[...]

    def rot(x):
        sw = jnp.where(first_half, pltpu.roll(x, PAIR - HEAD_DIM // 2, 1), pltpu.roll(x, HEAD_DIM // 2, 1))
        return x * cos + sw * sin

    for p in range(n_pairs):
        lanes = slice(p * PAIR, (p + 1) * PAIR)
        q = rot(proj_ref[0, :, p * PAIR:(p + 1) * PAIR])
        k = rot(proj_ref[0, :, W_RET + p * PAIR:W_RET + (p + 1) * PAIR]) * HEAD_DIM ** -0.5
        v = proj_ref[0, :, 2 * W_RET + p * PAIR:2 * W_RET + (p + 1) * PAIR]
        gate = proj_ref[0, :, 3 * W_RET + p * PAIR:3 * W_RET + (p + 1) * PAIR]
        kb = k.astype(BF16)
        vb = v.astype(BF16)
        sc = _dot_nt(_stack(q, m0).astype(BF16), kb) * dm_ref[p]
        o2 = _dot(sc.astype(BF16), vb)
        o_intra = jnp.where(m0, o2[:C], o2[C:])
        S = S_ref[p]
        o_inter = _dot(q.astype(BF16), S.astype(BF16)) * xi_ref[p]
        kv = _dot_tn((k * zeta_ref[p]).astype(BF16), vb)
        S_ref[p] = S * gc_ref[p][0:1, :] + jnp.where(bm, kv, 0.0)
        o = o_intra + o_inter
        oc = o - _dot(o.astype(BF16), pmean)
        var = _dot((oc * oc).astype(BF16), pmean)
        o_ref[0, :, lanes] = oc * lax.rsqrt(var + EPS) * lnw * _silu(gate)


def _ret_consts(C):
    lg = jnp.log1p(-jnp.exp2(-5.0 - jnp.arange(H_RET, dtype=F32)))
    idx = jnp.arange(C, dtype=F32)
    diff = idx[:, None] - idx[None, :]
    dmask = jnp.where(diff >= 0, jnp.exp(lg[:, None, None] * jnp.maximum(diff, 0.0)), 0.0)
    dm = dmask.reshape(H_RET // 2, 2 * C, C)
    xi = jnp.exp(lg[:, None] * (idx + 1.0))
    zeta = jnp.exp(lg[:, None] * (C - 1.0 - idx))
    gch = jnp.exp(lg * C)

    def pair_lanes(a):
        r = a.shape[1]
        a = jnp.repeat(a[:, :, None], HEAD_DIM, axis=2)
        a = a.reshape(H_RET // 2, 2, r, HEAD_DIM).transpose(0, 2, 1, 3)
        return a.reshape(H_RET // 2, r, PAIR)

    gc = pair_lanes(jnp.broadcast_to(gch[:, None], (H_RET, 8)))
    return dm, pair_lanes(xi), pair_lanes(zeta), gc


def _rope_tables(pos):
    half = HEAD_DIM // 2
    inv_freq = ROPE_BASE ** (-jnp.arange(half, dtype=F32) / half)
    ang = pos.astype(F32)[:, None] * inv_freq[None, :]
    cos, sin = jnp.cos(ang), jnp.sin(ang)
    cos_t = jnp.concatenate([cos, cos, cos, cos], axis=-1)
    sin_t = jnp.concatenate([-sin, sin, -sin, sin], axis=-1)
    return cos_t, sin_t


def retention(proj, tables, s0_all, layer, depth, prev, ln_w, C, nb):
    b, l, _ = proj.shape
    n_pairs = H_RET // 2
    dm, xi, zeta, gc, cos_t, sin_t = tables
    lnw = jnp.concatenate([ln_w, ln_w]).reshape(1, PAIR)
    const3 = lambda shape: pl.BlockSpec(shape, lambda i, j: (0, 0, 0))
    st_tail = (H_RET, HEAD_DIM, HEAD_DIM)
    in_specs = [pl.BlockSpec((nb, C, RET_COLS), lambda i, j: (i, j, 0)),
                pl.BlockSpec((C, PAIR), lambda i, j: (j, 0)),
                pl.BlockSpec((C, PAIR), lambda i, j: (j, 0)),
                const3((n_pairs, 2 * C, C)),
                const3((n_pairs, C, PAIR)),
                const3((n_pairs, C, PAIR)),
                const3((n_pairs, 8, PAIR)),
                pl.BlockSpec((1, PAIR), lambda i, j: (0, 0))]
    args = [proj, cos_t, sin_t, dm, xi, zeta, gc, lnw]
    if s0_all is not None:
        in_specs.append(_state_specs(st_tail, layer, nb))
        args.append(s0_all)
    aliases = {}
    if prev is not None:
        aliases[len(args)] = 1
        in_specs.append(_ANY_SPEC)
        args.append(prev)
    return pl.pallas_call(
        functools.partial(_ret_kernel, C=C, n_pairs=n_pairs, has_state=s0_all is not None,
                          n_prev=0 if prev is None else 1, nb=nb),
        grid=(b // nb, l // C),
        in_specs=in_specs,
        out_specs=[pl.BlockSpec((nb, C, W_RET), lambda i, j: (i, j, 0)), _state_specs(st_tail, layer, nb)],
        out_shape=[jax.ShapeDtypeStruct((b, l, W_RET), F32),
                   jax.ShapeDtypeStruct((depth, b) + st_tail, F32)],
        scratch_shapes=[pltpu.VMEM((nb, n_pairs, PAIR, PAIR), F32)],
        input_output_aliases=aliases,
        compiler_params=_cparams(("parallel", "arbitrary")),
        name="retention",
    )(*args)


def _tri_inverse_many(a_list, c, lo, ii, jj):
    def mm(x, y):
        return _dot(x.astype(BF16), _stack(y, lo).astype(BF16))

    base = min(c, TRI_BASE)
    lb = _log2(base)
    dblk = (ii >> lb) == (jj >> lb)
    eye = jnp.where(ii == jj, 1.0, 0.0)
    xs = [jnp.where(dblk, -a, 0.0) for a in a_list]
    ts = [eye + x for x in xs]
    for _ in range(lb - 1):
        xs = [mm(x, x) for x in xs]
        ts = [t + mm(t, x) for t, x in zip(ts, xs)]
    blk = base
    while blk < c:
        inner = (ii >> _log2(blk)) == (jj >> _log2(blk))
        outer = (ii >> _log2(2 * blk)) == (jj >> _log2(2 * blk))
        sel = outer & jnp.logical_not(inner)
        ls = [jnp.where(sel, a, 0.0) for a in a_list]
        ms = [mm(l, t) for l, t in zip(ls, ts)]
        ts = [t - mm(t, m) for t, m in zip(ts, ms)]
        blk *= 2
    return ts


def _dn_kernel(*refs, TT, c, n_pairs, has_state, n_prev, nb):
    n_in = 5 + (2 if has_state else 0)
    (proj_ref, convw_ref, alog_ref, dtb_ref, nw_ref, *st), rest = _split_refs(refs, n_in, has_state, n_prev)
    s0_ref, sconv_ref = st if has_state else (None, None)
    o_ref, sout_ref, convout_ref, S_ref, tail_ref = rest
    n2 = 2 * c
    ii = _row_iota((c, n2))
    lane2 = _lane_iota((c, n2))
    t = pl.program_id(1)

    @pl.when(t == 0)
    def _():
        tail_ref[...] = jnp.zeros_like(tail_ref)
        for i in range(nb):
            _load_pair_state(S_ref.at[i], _batch_view(s0_ref, i), n_pairs)
            if has_state:
                tail_ref[i, 8 - (CONV_W - 1):8, :] = sconv_ref[i]

    items = []
    for i in range(nb):
        items.append(_dn_prepare(_batch_view(proj_ref, i), convw_ref, alog_ref, dtb_ref, tail_ref.at[i], ii, lane2,
                                 TT=TT, c=c, n_pairs=n_pairs))
    flat = [it for batch_items in items for it in batch_items]
    tinvs = _tri_inverse_many([it['a'] for it in flat], c, lane2 < c, ii, lane2 & (c - 1))
    for it, tinv in zip(flat, tinvs):
        tb = tinv.astype(BF16)
        it['u'] = _dot(tb, it['vb'])
        it['w'] = _dot(tb, it['kbe']).astype(BF16)
    for i in range(nb):
        _dn_scan(items[i], _batch_view(proj_ref, i), nw_ref, _batch_view(o_ref, i), S_ref.at[i],
                 TT=TT, c=c, n_pairs=n_pairs)

    @pl.when(t == pl.num_programs(1) - 1)
    def _():
        for i in range(nb):
            _store_pair_state(_batch_view(sout_ref, i), S_ref.at[i], n_pairs)
            convout_ref[i] = tail_ref[i, 8 - (CONV_W - 1):8, :]


def _dn_prepare(proj_ref, convw_ref, alog_ref, dtb_ref, tail_ref, ii, lane2, *, TT, c, n_pairs):
    m0 = _head0_lanes()
    pmean = _group_mean_matrix(_block_diag_mask())
    nchunks = TT // c
    lc = _log2(c)

    x = proj_ref[0, :, 0:DN_CONV_CH]
    tail = tail_ref[...]
    rows8 = _row_iota((8, DN_CONV_CH))
    acc = x * convw_ref[CONV_W - 1:CONV_W, :]
    for s in range(1, CONV_W):
        xs = pltpu.roll(x, s, 0)
        top = jnp.where(rows8 < s, pltpu.roll(tail, s, 0), xs[0:8])
        xs = top if TT == 8 else jnp.concatenate([top, xs[8:]], axis=0)
        acc = acc + xs * convw_ref[CONV_W - 1 - s:CONV_W - s, :]
    tail_ref[...] = x[TT - 8:TT]
    qkv = _silu(acc)

    misc = proj_ref[0, :, DN_CONV_CH + W_DN:DN_COLS]
    sig = jax.nn.sigmoid(misc)
    glog = -jnp.exp(alog_ref[...]) * _softplus(misc + dtb_ref[...])
    rt = _row_iota((TT, TT))
    ct = _lane_iota((TT, TT))
    tril_t = jnp.where(((rt >> lc) == (ct >> lc)) & (ct <= rt), 1.0, 0.0).astype(BF16)
    cum_tok = _dot_exact_lhs(tril_t, glog)

    n2 = 2 * c
    jj = lane2 & (c - 1)
    lo2 = lane2 < c
    incl = jj <= ii
    strict = jj < ii
    ones_c = jnp.ones((c, PAIR), BF16)
    lane128 = _lane_iota((c, PAIR))

    items = []
    for p in range(n_pairs):
        h0, h1 = 2 * p, 2 * p + 1
        q_all = qkv[:, p * PAIR:(p + 1) * PAIR]
        k_all = qkv[:, W_DN + p * PAIR:W_DN + (p + 1) * PAIR]
        v_all = qkv[:, 2 * W_DN + p * PAIR:2 * W_DN + (p + 1) * PAIR]
        q_all = q_all * lax.rsqrt(_dot((q_all * q_all).astype(BF16), pmean) * HEAD_DIM + EPS) * HEAD_DIM ** -0.5
        k_all = k_all * lax.rsqrt(_dot((k_all * k_all).astype(BF16), pmean) * HEAD_DIM + EPS)
        for ci in range(nchunks):
            rows = slice(ci * c, (ci + 1) * c)
            q, k, v = q_all[rows], k_all[rows], v_all[rows]
            b0, b1 = sig[rows, h0:h0 + 1], sig[rows, h1:h1 + 1]
            c0 = cum_tok[rows, DN_A_LANE + h0:DN_A_LANE + h0 + 1]
            c1 = cum_tok[rows, DN_A_LANE + h1:DN_A_LANE + h1 + 1]
            beta = jnp.where(m0, jnp.broadcast_to(b0, (c, PAIR)), jnp.broadcast_to(b1, (c, PAIR)))
            cum_pair = jnp.where(m0, jnp.broadcast_to(c0, (c, PAIR)), jnp.broadcast_to(c1, (c, PAIR)))
            cum_col = cum_pair if n2 == PAIR else jnp.where(lo2, jnp.broadcast_to(c0, (c, n2)),
                                                            jnp.broadcast_to(c1, (c, n2)))
            ctok = cum_tok[rows]
            cum_sel = jnp.concatenate([jnp.where(lane128 ==DN_A_LANE + h0, ctok, 0.0),
                                       jnp.where(lane128 ==DN_A_LANE + h1, ctok, 0.0)], axis=0)
            cum_row = _dot_nt_exact(ones_c, cum_sel)
            gam = jnp.where(incl, jnp.exp(jnp.minimum(cum_col - cum_row, 0.0)), 0.0)
            k_st = _stack(k, m0).astype(BF16)
            kb = k * beta
            a = jnp.where(strict, _dot_nt(kb.astype(BF16), k_st) * gam, 0.0)
            attn = _dot_nt(q.astype(BF16), k_st) * gam
            ecum = jnp.exp(cum_pair)
            last_row = cum_tok[(ci + 1) * c - 1:(ci + 1) * c, :]
            last = jnp.where(m0, jnp.broadcast_to(last_row[:, DN_A_LANE + h0:DN_A_LANE + h0 + 1], (1, PAIR)),
                             jnp.broadcast_to(last_row[:, DN_A_LANE + h1:DN_A_LANE + h1 + 1], (1, PAIR)))
            items.append(dict(p=p, ci=ci, a=a, attn=attn.astype(BF16),
                              vb=_stack(v * beta, m0).astype(BF16), kbe=_stack(kb * ecum, m0).astype(BF16),
                              q_dec=(q * ecum).astype(BF16), k_dec=(k * jnp.exp(last - cum_pair)).astype(BF16),
                              dec=jnp.exp(last)))
    return items


def _dn_scan(items, proj_ref, nw_ref, o_ref, S_ref, *, TT, c, n_pairs):
    m0 = _head0_lanes()
    bm = _block_diag_mask()
    pmean = _group_mean_matrix(bm)
    nw = nw_ref[...]
    nchunks = TT // c
    S = [S_ref[p] for p in range(n_pairs)]
    outs = [[] for _ in range(n_pairs)]
    for ci in range(nchunks):
        for p in range(n_pairs):
            it = items[p * nchunks + ci]
            Sb = S[p].astype(BF16)
            v_new = it['u'] - _dot(it['w'], Sb)
            o = _dot(it['q_dec'], Sb) + _dot(it['attn'], _stack(v_new, m0).astype(BF16))
            S[p] = S[p] * it['dec'] + jnp.where(bm, _dot_tn(it['k_dec'], v_new.astype(BF16)), 0.0)
            outs[p].append(o)
    for p in range(n_pairs):
        S_ref[p] = S[p]
        o = outs[p][0] if nchunks == 1 else jnp.concatenate(outs[p], axis=0)
        gate = proj_ref[0, :, DN_CONV_CH + p * PAIR:DN_CONV_CH + (p + 1) * PAIR]
        ms = _dot((o * o).astype(BF16), pmean)
        o_ref[0, :, p * PAIR:(p + 1) * PAIR] = o * lax.rsqrt(ms + EPS) * nw * _silu(gate)


def delta_net(proj, conv_w, s_conv_all, a_log, dt_bias, s0_all, layer, depth, prev, norm_w, TT, c, nb):
    b, l, _ = proj.shape
    n_pairs = H_DN // 2
    alog = jnp.zeros((1, PAIR), F32).at[0, DN_A_LANE:DN_A_LANE + H_DN].set(a_log)
    dtb = jnp.zeros((1, PAIR), F32).at[0, DN_A_LANE:DN_A_LANE + H_DN].set(dt_bias)
    nw = jnp.concatenate([norm_w, norm_w]).reshape(1, PAIR)
    row = lambda n: pl.BlockSpec((1, n), lambda i, j: (0, 0))
    st_tail = (H_DN, HEAD_DIM, HEAD_DIM)
    cv_tail = (CONV_W - 1, DN_CONV_CH)
    in_specs = [pl.BlockSpec((nb, TT, DN_COLS), lambda i, j: (i, j, 0)),
                pl.BlockSpec((CONV_W, DN_CONV_CH), lambda i, j: (0, 0)),
                row(PAIR), row(PAIR), row(PAIR)]
    args = [proj, conv_w, alog, dtb, nw]
    has_state = s0_all is not None
    if has_state:
        in_specs += [_state_specs(st_tail, layer, nb), _state_specs(cv_tail, layer, nb)]
        args += [s0_all, s_conv_all]
    aliases = {}
    if prev is not None:
        aliases = {len(args): 1, len(args) + 1: 2}
        in_specs += [_ANY_SPEC, _ANY_SPEC]
        args += list(prev)
    return pl.pallas_call(
        functools.partial(_dn_kernel, TT=TT, c=c, n_pairs=n_pairs, has_state=has_state,
                          n_prev=0 if prev is None else 2, nb=nb),
        grid=(b // nb, l // TT),
        in_specs=in_specs,
        out_specs=[pl.BlockSpec((nb, TT, W_DN), lambda i, j: (i, j, 0)),
                   _state_specs(st_tail, layer, nb), _state_specs(cv_tail, layer, nb)],
        out_shape=[jax.ShapeDtypeStruct((b, l, W_DN), F32),
                   jax.ShapeDtypeStruct((depth, b) + st_tail, F32),
                   jax.ShapeDtypeStruct((depth, b) + cv_tail, F32)],
        scratch_shapes=[pltpu.VMEM((nb, n_pairs, PAIR, PAIR), F32), pltpu.VMEM((nb, 8, DN_CONV_CH), F32)],
        input_output_aliases=aliases,
        compiler_params=_cparams(("parallel", "arbitrary")),
        name="delta_net",
    )(*args)


def _gla_kernel(*refs, TT, m, cs, n_pairs, has_state, n_prev, nb):
    n_in = 4 + (1 if has_state else 0)
    (proj_ref, wa2_ref, ba_ref, nw_ref, *st), rest = _split_refs(refs, n_in, has_state, n_prev)
    s0_ref = st[0] if has_state else None
    o_ref, sout_ref, ST_ref = rest
    t = pl.program_id(1)

    @pl.when(t == 0)
    def _():
        for i in range(nb):
            _load_pair_state(ST_ref.at[i], _batch_view(s0_ref, i), n_pairs)
            if has_state:
                for p in range(n_pairs):
                    ST_ref[i, p] = ST_ref[i, p].T

    for i in range(nb):
        _gla_body(_batch_view(proj_ref, i), wa2_ref, ba_ref, nw_ref, _batch_view(o_ref, i), ST_ref.at[i],
                  TT=TT, m=m, cs=cs, n_pairs=n_pairs)

    @pl.when(t == pl.num_programs(1) - 1)
    def _():
        for i in range(nb):
            for p in range(n_pairs):
                s = ST_ref[i, p].T
                sout_ref[i, 2 * p] = s[0:HEAD_DIM, 0:HEAD_DIM]
                sout_ref[i, 2 * p + 1] = s[HEAD_DIM:PAIR, HEAD_DIM:PAIR]


def _gla_body(proj_ref, wa2_ref, ba_ref, nw_ref, o_ref, ST_ref, *, TT, m, cs, n_pairs):
    bm = _block_diag_mask()
    m0 = _head0_lanes()
    pmean = _group_mean_matrix(bm)
    nw = nw_ref[...]
    ng = m // cs
    n2 = 2 * m
    sh = _log2(cs)
    r2 = _row_iota((n2, n2))
    c2 = _lane_iota((n2, n2))
    incl2 = ((r2 >> sh) == (c2 >> sh)) & (c2 <= r2)
    r1 = _row_iota((m, m))
    c1 = _lane_iota((m, m))
    same1 = (r1 >> sh) == (c1 >> sh)
    tril = jnp.where(same1 & (c1 <= r1), 1.0, 0.0).astype(BF16)
    ones_blk = jnp.where(same1, 1.0, 0.0).astype(BF16)
    chunk_of_row = _row_iota((m, PAIR)) >> sh

    a1 = proj_ref[0, :, 4 * W_GLA:GLA_COLS].astype(BF16)
    for p in range(n_pairs):
        lanes = slice(p * PAIR, (p + 1) * PAIR)
        logit = _dot(a1, wa2_ref[:, lanes]) + ba_ref[:, lanes]
        g_all = -_softplus(-logit) * (1.0 / GLA_TAU)
        q_all = proj_ref[0, :, p * PAIR:(p + 1) * PAIR] * HEAD_DIM ** -0.5
        k_all = proj_ref[0, :, W_GLA + p * PAIR:W_GLA + (p + 1) * PAIR]
        v_all = proj_ref[0, :, 2 * W_GLA + p * PAIR:2 * W_GLA + (p + 1) * PAIR]
        ST = ST_ref[p]
        outs = []
        for gi in range(TT // m):
            rows = slice(gi * m, (gi + 1) * m)
            q, k, v, g = q_all[rows], k_all[rows], v_all[rows], g_all[rows]
            cum = _dot_exact_lhs(tril, g)
            last = _dot_exact_lhs(ones_blk, g)
            qt = q * jnp.exp(cum)
            kt = k * jnp.exp(-cum)
            kdec = (k * jnp.exp(last - cum)).astype(BF16)
            sc = jnp.where(incl2, _dot_nt(_stack(qt, m0).astype(BF16), _stack(kt, m0).astype(BF16)), 0.0)
            o2 = _dot(sc.astype(BF16), _stack(v, m0).astype(BF16))
            o_intra = o2[:m] + o2[m:]
            qtb = qt.astype(BF16)
            if ng == 1:
                kd_big, q_big = kdec, qtb
            else:
                kd_big = jnp.concatenate([jnp.where(chunk_of_row == n, kdec, 0) for n in range(ng)], axis=1)
                q_big = jnp.concatenate([jnp.where(chunk_of_row == n, qtb, 0) for n in range(ng)], axis=1)
            kvt = _dot_tn(v.astype(BF16), kd_big)
            sts = []
            for n in range(ng):
                sts.append(ST.astype(BF16))
                dec = jnp.exp(last[n * cs:n * cs + 1, :])
                ST = ST * dec + jnp.where(bm, kvt[:, n * PAIR:(n + 1) * PAIR], 0.0)
            st_all = sts[0] if ng == 1 else jnp.concatenate(sts, axis=1)
            outs.append(o_intra + _dot_nt(q_big, st_all))
        ST_ref[p] = ST
        o = outs[0] if len(outs) == 1 else jnp.concatenate(outs, axis=0)
        gate = proj_ref[0, :, 3 * W_GLA + p * PAIR:3 * W_GLA + (p + 1) * PAIR]
        ms = _dot((o * o).astype(BF16), pmean)
        o_ref[0, :, lanes] = o * lax.rsqrt(ms + EPS) * nw * _silu(gate)


def gla(proj, w_a2, b_a, s0_all, layer, depth, prev, norm_w, TT, m, cs, nb):
    b, l, _ = proj.shape
    n_pairs = H_GLA // 2
    wa2 = jnp.zeros((PAIR, W_GLA), F32).at[:GLA_RANK].set(w_a2).astype(BF16)
    nw = jnp.concatenate([norm_w, norm_w]).reshape(1, PAIR)
    st_tail = (H_GLA, HEAD_DIM, HEAD_DIM)
    in_specs = [pl.BlockSpec((nb, TT, GLA_COLS), lambda i, j: (i, j, 0)),
                pl.BlockSpec((PAIR, W_GLA), lambda i, j: (0, 0)),
                pl.BlockSpec((1, W_GLA), lambda i, j: (0, 0)),
                pl.BlockSpec((1, PAIR), lambda i, j: (0, 0))]
    args = [proj, wa2, b_a.reshape(1, W_GLA), nw]
    if s0_all is not None:
        in_specs.append(_state_specs(st_tail, layer, nb))
        args.append(s0_all)
    aliases = {}
    if prev is not None:
        aliases[len(args)] = 1
        in_specs.append(_ANY_SPEC)
        args.append(prev)
    return pl.pallas_call(
        functools.partial(_gla_kernel, TT=TT, m=m, cs=cs, n_pairs=n_pairs, has_state=s0_all is not None,
                          n_prev=0 if prev is None else 1, nb=nb),
        grid=(b // nb, l // TT),
        in_specs=in_specs,
        out_specs=[pl.BlockSpec((nb, TT, W_GLA), lambda i, j: (i, j, 0)), _state_specs(st_tail, layer, nb)],
        out_shape=[jax.ShapeDtypeStruct((b, l, W_GLA), F32),
                   jax.ShapeDtypeStruct((depth, b) + st_tail, F32)],
        scratch_shapes=[pltpu.VMEM((nb, n_pairs, PAIR, PAIR), F32)],
        input_output_aliases=aliases,
        compiler_params=_cparams(("parallel", "arbitrary")),
        name="gla",
    )(*args)


def _prep_layer_weights(li, w_in, w_out, w_mq, w_mo, w_ff1, w_ff2):
    wi = w_in[li]
    d = wi.shape[0]
    c0 = RET_COLS
    c1 = c0 + DN_CONV_CH + W_DN + 2 * H_DN
    w_ret = wi[:, :c0]
    w_dn = jnp.concatenate([wi[:, c0:c1], jnp.zeros((d, DN_COLS - (c1 - c0)), F32)], axis=1)
    w_gla = jnp.concatenate([wi[:, c1:], jnp.zeros((d, GLA_COLS - (wi.shape[1] - c1)), F32)], axis=1)
    cast = lambda a: a.astype(BF16)
    return dict(w_ret=cast(w_ret), w_dn=cast(w_dn), w_gla=cast(w_gla), w_out=cast(w_out[li]),
                w_mq=cast(w_mq[li]), w_mo=cast(w_mo[li]), w_ff1=cast(w_ff1[li]), w_ff2=cast(w_ff2[li]))


def _run_trunk(x, pos, s_ret, s_dn, s_conv, s_gla, mem_k, mem_v, mem_split, lw, p, cfg):
    b, l, d = x.shape
    t = b * l
    depth = len(lw)
    xf = x.reshape(t, d)
    ret_tables = _ret_consts(cfg['ret_c']) + _rope_tables(pos)
    r_s = d_s = g_s = None
    for li in range(depth):
        w = lw[li]
        g = p['norm_mix'][li]
        pr = rms_matmul(xf, g, w['w_ret'], cfg['tm']).reshape(b, l, RET_COLS)
        pd = rms_matmul(xf, g, w['w_dn'], cfg['tm']).reshape(b, l, DN_COLS)
        pg = rms_matmul(xf, g, w['w_gla'], cfg['tm']).reshape(b, l, GLA_COLS)
        o_ret, r_s = retention(pr, ret_tables, s_ret, li, depth, r_s, p['ret_norm_w'][li], cfg['ret_c'], cfg['nb'])
        o_dn, *d_s = delta_net(pd, p['dn_conv_w'][li], s_conv, p['dn_a_log'][li], p['dn_dt_bias'][li],
                               s_dn, li, depth, d_s, p['dn_norm_w'][li], cfg['dn_tt'], cfg['dn_c'], cfg['nb'])
        o_gla, g_s = gla(pg, p['gla_w_a2'][li], p['gla_b_a'][li], s_gla, li, depth, g_s, p['gla_norm_w'][li],
                         cfg['gla_tt'], cfg['gla_m'], cfg['gla_cs'], cfg['nb'])
        xf = matmul_res([o_ret.reshape(t, W_RET), o_dn.reshape(t, W_DN), o_gla.reshape(t, W_GLA)],
                        w['w_out'], xf, cfg['tm'])
        qm = rms_matmul(xf, p['norm_cross'][li], w['w_mq'], cfg['tm']).reshape(b, l, d)
        if mem_split:
            att = mem_attention(qm, mem_k, mem_v, cfg['tq'], layer=li)
        else:
            att = mem_attention(qm, mem_k[li], mem_v[li], cfg['tq'])
        xf = matmul_res([att.reshape(t, d)], w['w_mo'], xf, cfg['tm'])
        xf = ffn(xf, p['norm_ffn'][li], w['w_ff1'], w['w_ff2'], p['norm_final'], li == depth - 1,
                 cfg['ffn_tm'], cfg['ffn_tf'])
    return (xf.reshape(b, l, d), r_s, d_s[0], d_s[1], g_s)


def kernel(x_prompt, x_sample, state_ret, state_dn, state_dn_conv, state_gla, cache_mem_k, cache_mem_v, mem_prompt, norm_mix, w_in, ret_norm_w, dn_conv_w, dn_a_log, dn_dt_bias, dn_norm_w, gla_w_a2, gla_b_a, gla_norm_w, w_out, norm_cross, norm_mem, w_mq, w_mk, w_mv, w_mo, norm_ffn, w_ff1, w_ff2, norm_final):
    b_p, l_p, d = x_prompt.shape
    b_s, l_s, _ = x_sample.shape
    depth = w_in.shape[0]
    n_mem = mem_prompt.shape[1]
    p = dict(norm_mix=norm_mix, ret_norm_w=ret_norm_w, dn_conv_w=dn_conv_w, dn_a_log=dn_a_log,
             dn_dt_bias=dn_dt_bias, dn_norm_w=dn_norm_w, gla_w_a2=gla_w_a2, gla_b_a=gla_b_a,
             gla_norm_w=gla_norm_w, norm_cross=norm_cross, norm_ffn=norm_ffn, norm_final=norm_final)
    lw = [_prep_layer_weights(li, w_in, w_out, w_mq, w_mo, w_ff1, w_ff2) for li in range(depth)]

    memf = mem_prompt.reshape(b_p * n_mem, d)
    mk = [rms_matmul(memf, norm_mem[li], w_mk[li].astype(BF16), 512).reshape(b_p, n_mem, d) for li in range(depth)]
    mv = [rms_matmul(memf, norm_mem[li], w_mv[li].astype(BF16), 512).reshape(b_p, n_mem, d) for li in range(depth)]

    cfg_p = dict(tm=512, ret_c=256, dn_tt=256, dn_c=math.gcd(l_p, DN_CHUNK), gla_tt=256, gla_m=64,
                 gla_cs=math.gcd(l_p, GLA_CHUNK), tq=512, ffn_tm=1024, ffn_tf=512, nb=1)
    y_p, p_ret, p_dn, p_conv, p_gla = _run_trunk(
        x_prompt, jnp.arange(l_p, dtype=jnp.int32), None, None, None, None, mk, mv, False, lw, p, cfg_p)

    cfg_s = dict(tm=512, ret_c=l_s, dn_tt=l_s, dn_c=math.gcd(l_s, DN_CHUNK), gla_tt=l_s,
                 gla_m=math.gcd(l_s, GLA_CHUNK), gla_cs=math.gcd(l_s, GLA_CHUNK), tq=l_s, ffn_tm=1024, ffn_tf=512,
                 nb=8)
    y_s, s_ret, s_dn, s_conv, s_gla = _run_trunk(
        x_sample, PAST_LEN + jnp.arange(l_s, dtype=jnp.int32), state_ret, state_dn, state_dn_conv, state_gla,
        cache_mem_k, cache_mem_v, True, lw, p, cfg_s)

    mem_shape = (depth, b_p, n_mem, MEM_HEADS, d // MEM_HEADS)
    return (y_p, y_s, p_ret, p_dn, p_conv, p_gla, jnp.stack(mk).reshape(mem_shape), jnp.stack(mv).reshape(mem_shape),
            s_ret, s_dn, s_conv, s_gla)
```

```python
import functools
import math

import jax
import jax.numpy as jnp
from jax import lax
from jax.experimental import pallas as pl
from jax.experimental.pallas import tpu as pltpu

F32 = jnp.float32
BF16 = jnp.bfloat16

LANES = 128
HEAD_DIM = 64
PAIR = 2 * HEAD_DIM
H_RET, H_DN, H_GLA = 6, 6, 4
W_RET, W_DN, W_GLA = H_RET * HEAD_DIM, H_DN * HEAD_DIM, H_GLA * HEAD_DIM
CONV_W = 4
DN_CONV_CH = 3 * W_DN
GLA_RANK = 16
GLA_TAU = 16.0
MEM_HEADS = 4
DN_CHUNK = 64
GLA_CHUNK = 16
ROPE_BASE = 10000.0
EPS = 1e-6
PAST_LEN = 16384

DN_COLS = DN_CONV_CH + W_DN + PAIR
GLA_COLS = 4 * W_GLA + PAIR
RET_COLS = 4 * W_RET
DN_A_LANE = H_DN
TRI_BASE = 16

VMEM_LIMIT = 56 * 1024 * 1024


def _cparams(sem):
    return pltpu.CompilerParams(dimension_semantics=sem, vmem_limit_bytes=VMEM_LIMIT)


def _log2(n):
    l = int(math.log2(n))
    assert (1 << l) == n, n
    return l


def _dot(a, b):
    return jnp.dot(a, b, preferred_element_type=F32)


def _dot_nt(a, b):
    return lax.dot_general(a, b, (((1,), (1,)), ((), ())), preferred_element_type=F32)


def _dot_tn(a, b):
    return lax.dot_general(a, b, (((0,), (0,)), ((), ())), preferred_element_type=F32)


def _split2(a):
    hi = a.astype(BF16)
    lo = (a - hi.astype(F32)).astype(BF16)
    return hi, lo


def _split3(a):
    hi = a.astype(BF16)
    r = a - hi.astype(F32)
    mid = r.astype(BF16)
    lo = (r - mid.astype(F32)).astype(BF16)
    return hi, mid, lo


def _dot_exact_lhs(a_bf16, b):
    hi, mid, lo = _split3(b)
    return _dot(a_bf16, hi) + _dot(a_bf16, mid) + _dot(a_bf16, lo)


def _dot_nt_exact(a_bf16, b):
    hi, mid, lo = _split3(b)
    return _dot_nt(a_bf16, hi) + _dot_nt(a_bf16, mid) + _dot_nt(a_bf16, lo)


def _silu(x):
    return x * jax.nn.sigmoid(x)


def _softplus(x):
    return jnp.maximum(x, 0.0) + jnp.log1p(jnp.exp(-jnp.abs(x)))


def _lane_iota(shape):
    return lax.broadcasted_iota(jnp.int32, shape, len(shape) - 1)


def _row_iota(shape):
    return lax.broadcasted_iota(jnp.int32, shape, 0)


def _head0_lanes():
    return _lane_iota((1, PAIR)) < HEAD_DIM


def _stack(x, lo):
    return jnp.concatenate([jnp.where(lo, x, 0.0), jnp.where(lo, 0.0, x)], axis=0)


def _block_diag_mask():
    r = _row_iota((PAIR, PAIR))
    c = _lane_iota((PAIR, PAIR))
    return (r >> 6) == (c >> 6)


def _group_mean_matrix(bm):
    return jnp.where(bm, 1.0 / HEAD_DIM, 0.0).astype(BF16)


def _load_pair_state(S_ref, s0_ref, n_pairs):
    S_ref[...] = jnp.zeros_like(S_ref)
    if s0_ref is not None:
        for p in range(n_pairs):
            S_ref[p, 0:HEAD_DIM, 0:HEAD_DIM] = s0_ref[0, 2 * p]
            S_ref[p, HEAD_DIM:PAIR, HEAD_DIM:PAIR] = s0_ref[0, 2 * p + 1]


def _store_pair_state(sout_ref, S_ref, n_pairs):
    for p in range(n_pairs):
        sout_ref[0, 2 * p] = S_ref[p, 0:HEAD_DIM, 0:HEAD_DIM]
        sout_ref[0, 2 * p + 1] = S_ref[p, HEAD_DIM:PAIR, HEAD_DIM:PAIR]


def _split_refs(refs, n_in, has_state, n_prev):
    ins = refs[:n_in]
    refs = refs[n_in:]
    return ins, refs[n_prev:]


def _rms_matmul_kernel(x_ref, g_ref, w_ref, o_ref, wb_ref, *, w_transposed):
    @pl.when(pl.program_id(0) == 0)
    def _():
        wb_ref[...] = w_ref[...].astype(BF16)

    x = x_ref[...]
    xn = (x * lax.rsqrt(jnp.mean(x * x, axis=-1, keepdims=True) + EPS) * g_ref[...]).astype(BF16)
    y = _dot_nt(xn, wb_ref[...]) if w_transposed else _dot(xn, wb_ref[...])
    o_ref[...] = y.astype(o_ref.dtype)


def rms_matmul(x, g, w, tm, out_dtype=F32, w_transposed=False):
    t, d = x.shape
    n = w.shape[0] if w_transposed else w.shape[1]
    return pl.pallas_call(
        functools.partial(_rms_matmul_kernel, w_transposed=w_transposed),
        grid=(t // tm,),
        in_specs=[pl.BlockSpec((tm, d), lambda i: (i, 0)),
                  pl.BlockSpec((1, d), lambda i: (0, 0)),
                  pl.BlockSpec(w.shape, lambda i: (0, 0))],
        out_specs=pl.BlockSpec((tm, n), lambda i: (i, 0)),
        out_shape=jax.ShapeDtypeStruct((t, n), out_dtype),
        scratch_shapes=[pltpu.VMEM(w.shape, BF16)],
        compiler_params=_cparams(("arbitrary",)),
        name="rms_matmul",
    )(x, g.reshape(1, d), w)


def _matmul_res_kernel(*refs, widths):
    n_in = len(widths)
    a_refs = refs[:n_in]
    w_ref, r_ref, o_ref = refs[n_in:]
    acc = r_ref[...]
    off = 0
    for a_ref, wk in zip(a_refs, widths):
        acc = acc + _dot(a_ref[...].astype(BF16), w_ref[off:off + wk, :])
        off += wk
    o_ref[...] = acc


def matmul_res(a_list, w_bf16, res, tm):
    t, n = res.shape
    widths = tuple(a.shape[1] for a in a_list)
    k = w_bf16.shape[0]
    assert sum(widths) == k
    in_specs = [pl.BlockSpec((tm, wk), lambda i: (i, 0)) for wk in widths]
    in_specs += [pl.BlockSpec((k, n), lambda i: (0, 0)), pl.BlockSpec((tm, n), lambda i: (i, 0))]
    return pl.pallas_call(
        functools.partial(_matmul_res_kernel, widths=widths),
        grid=(t // tm,),
        in_specs=in_specs,
        out_specs=pl.BlockSpec((tm, n), lambda i: (i, 0)),
        out_shape=jax.ShapeDtypeStruct((t, n), F32),
        compiler_params=_cparams(("parallel",)),
        name="matmul_res",
    )(*a_list, w_bf16, res)


def _ffn_kernel(x_ref, g_ref, w1_ref, w2_ref, gf_ref, o_ref, xn_ref, acc_ref, *, final_norm):
    j = pl.program_id(1)

    @pl.when(j == 0)
    def _():
        x = x_ref[...]
        xn = x * lax.rsqrt(jnp.mean(x * x, axis=-1, keepdims=True) + EPS) * g_ref[...]
        xn_ref[...] = xn.astype(BF16)
        acc_ref[...] = x

    h = _dot(xn_ref[...], w1_ref[...])
    h = jnp.square(jnp.maximum(h, 0.0))
    acc_ref[...] += _dot(h.astype(BF16), w2_ref[...])

    @pl.when(j == pl.num_programs(1) - 1)
    def _():
        y = acc_ref[...]
        if final_norm:
            y = y * lax.rsqrt(jnp.mean(y * y, axis=-1, keepdims=True) + EPS) * gf_ref[...]
        o_ref[...] = y


def ffn(x, g, w1_bf16, w2_bf16, g_final, final_norm, tm, tf):
    t, d = x.shape
    f = w1_bf16.shape[1]
    return pl.pallas_call(
        functools.partial(_ffn_kernel, final_norm=final_norm),
        grid=(t // tm, f // tf),
        in_specs=[pl.BlockSpec((tm, d), lambda i, j: (i, 0)),
                  pl.BlockSpec((1, d), lambda i, j: (0, 0)),
                  pl.BlockSpec((d, tf), lambda i, j: (0, j)),
                  pl.BlockSpec((tf, d), lambda i, j: (j, 0)),
                  pl.BlockSpec((1, d), lambda i, j: (0, 0))],
        out_specs=pl.BlockSpec((tm, d), lambda i, j: (i, 0)),
        out_shape=jax.ShapeDtypeStruct((t, d), F32),
        scratch_shapes=[pltpu.VMEM((tm, d), BF16), pltpu.VMEM((tm, d), F32)],
        compiler_params=_cparams(("parallel", "arbitrary")),
        name="ffn",
    )(x, g.reshape(1, d), w1_bf16, w2_bf16, g_final.reshape(1, d))


def _attn_kernel(q_ref, k_ref, v_ref, o_ref, *, n_heads, dh, row_view, nb):
    scale = dh ** -0.5
    lane_tiles = dh // LANES
    rows_per_token = n_heads * lane_tiles

    def head_slab(ref, i, h):
        if not row_view:
            return ref[i, :, h * dh:(h + 1) * dh].astype(BF16)
        n_mem = ref.shape[1] // rows_per_token
        parts = [ref[i, pl.ds(j * n_heads + h, n_mem, stride=rows_per_token), :] for j in range(lane_tiles)]
        return jnp.concatenate(parts, axis=1).astype(BF16)

    for i in range(nb):
        for h in range(n_heads):
            sl = slice(h * dh, (h + 1) * dh)
            q = q_ref[i, :, sl].astype(BF16)
            s = _dot_nt(q, head_slab(k_ref, i, h)) * scale
            m = jnp.max(s, axis=-1, keepdims=True)
            e = jnp.exp(s - m)
            l = jnp.sum(e, axis=-1, keepdims=True)
            o_ref[i, :, sl] = (_dot(e.astype(BF16), head_slab(v_ref, i, h)) / l).astype(o_ref.dtype)


def mem_attention(q, mem_k, mem_v, tq, nb, layer=None):
    b, l, d = q.shape
    dh = d // MEM_HEADS
    if layer is None:
        m = mem_k.shape[1]
        mem_spec = pl.BlockSpec((nb, m, d), lambda i, j: (i, 0, 0))
    else:
        rows = mem_k.shape[1]
        mem_spec = pl.BlockSpec((nb, rows, LANES), lambda i, j: (layer * (b // nb) + i, 0, 0))
    return pl.pallas_call(
        functools.partial(_attn_kernel, n_heads=MEM_HEADS, dh=dh, row_view=layer is not None, nb=nb),
        grid=(b // nb, l // tq),
        in_specs=[pl.BlockSpec((nb, tq, d), lambda i, j: (i, j, 0)), mem_spec, mem_spec],
        out_specs=pl.BlockSpec((nb, tq, d), lambda i, j: (i, j, 0)),
        out_shape=jax.ShapeDtypeStruct((b, l, d), q.dtype),
        compiler_params=_cparams(("parallel", "arbitrary")),
        name="mem_attention",
    )(q, mem_k, mem_v)


def cache_row_view(cache):
    depth, b, m, nh, dh = cache.shape
    c = cache.reshape(depth, b, m, nh, dh // LANES, LANES).transpose(0, 1, 2, 4, 3, 5)
    return c.reshape(depth * b, m * nh * (dh // LANES), LANES)


def _state_specs(shape_tail, layer, nb):
    nd = len(shape_tail)
    return pl.BlockSpec((None, nb) + tuple(shape_tail), lambda i, j: (layer, i) + (0,) * nd)


_ANY_SPEC = pl.BlockSpec(memory_space=pl.ANY)


def _batch_view(ref, i):
    return None if ref is None else ref.at[pl.ds(i, 1)]


def _ret_kernel(*refs, C, n_pairs, has_state, n_prev, nb):
    n_in = 8 + (1 if has_state else 0)
    (proj_ref, cos_ref, sin_ref, dm_ref, xi_ref, zeta_ref, gc_ref, lnw_ref, *st), rest = \
        _split_refs(refs, n_in, has_state, n_prev)
    s0_ref = st[0] if has_state else None
    o_ref, sout_ref, S_ref = rest
    t = pl.program_id(1)

    @pl.when(t == 0)
    def _():
        for i in range(nb):
            _load_pair_state(S_ref.at[i], _batch_view(s0_ref, i), n_pairs)

    for i in range(nb):
        _ret_body(_batch_view(proj_ref, i), cos_ref, sin_ref, dm_ref, xi_ref, zeta_ref, gc_ref, lnw_ref,
                  _batch_view(o_ref, i), S_ref.at[i], C=C, n_pairs=n_pairs)

    @pl.when(t == pl.num_programs(1) - 1)
    def _():
        for i in range(nb):
            _store_pair_state(_batch_view(sout_ref, i), S_ref.at[i], n_pairs)


def _ret_body(proj_ref, cos_ref, sin_ref, dm_ref, xi_ref, zeta_ref, gc_ref, lnw_ref, o_ref, S_ref, *, C, n_pairs):
    m0 = _head0_lanes()
    first_half = (_lane_iota((1, PAIR)) & (HEAD_DIM - 1)) < HEAD_DIM // 2
    bm = _block_diag_mask()
    pmean = _group_mean_matrix(bm)
    cos = cos_ref[...]
    sin = sin_ref[...]
    lnw = lnw_ref[...]

    def rot(x):
        sw = jnp.where(first_half, pltpu.roll(x, PAIR - HEAD_DIM // 2, 1), pltpu.roll(x, HEAD_DIM // 2, 1))
        return x * cos + sw * sin

    for p in range(n_pairs):
        lanes = slice(p * PAIR, (p + 1) * PAIR)
        q = rot(proj_ref[0, :, p * PAIR:(p + 1) * PAIR])
        k = rot(proj_ref[0, :, W_RET + p * PAIR:W_RET + (p + 1) * PAIR]) * HEAD_DIM ** -0.5
        v = proj_ref[0, :, 2 * W_RET + p * PAIR:2 * W_RET + (p + 1) * PAIR]
        gate = proj_ref[0, :, 3 * W_RET + p * PAIR:3 * W_RET + (p + 1) * PAIR]
        kb = k.astype(BF16)
        vb = v.astype(BF16)
        sc = _dot_nt(_stack(q, m0).astype(BF16), kb) * dm_ref[p]
        o2 = _dot(sc.astype(BF16), vb)
        o_intra = jnp.where(m0, o2[:C], o2[C:])
        S = S_ref[p]
        o_inter = _dot(q.astype(BF16), S.astype(BF16)) * xi_ref[p]
        kv = _dot_tn((k * zeta_ref[p]).astype(BF16), vb)
        S_ref[p] = S * gc_ref[p][0:1, :] + jnp.where(bm, kv, 0.0)
        o = o_intra + o_inter
        oc = o - _dot(o.astype(BF16), pmean)
        var = _dot((oc * oc).astype(BF16), pmean)
        o_ref[0, :, lanes] = (oc * lax.rsqrt(var + EPS) * lnw * _silu(gate)).astype(o_ref.dtype)


def _ret_consts(C):
    lg = jnp.log1p(-jnp.exp2(-5.0 - jnp.arange(H_RET, dtype=F32)))
    idx = jnp.arange(C, dtype=F32)
    diff = idx[:, None] - idx[None, :]
    dmask = jnp.where(diff >= 0, jnp.exp(lg[:, None, None] * jnp.maximum(diff, 0.0)), 0.0)
    dm = dmask.reshape(H_RET // 2, 2 * C, C)
    xi = jnp.exp(lg[:, None] * (idx + 1.0))
    zeta = jnp.exp(lg[:, None] * (C - 1.0 - idx))
    gch = jnp.exp(lg * C)

    def pair_lanes(a):
        r = a.shape[1]
        a = jnp.repeat(a[:, :, None], HEAD_DIM, axis=2)
        a = a.reshape(H_RET // 2, 2, r, HEAD_DIM).transpose(0, 2, 1, 3)
        return a.reshape(H_RET // 2, r, PAIR)

    gc = pair_lanes(jnp.broadcast_to(gch[:, None], (H_RET, 8)))
    return dm, pair_lanes(xi), pair_lanes(zeta), gc


def _rope_tables(pos):
    half = HEAD_DIM // 2
    inv_freq = ROPE_BASE ** (-jnp.arange(half, dtype=F32) / half)
    ang = pos.astype(F32)[:, None] * inv_freq[None, :]
    cos, sin = jnp.cos(ang), jnp.sin(ang)
    cos_t = jnp.concatenate([cos, cos, cos, cos], axis=-1)
    sin_t = jnp.concatenate([-sin, sin, -sin, sin], axis=-1)
    return cos_t, sin_t


def retention(proj, tables, s0_all, layer, depth, prev, ln_w, C, nb, act_dtype):
    b, l, _ = proj.shape
    n_pairs = H_RET // 2
    dm, xi, zeta, gc, cos_t, sin_t = tables
    lnw = jnp.concatenate([ln_w, ln_w]).reshape(1, PAIR)
    const3 = lambda shape: pl.BlockSpec(shape, lambda i, j: (0, 0, 0))
    st_tail = (H_RET, HEAD_DIM, HEAD_DIM)
    in_specs = [pl.BlockSpec((nb, C, RET_COLS), lambda i, j: (i, j, 0)),
                pl.BlockSpec((C, PAIR), lambda i, j: (j, 0)),
                pl.BlockSpec((C, PAIR), lambda i, j: (j, 0)),
                const3((n_pairs, 2 * C, C)),
                const3((n_pairs, C, PAIR)),
                const3((n_pairs, C, PAIR)),
                const3((n_pairs, 8, PAIR)),
                pl.BlockSpec((1, PAIR), lambda i, j: (0, 0))]
    args = [proj, cos_t, sin_t, dm, xi, zeta, gc, lnw]
    if s0_all is not None:
        in_specs.append(_state_specs(st_tail, layer, nb))
        args.append(s0_all)
    aliases = {}
    if prev is not None:
        aliases[len(args)] = 1
        in_specs.append(_ANY_SPEC)
        args.append(prev)
    return pl.pallas_call(
        functools.partial(_ret_kernel, C=C, n_pairs=n_pairs, has_state=s0_all is not None,
                          n_prev=0 if prev is None else 1, nb=nb),
        grid=(b // nb, l // C),
        in_specs=in_specs,
        out_specs=[pl.BlockSpec((nb, C, W_RET), lambda i, j: (i, j, 0)), _state_specs(st_tail, layer, nb)],
        out_shape=[jax.ShapeDtypeStruct((b, l, W_RET), act_dtype),
                   jax.ShapeDtypeStruct((depth, b) + st_tail, F32)],
        scratch_shapes=[pltpu.VMEM((nb, n_pairs, PAIR, PAIR), F32)],
        input_output_aliases=aliases,
        compiler_params=_cparams(("parallel", "arbitrary")),
        name="retention",
    )(*args)


def _tri_inverse_many(a_list, c, lo, ii, jj):
    def mm(x, y):
        return _dot(x.astype(BF16), _stack(y, lo).astype(BF16))

    base = min(c, TRI_BASE)
    lb = _log2(base)
    dblk = (ii >> lb) == (jj >> lb)
    eye = jnp.where(ii == jj, 1.0, 0.0)
    xs = [jnp.where(dblk, -a, 0.0) for a in a_list]
    ts = [eye + x for x in xs]
    for _ in range(lb - 1):
        xs = [mm(x, x) for x in xs]
        ts = [t + mm(t, x) for t, x in zip(ts, xs)]
    blk = base
    while blk < c:
        inner = (ii >> _log2(blk)) == (jj >> _log2(blk))
        outer = (ii >> _log2(2 * blk)) == (jj >> _log2(2 * blk))
        sel = outer & jnp.logical_not(inner)
        ls = [jnp.where(sel, a, 0.0) for a in a_list]
        ms = [mm(l, t) for l, t in zip(ls, ts)]
        ts = [t - mm(t, m) for t, m in zip(ts, ms)]
        blk *= 2
    return ts


def _dn_kernel(*refs, TT, c, n_pairs, has_state, n_prev, nb):
    n_in = 5 + (2 if has_state else 0)
    (proj_ref, convw_ref, alog_ref, dtb_ref, nw_ref, *st), rest = _split_refs(refs, n_in, has_state, n_prev)
    s0_ref, sconv_ref = st if has_state else (None, None)
    o_ref, sout_ref, convout_ref, S_ref, tail_ref = rest
    n2 = 2 * c
    ii = _row_iota((c, n2))
    lane2 = _lane_iota((c, n2))
    t = pl.program_id(1)

    @pl.when(t == 0)
    def _():
        tail_ref[...] = jnp.zeros_like(tail_ref)
        for i in range(nb):
            _load_pair_state(S_ref.at[i], _batch_view(s0_ref, i), n_pairs)
            if has_state:
                tail_ref[i, 8 - (CONV_W - 1):8, :] = sconv_ref[i]

    items = []
    for i in range(nb):
        items.append(_dn_prepare(_batch_view(proj_ref, i), convw_ref, alog_ref, dtb_ref, tail_ref.at[i], ii, lane2,
                                 TT=TT, c=c, n_pairs=n_pairs))
    flat = [it for batch_items in items for it in batch_items]
    tinvs = _tri_inverse_many([it['a'] for it in flat], c, lane2 < c, ii, lane2 & (c - 1))
    for it, tinv in zip(flat, tinvs):
        tb = tinv.astype(BF16)
        it['u'] = _dot(tb, it['vb'])
        it['w'] = _dot(tb, it['kbe']).astype(BF16)
    for i in range(nb):
        _dn_scan(items[i], _batch_view(proj_ref, i), nw_ref, _batch_view(o_ref, i), S_ref.at[i],
                 TT=TT, c=c, n_pairs=n_pairs)

    @pl.when(t == pl.num_programs(1) - 1)
    def _():
        for i in range(nb):
            _store_pair_state(_batch_view(sout_ref, i), S_ref.at[i], n_pairs)
            convout_ref[i] = tail_ref[i, 8 - (CONV_W - 1):8, :]


def _dn_prepare(proj_ref, convw_ref, alog_ref, dtb_ref, tail_ref, ii, lane2, *, TT, c, n_pairs):
    m0 = _head0_lanes()
    pmean = _group_mean_matrix(_block_diag_mask())
    nchunks = TT // c
    lc = _log2(c)

    x = proj_ref[0, :, 0:DN_CONV_CH]
    tail = tail_ref[...]
    rows8 = _row_iota((8, DN_CONV_CH))
    acc = x * convw_ref[CONV_W - 1:CONV_W, :]
    for s in range(1, CONV_W):
        xs = pltpu.roll(x, s, 0)
        top = jnp.where(rows8 < s, pltpu.roll(tail, s, 0), xs[0:8])
        xs = top if TT == 8 else jnp.concatenate([top, xs[8:]], axis=0)
        acc = acc + xs * convw_ref[CONV_W - 1 - s:CONV_W - s, :]
    tail_ref[...] = x[TT - 8:TT]
    qkv = _silu(acc)

    misc = proj_ref[0, :, DN_CONV_CH + W_DN:DN_COLS]
    sig = jax.nn.sigmoid(misc)
    glog = -jnp.exp(alog_ref[...]) * _softplus(misc + dtb_ref[...])
    rt = _row_iota((TT, TT))
    ct = _lane_iota((TT, TT))
    tril_t = jnp.where(((rt >> lc) == (ct >> lc)) & (ct <= rt), 1.0, 0.0).astype(BF16)
    cum_tok = _dot_exact_lhs(tril_t, glog)

    n2 = 2 * c
    jj = lane2 & (c - 1)
    lo2 = lane2 < c
    incl = jj <= ii
    strict = jj < ii
    ones_c = jnp.ones((c, PAIR), BF16)
    lane128 = _lane_iota((c, PAIR))

    items = []
    for p in range(n_pairs):
        h0, h1 = 2 * p, 2 * p + 1
        q_all = qkv[:, p * PAIR:(p + 1) * PAIR]
        k_all = qkv[:, W_DN + p * PAIR:W_DN + (p + 1) * PAIR]
        v_all = qkv[:, 2 * W_DN + p * PAIR:2 * W_DN + (p + 1) * PAIR]
        q_all = q_all * lax.rsqrt(_dot((q_all * q_all).astype(BF16), pmean) * HEAD_DIM + EPS) * HEAD_DIM ** -0.5
        k_all = k_all * lax.rsqrt(_dot((k_all * k_all).astype(BF16), pmean) * HEAD_DIM + EPS)
        for ci in range(nchunks):
            rows = slice(ci * c, (ci + 1) * c)
            q, k, v = q_all[rows], k_all[rows], v_all[rows]
            b0, b1 = sig[rows, h0:h0 + 1], sig[rows, h1:h1 + 1]
            c0 = cum_tok[rows, DN_A_LANE + h0:DN_A_LANE + h0 + 1]
            c1 = cum_tok[rows, DN_A_LANE + h1:DN_A_LANE + h1 + 1]
            beta = jnp.where(m0, jnp.broadcast_to(b0, (c, PAIR)), jnp.broadcast_to(b1, (c, PAIR)))
            cum_pair = jnp.where(m0, jnp.broadcast_to(c0, (c, PAIR)), jnp.broadcast_to(c1, (c, PAIR)))
            cum_col = cum_pair if n2 == PAIR else jnp.where(lo2, jnp.broadcast_to(c0, (c, n2)),
                                                            jnp.broadcast_to(c1, (c, n2)))
            ctok = cum_tok[rows]
            cum_sel = jnp.concatenate([jnp.where(lane128 ==DN_A_LANE + h0, ctok, 0.0),
                                       jnp.where(lane128 ==DN_A_LANE + h1, ctok, 0.0)], axis=0)
            cum_row = _dot_nt_exact(ones_c, cum_sel)
            gam = jnp.where(incl, jnp.exp(jnp.minimum(cum_col - cum_row, 0.0)), 0.0)
            k_st = _stack(k, m0).astype(BF16)
            kb = k * beta
            a = jnp.where(strict, _dot_nt(kb.astype(BF16), k_st) * gam, 0.0)
            attn = _dot_nt(q.astype(BF16), k_st) * gam
            ecum = jnp.exp(cum_pair)
            last_row = cum_tok[(ci + 1) * c - 1:(ci + 1) * c, :]
            last = jnp.where(m0, jnp.broadcast_to(last_row[:, DN_A_LANE + h0:DN_A_LANE + h0 + 1], (1, PAIR)),
                             jnp.broadcast_to(last_row[:, DN_A_LANE + h1:DN_A_LANE + h1 + 1], (1, PAIR)))
            items.append(dict(p=p, ci=ci, a=a, attn=attn.astype(BF16),
                              vb=_stack(v * beta, m0).astype(BF16), kbe=_stack(kb * ecum, m0).astype(BF16),
                              q_dec=(q * ecum).astype(BF16), k_dec=(k * jnp.exp(last - cum_pair)).astype(BF16),
                              dec=jnp.exp(last)))
    return items


def _dn_scan(items, proj_ref, nw_ref, o_ref, S_ref, *, TT, c, n_pairs):
    m0 = _head0_lanes()
    bm = _block_diag_mask()
    pmean = _group_mean_matrix(bm)
    nw = nw_ref[...]
    nchunks = TT // c
    S = [S_ref[p] for p in range(n_pairs)]
    outs = [[] for _ in range(n_pairs)]
    for ci in range(nchunks):
        for p in range(n_pairs):
            it = items[p * nchunks + ci]
            Sb = S[p].astype(BF16)
            v_new = it['u'] - _dot(it['w'], Sb)
            o = _dot(it['q_dec'], Sb) + _dot(it['attn'], _stack(v_new, m0).astype(BF16))
            S[p] = S[p] * it['dec'] + jnp.where(bm, _dot_tn(it['k_dec'], v_new.astype(BF16)), 0.0)
            outs[p].append(o)
    for p in range(n_pairs):
        S_ref[p] = S[p]
        o = outs[p][0] if nchunks == 1 else jnp.concatenate(outs[p], axis=0)
        gate = proj_ref[0, :, DN_CONV_CH + p * PAIR:DN_CONV_CH + (p + 1) * PAIR]
        ms = _dot((o * o).astype(BF16), pmean)
        o_ref[0, :, p * PAIR:(p + 1) * PAIR] = (o * lax.rsqrt(ms + EPS) * nw * _silu(gate)).astype(o_ref.dtype)


def delta_net(proj, conv_w, s_conv_all, a_log, dt_bias, s0_all, layer, depth, prev, norm_w, TT, c, nb, act_dtype):
    b, l, _ = proj.shape
    n_pairs = H_DN // 2
    alog = jnp.zeros((1, PAIR), F32).at[0, DN_A_LANE:DN_A_LANE + H_DN].set(a_log)
    dtb = jnp.zeros((1, PAIR), F32).at[0, DN_A_LANE:DN_A_LANE + H_DN].set(dt_bias)
    nw = jnp.concatenate([norm_w, norm_w]).reshape(1, PAIR)
    row = lambda n: pl.BlockSpec((1, n), lambda i, j: (0, 0))
    st_tail = (H_DN, HEAD_DIM, HEAD_DIM)
    cv_tail = (CONV_W - 1, DN_CONV_CH)
    in_specs = [pl.BlockSpec((nb, TT, DN_COLS), lambda i, j: (i, j, 0)),
                pl.BlockSpec((CONV_W, DN_CONV_CH), lambda i, j: (0, 0)),
                row(PAIR), row(PAIR), row(PAIR)]
    args = [proj, conv_w, alog, dtb, nw]
    has_state = s0_all is not None
    if has_state:
        in_specs += [_state_specs(st_tail, layer, nb), _state_specs(cv_tail, layer, nb)]
        args += [s0_all, s_conv_all]
    aliases = {}
    if prev is not None:
        aliases = {len(args): 1, len(args) + 1: 2}
        in_specs += [_ANY_SPEC, _ANY_SPEC]
        args += list(prev)
    return pl.pallas_call(
        functools.partial(_dn_kernel, TT=TT, c=c, n_pairs=n_pairs, has_state=has_state,
                          n_prev=0 if prev is None else 2, nb=nb),
        grid=(b // nb, l // TT),
        in_specs=in_specs,
        out_specs=[pl.BlockSpec((nb, TT, W_DN), lambda i, j: (i, j, 0)),
                   _state_specs(st_tail, layer, nb), _state_specs(cv_tail, layer, nb)],
        out_shape=[jax.ShapeDtypeStruct((b, l, W_DN), act_dtype),
                   jax.ShapeDtypeStruct((depth, b) + st_tail, F32),
                   jax.ShapeDtypeStruct((depth, b) + cv_tail, F32)],
        scratch_shapes=[pltpu.VMEM((nb, n_pairs, PAIR, PAIR), F32), pltpu.VMEM((nb, 8, DN_CONV_CH), F32)],
        input_output_aliases=aliases,
        compiler_params=_cparams(("parallel", "arbitrary")),
        name="delta_net",
    )(*args)


def _gla_kernel(*refs, TT, m, cs, n_pairs, has_state, n_prev, nb):
    n_in = 4 + (1 if has_state else 0)
    (proj_ref, wa2_ref, ba_ref, nw_ref, *st), rest = _split_refs(refs, n_in, has_state, n_prev)
    s0_ref = st[0] if has_state else None
    o_ref, sout_ref, ST_ref = rest
    t = pl.program_id(1)

    @pl.when(t == 0)
    def _():
        for i in range(nb):
            _load_pair_state(ST_ref.at[i], _batch_view(s0_ref, i), n_pairs)
            if has_state:
                for p in range(n_pairs):
                    ST_ref[i, p] = ST_ref[i, p].T

    for i in range(nb):
        _gla_body(_batch_view(proj_ref, i), wa2_ref, ba_ref, nw_ref, _batch_view(o_ref, i), ST_ref.at[i],
                  TT=TT, m=m, cs=cs, n_pairs=n_pairs)

    @pl.when(t == pl.num_programs(1) - 1)
    def _():
        for i in range(nb):
            for p in range(n_pairs):
                s = ST_ref[i, p].T
                sout_ref[i, 2 * p] = s[0:HEAD_DIM, 0:HEAD_DIM]
                sout_ref[i, 2 * p + 1] = s[HEAD_DIM:PAIR, HEAD_DIM:PAIR]


def _gla_body(proj_ref, wa2_ref, ba_ref, nw_ref, o_ref, ST_ref, *, TT, m, cs, n_pairs):
    bm = _block_diag_mask()
    m0 = _head0_lanes()
    pmean = _group_mean_matrix(bm)
    nw = nw_ref[...]
    ng = m // cs
    n2 = 2 * m
    sh = _log2(cs)
    r2 = _row_iota((n2, n2))
    c2 = _lane_iota((n2, n2))
    incl2 = ((r2 >> sh) == (c2 >> sh)) & (c2 <= r2)
    r1 = _row_iota((m, m))
    c1 = _lane_iota((m, m))
    same1 = (r1 >> sh) == (c1 >> sh)
    tril = jnp.where(same1 & (c1 <= r1), 1.0, 0.0).astype(BF16)
    ones_blk = jnp.where(same1, 1.0, 0.0).astype(BF16)
    chunk_of_row = _row_iota((m, PAIR)) >> sh

    a1 = proj_ref[0, :, 4 * W_GLA:GLA_COLS].astype(BF16)
    for p in range(n_pairs):
        lanes = slice(p * PAIR, (p + 1) * PAIR)
        logit = _dot(a1, wa2_ref[:, lanes]) + ba_ref[:, lanes]
        g_all = -_softplus(-logit) * (1.0 / GLA_TAU)
        q_all = proj_ref[0, :, p * PAIR:(p + 1) * PAIR] * HEAD_DIM ** -0.5
        k_all = proj_ref[0, :, W_GLA + p * PAIR:W_GLA + (p + 1) * PAIR]
        v_all = proj_ref[0, :, 2 * W_GLA + p * PAIR:2 * W_GLA + (p + 1) * PAIR]
        ST = ST_ref[p]
        outs = []
        for gi in range(TT // m):
            rows = slice(gi * m, (gi + 1) * m)
            q, k, v, g = q_all[rows], k_all[rows], v_all[rows], g_all[rows]
            cum = _dot_exact_lhs(tril, g)
            last = _dot_exact_lhs(ones_blk, g)
            qt = q * jnp.exp(cum)
            kt = k * jnp.exp(-cum)
            kdec = (k * jnp.exp(last - cum)).astype(BF16)
            sc = jnp.where(incl2, _dot_nt(_stack(qt, m0).astype(BF16), _stack(kt, m0).astype(BF16)), 0.0)
            o2 = _dot(sc.astype(BF16), _stack(v, m0).astype(BF16))
            o_intra = o2[:m] + o2[m:]
            qtb = qt.astype(BF16)
            if ng == 1:
                kd_big, q_big = kdec, qtb
            else:
                kd_big = jnp.concatenate([jnp.where(chunk_of_row == n, kdec, 0) for n in range(ng)], axis=1)
                q_big = jnp.concatenate([jnp.where(chunk_of_row == n, qtb, 0) for n in range(ng)], axis=1)
            kvt = _dot_tn(v.astype(BF16), kd_big)
            sts = []
            for n in range(ng):
                sts.append(ST.astype(BF16))
                dec = jnp.exp(last[n * cs:n * cs + 1, :])
                ST = ST * dec + jnp.where(bm, kvt[:, n * PAIR:(n + 1) * PAIR], 0.0)
            st_all = sts[0] if ng == 1 else jnp.concatenate(sts, axis=1)
            outs.append(o_intra + _dot_nt(q_big, st_all))
        ST_ref[p] = ST
        o = outs[0] if len(outs) == 1 else jnp.concatenate(outs, axis=0)
        gate = proj_ref[0, :, 3 * W_GLA + p * PAIR:3 * W_GLA + (p + 1) * PAIR]
        ms = _dot((o * o).astype(BF16), pmean)
        o_ref[0, :, lanes] = (o * lax.rsqrt(ms + EPS) * nw * _silu(gate)).astype(o_ref.dtype)


def gla(proj, w_a2, b_a, s0_all, layer, depth, prev, norm_w, TT, m, cs, nb, act_dtype):
    b, l, _ = proj.shape
    n_pairs = H_GLA // 2
    wa2 = jnp.zeros((PAIR, W_GLA), F32).at[:GLA_RANK].set(w_a2).astype(BF16)
    nw = jnp.concatenate([norm_w, norm_w]).reshape(1, PAIR)
    st_tail = (H_GLA, HEAD_DIM, HEAD_DIM)
    in_specs = [pl.BlockSpec((nb, TT, GLA_COLS), lambda i, j: (i, j, 0)),
                pl.BlockSpec((PAIR, W_GLA), lambda i, j: (0, 0)),
                pl.BlockSpec((1, W_GLA), lambda i, j: (0, 0)),
                pl.BlockSpec((1, PAIR), lambda i, j: (0, 0))]
    args = [proj, wa2, b_a.reshape(1, W_GLA), nw]
    if s0_all is not None:
        in_specs.append(_state_specs(st_tail, layer, nb))
        args.append(s0_all)
    aliases = {}
    if prev is not None:
        aliases[len(args)] = 1
        in_specs.append(_ANY_SPEC)
        args.append(prev)
    return pl.pallas_call(
        functools.partial(_gla_kernel, TT=TT, m=m, cs=cs, n_pairs=n_pairs, has_state=s0_all is not None,
                          n_prev=0 if prev is None else 1, nb=nb),
        grid=(b // nb, l // TT),
        in_specs=in_specs,
        out_specs=[pl.BlockSpec((nb, TT, W_GLA), lambda i, j: (i, j, 0)), _state_specs(st_tail, layer, nb)],
        out_shape=[jax.ShapeDtypeStruct((b, l, W_GLA), act_dtype),
                   jax.ShapeDtypeStruct((depth, b) + st_tail, F32)],
        scratch_shapes=[pltpu.VMEM((nb, n_pairs, PAIR, PAIR), F32)],
        input_output_aliases=aliases,
        compiler_params=_cparams(("parallel", "arbitrary")),
        name="gla",
    )(*args)


def _prep_layer_weights(li, w_in_t, w_out, w_mq, w_mo, w_ff1, w_ff2):
    wi = w_in_t[li]
    d = wi.shape[1]
    c0 = RET_COLS
    c1 = c0 + DN_CONV_CH + W_DN + 2 * H_DN
    w_ret = wi[:c0]
    w_dn = jnp.concatenate([wi[c0:c1], jnp.zeros((DN_COLS - (c1 - c0), d), F32)], axis=0)
    w_gla = jnp.concatenate([wi[c1:], jnp.zeros((GLA_COLS - (wi.shape[0] - c1), d), F32)], axis=0)
    cast = lambda a: a.astype(BF16)
    return dict(w_ret=w_ret, w_dn=w_dn, w_gla=w_gla, w_out=cast(w_out[li]),
                w_mq=w_mq[li], w_mo=cast(w_mo[li]), w_ff1=cast(w_ff1[li]), w_ff2=cast(w_ff2[li]))


def _run_trunk(x, pos, s_ret, s_dn, s_conv, s_gla, mem_k, mem_v, mem_split, lw, p, cfg):
    b, l, d = x.shape
    t = b * l
    depth = len(lw)
    xf = x.reshape(t, d)
    ret_tables = _ret_consts(cfg['ret_c']) + _rope_tables(pos)
    r_s = d_s = g_s = None
    for li in range(depth):
        w = lw[li]
        g = p['norm_mix'][li]
        pr = rms_matmul(xf, g, w['w_ret'], cfg['tm'], w_transposed=True).reshape(b, l, RET_COLS)
        pd = rms_matmul(xf, g, w['w_dn'], cfg['tm'], w_transposed=True).reshape(b, l, DN_COLS)
        pg = rms_matmul(xf, g, w['w_gla'], cfg['tm'], w_transposed=True).reshape(b, l, GLA_COLS)
        act = cfg['act_dtype']
        o_ret, r_s = retention(pr, ret_tables, s_ret, li, depth, r_s, p['ret_norm_w'][li], cfg['ret_c'], cfg['nb'], act)
        o_dn, *d_s = delta_net(pd, p['dn_conv_w'][li], s_conv, p['dn_a_log'][li], p['dn_dt_bias'][li],
                               s_dn, li, depth, d_s, p['dn_norm_w'][li], cfg['dn_tt'], cfg['dn_c'], cfg['nb'], act)
        o_gla, g_s = gla(pg, p['gla_w_a2'][li], p['gla_b_a'][li], s_gla, li, depth, g_s, p['gla_norm_w'][li],
                         cfg['gla_tt'], cfg['gla_m'], cfg['gla_cs'], cfg['nb'], act)
        xf = matmul_res([o_ret.reshape(t, W_RET), o_dn.reshape(t, W_DN), o_gla.reshape(t, W_GLA)],
                        w['w_out'], xf, cfg['tm'])
        qm = rms_matmul(xf, p['norm_cross'][li], w['w_mq'], cfg['tm'], act).reshape(b, l, d)
        if mem_split:
            att = mem_attention(qm, mem_k, mem_v, cfg['tq'], cfg['attn_nb'], layer=li)
        else:
            att = mem_attention(qm, mem_k[li], mem_v[li], cfg['tq'], cfg['attn_nb'])
        xf = matmul_res([att.reshape(t, d)], w['w_mo'], xf, cfg['tm'])
        xf = ffn(xf, p['norm_ffn'][li], w['w_ff1'], w['w_ff2'], p['norm_final'], li == depth - 1,
                 cfg['ffn_tm'], cfg['ffn_tf'])
    return (xf.reshape(b, l, d), r_s, d_s[0], d_s[1], g_s)


def kernel(x_prompt, x_sample, state_ret, state_dn, state_dn_conv, state_gla, cache_mem_k, cache_mem_v, mem_prompt, norm_mix, w_in, ret_norm_w, dn_conv_w, dn_a_log, dn_dt_bias, dn_norm_w, gla_w_a2, gla_b_a, gla_norm_w, w_out, norm_cross, norm_mem, w_mq, w_mk, w_mv, w_mo, norm_ffn, w_ff1, w_ff2, norm_final):
    b_p, l_p, d = x_prompt.shape
    b_s, l_s, _ = x_sample.shape
    depth = w_in.shape[0]
    n_mem = mem_prompt.shape[1]
    p = dict(norm_mix=norm_mix, ret_norm_w=ret_norm_w, dn_conv_w=dn_conv_w, dn_a_log=dn_a_log,
             dn_dt_bias=dn_dt_bias, dn_norm_w=dn_norm_w, gla_w_a2=gla_w_a2, gla_b_a=gla_b_a,
             gla_norm_w=gla_norm_w, norm_cross=norm_cross, norm_ffn=norm_ffn, norm_final=norm_final)
    w_in_t = jnp.transpose(w_in, (0, 2, 1))
    lw = [_prep_layer_weights(li, w_in_t, w_out, w_mq, w_mo, w_ff1, w_ff2) for li in range(depth)]

    memf = mem_prompt.reshape(b_p * n_mem, d)
    mk = [rms_matmul(memf, norm_mem[li], w_mk[li], 512).reshape(b_p, n_mem, d) for li in range(depth)]
    mv = [rms_matmul(memf, norm_mem[li], w_mv[li], 512).reshape(b_p, n_mem, d) for li in range(depth)]

    cfg_p = dict(tm=512, ret_c=256, dn_tt=256, dn_c=math.gcd(l_p, DN_CHUNK), gla_tt=256, gla_m=64,
                 gla_cs=math.gcd(l_p, GLA_CHUNK), tq=512, attn_nb=1, ffn_tm=1024, ffn_tf=1024, nb=1, act_dtype=BF16)
    y_p, p_ret, p_dn, p_conv, p_gla = _run_trunk(
        x_prompt, jnp.arange(l_p, dtype=jnp.int32), None, None, None, None, mk, mv, False, lw, p, cfg_p)

    cfg_s = dict(tm=512, ret_c=l_s, dn_tt=l_s, dn_c=math.gcd(l_s, DN_CHUNK), gla_tt=l_s,
                 gla_m=math.gcd(l_s, GLA_CHUNK), gla_cs=math.gcd(l_s, GLA_CHUNK), tq=l_s, ffn_tm=1024, ffn_tf=512,
                 attn_nb=2, nb=8, act_dtype=F32)
    y_s, s_ret, s_dn, s_conv, s_gla = _run_trunk(
        x_sample, PAST_LEN + jnp.arange(l_s, dtype=jnp.int32), state_ret, state_dn, state_dn_conv, state_gla,
        cache_row_view(cache_mem_k), cache_row_view(cache_mem_v), True, lw, p, cfg_s)

    mem_shape = (depth, b_p, n_mem, MEM_HEADS, d // MEM_HEADS)
    return (y_p, y_s, p_ret, p_dn, p_conv, p_gla, jnp.stack(mk).reshape(mem_shape), jnp.stack(mv).reshape(mem_shape),
            s_ret, s_dn, s_conv, s_gla)
```

```python
import functools
import math

import jax
import jax.numpy as jnp
from jax import lax
from jax.experimental import pallas as pl
from jax.experimental.pallas import tpu as pltpu

F32 = jnp.float32
BF16 = jnp.bfloat16

LANES = 128
HEAD_DIM = 64
PAIR = 2 * HEAD_DIM
H_RET, H_DN, H_GLA = 6, 6, 4
W_RET, W_DN, W_GLA = H_RET * HEAD_DIM, H_DN * HEAD_DIM, H_GLA * HEAD_DIM
CONV_W = 4
DN_CONV_CH = 3 * W_DN
GLA_RANK = 16
GLA_TAU = 16.0
MEM_HEADS = 4
DN_CHUNK = 64
GLA_CHUNK = 16
ROPE_BASE = 10000.0
EPS = 1e-6
PAST_LEN = 16384

DN_COLS = DN_CONV_CH + W_DN + PAIR
GLA_COLS = 4 * W_GLA + PAIR
RET_COLS = 4 * W_RET
DN_A_LANE = H_DN
TRI_BASE = 16

VMEM_LIMIT = 56 * 1024 * 1024


def _cparams(sem):
    return pltpu.CompilerParams(dimension_semantics=sem, vmem_limit_bytes=VMEM_LIMIT)


def _log2(n):
    l = int(math.log2(n))
    assert (1 << l) == n, n
    return l


def _dot(a, b):
    return jnp.dot(a, b, preferred_element_type=F32)


def _dot_nt(a, b):
    return lax.dot_general(a, b, (((1,), (1,)), ((), ())), preferred_element_type=F32)


def _dot_tn(a, b):
    return lax.dot_general(a, b, (((0,), (0,)), ((), ())), preferred_element_type=F32)


def _split2(a):
    hi = a.astype(BF16)
    lo = (a - hi.astype(F32)).astype(BF16)
    return hi, lo


def _split3(a):
    hi = a.astype(BF16)
    r = a - hi.astype(F32)
    mid = r.astype(BF16)
    lo = (r - mid.astype(F32)).astype(BF16)
    return hi, mid, lo


def _dot_exact_lhs(a_bf16, b):
    hi, mid, lo = _split3(b)
    return _dot(a_bf16, hi) + _dot(a_bf16, mid) + _dot(a_bf16, lo)


def _dot_nt_exact(a_bf16, b):
    hi, mid, lo = _split3(b)
    return _dot_nt(a_bf16, hi) + _dot_nt(a_bf16, mid) + _dot_nt(a_bf16, lo)


def _silu(x):
    return x * jax.nn.sigmoid(x)


def _softplus(x):
    return jnp.maximum(x, 0.0) + jnp.log1p(jnp.exp(-jnp.abs(x)))


def _lane_iota(shape):
    return lax.broadcasted_iota(jnp.int32, shape, len(shape) - 1)


def _row_iota(shape):
    return lax.broadcasted_iota(jnp.int32, shape, 0)


def _head0_lanes():
    return _lane_iota((1, PAIR)) < HEAD_DIM


def _stack(x, lo):
    return jnp.concatenate([jnp.where(lo, x, 0.0), jnp.where(lo, 0.0, x)], axis=0)


def _block_diag_mask():
    r = _row_iota((PAIR, PAIR))
    c = _lane_iota((PAIR, PAIR))
    return (r >> 6) == (c >> 6)


def _group_mean_matrix(bm):
    return jnp.where(bm, 1.0 / HEAD_DIM, 0.0).astype(BF16)


def _load_pair_state(S_ref, s0_ref, n_pairs):
    S_ref[...] = jnp.zeros_like(S_ref)
    if s0_ref is not None:
        for p in range(n_pairs):
            S_ref[p, 0:HEAD_DIM, 0:HEAD_DIM] = s0_ref[0, 2 * p]
            S_ref[p, HEAD_DIM:PAIR, HEAD_DIM:PAIR] = s0_ref[0, 2 * p + 1]


def _store_pair_state(sout_ref, S_ref, n_pairs):
    for p in range(n_pairs):
        sout_ref[0, 2 * p] = S_ref[p, 0:HEAD_DIM, 0:HEAD_DIM]
        sout_ref[0, 2 * p + 1] = S_ref[p, HEAD_DIM:PAIR, HEAD_DIM:PAIR]


def _split_refs(refs, n_in, has_state, n_prev):
    ins = refs[:n_in]
    refs = refs[n_in:]
    return ins, refs[n_prev:]


def _rms_matmul_kernel(x_ref, g_ref, w_ref, o_ref, wb_ref, *, w_transposed):
    @pl.when(pl.program_id(0) == 0)
    def _():
        wb_ref[...] = w_ref[...].astype(BF16)

    x = x_ref[...]
    xn = (x * lax.rsqrt(jnp.mean(x * x, axis=-1, keepdims=True) + EPS) * g_ref[...]).astype(BF16)
    y = _dot_nt(xn, wb_ref[...]) if w_transposed else _dot(xn, wb_ref[...])
    o_ref[...] = y.astype(o_ref.dtype)


def rms_matmul(x, g, w, tm, out_dtype=F32, w_transposed=False):
    t, d = x.shape
    n = w.shape[0] if w_transposed else w.shape[1]
    return pl.pallas_call(
        functools.partial(_rms_matmul_kernel, w_transposed=w_transposed),
        grid=(t // tm,),
        in_specs=[pl.BlockSpec((tm, d), lambda i: (i, 0)),
                  pl.BlockSpec((1, d), lambda i: (0, 0)),
                  pl.BlockSpec(w.shape, lambda i: (0, 0))],
        out_specs=pl.BlockSpec((tm, n), lambda i: (i, 0)),
        out_shape=jax.ShapeDtypeStruct((t, n), out_dtype),
        scratch_shapes=[pltpu.VMEM(w.shape, BF16)],
        compiler_params=_cparams(("arbitrary",)),
        name="rms_matmul",
    )(x, g.reshape(1, d), w)


def _matmul_res_kernel(*refs, widths):
    n_in = len(widths)
    a_refs = refs[:n_in]
    w_ref, r_ref, o_ref = refs[n_in:]
    acc = r_ref[...]
    off = 0
    for a_ref, wk in zip(a_refs, widths):
        acc = acc + _dot(a_ref[...].astype(BF16), w_ref[off:off + wk, :])
        off += wk
    o_ref[...] = acc


def matmul_res(a_list, w_bf16, res, tm):
    t, n = res.shape
    widths = tuple(a.shape[1] for a in a_list)
    k = w_bf16.shape[0]
    assert sum(widths) == k
    in_specs = [pl.BlockSpec((tm, wk), lambda i: (i, 0)) for wk in widths]
    in_specs += [pl.BlockSpec((k, n), lambda i: (0, 0)), pl.BlockSpec((tm, n), lambda i: (i, 0))]
    return pl.pallas_call(
        functools.partial(_matmul_res_kernel, widths=widths),
        grid=(t // tm,),
        in_specs=in_specs,
        out_specs=pl.BlockSpec((tm, n), lambda i: (i, 0)),
        out_shape=jax.ShapeDtypeStruct((t, n), F32),
        compiler_params=_cparams(("parallel",)),
        name="matmul_res",
    )(*a_list, w_bf16, res)


def _ffn_kernel(x_ref, g_ref, w1_ref, w2_ref, gf_ref, o_ref, xn_ref, acc_ref, *, final_norm):
    j = pl.program_id(1)

    @pl.when(j == 0)
    def _():
        x = x_ref[...]
        xn = x * lax.rsqrt(jnp.mean(x * x, axis=-1, keepdims=True) + EPS) * g_ref[...]
        xn_ref[...] = xn.astype(BF16)
        acc_ref[...] = x

    h = _dot(xn_ref[...], w1_ref[...])
    h = jnp.square(jnp.maximum(h, 0.0))
    acc_ref[...] += _dot(h.astype(BF16), w2_ref[...])

    @pl.when(j == pl.num_programs(1) - 1)
    def _():
        y = acc_ref[...]
        if final_norm:
            y = y * lax.rsqrt(jnp.mean(y * y, axis=-1, keepdims=True) + EPS) * gf_ref[...]
        o_ref[...] = y


def ffn(x, g, w1_bf16, w2_bf16, g_final, final_norm, tm, tf):
    t, d = x.shape
    f = w1_bf16.shape[1]
    return pl.pallas_call(
        functools.partial(_ffn_kernel, final_norm=final_norm),
        grid=(t // tm, f // tf),
        in_specs=[pl.BlockSpec((tm, d), lambda i, j: (i, 0)),
                  pl.BlockSpec((1, d), lambda i, j: (0, 0)),
                  pl.BlockSpec((d, tf), lambda i, j: (0, j)),
                  pl.BlockSpec((tf, d), lambda i, j: (j, 0)),
                  pl.BlockSpec((1, d), lambda i, j: (0, 0))],
        out_specs=pl.BlockSpec((tm, d), lambda i, j: (i, 0)),
        out_shape=jax.ShapeDtypeStruct((t, d), F32),
        scratch_shapes=[pltpu.VMEM((tm, d), BF16), pltpu.VMEM((tm, d), F32)],
        compiler_params=_cparams(("parallel", "arbitrary")),
        name="ffn",
    )(x, g.reshape(1, d), w1_bf16, w2_bf16, g_final.reshape(1, d))


def _attn_kernel(q_ref, k_ref, v_ref, o_ref, *, n_heads, dh, row_view, nb):
    scale = dh ** -0.5
    lane_tiles = dh // LANES
    rows_per_token = n_heads * lane_tiles

    def head_slab(ref, i, h):
        if not row_view:
            return ref[i, :, h * dh:(h + 1) * dh].astype(BF16)
        n_mem = ref.shape[1] // rows_per_token
        parts = [ref[i, pl.ds(j * n_heads + h, n_mem, stride=rows_per_token), :] for j in range(lane_tiles)]
        return jnp.concatenate(parts, axis=1).astype(BF16)

    for i in range(nb):
        for h in range(n_heads):
            sl = slice(h * dh, (h + 1) * dh)
            q = q_ref[i, :, sl].astype(BF16)
            s = _dot_nt(q, head_slab(k_ref, i, h)) * scale
            m = jnp.max(s, axis=-1, keepdims=True)
            e = jnp.exp(s - m)
            l = jnp.sum(e, axis=-1, keepdims=True)
            o_ref[i, :, sl] = (_dot(e.astype(BF16), head_slab(v_ref, i, h)) / l).astype(o_ref.dtype)


def mem_attention(q, mem_k, mem_v, tq, nb, layer=None):
    b, l, d = q.shape
    dh = d // MEM_HEADS
    if layer is None:
        m = mem_k.shape[1]
        mem_spec = pl.BlockSpec((nb, m, d), lambda i, j: (i, 0, 0))
    else:
        rows = mem_k.shape[1]
        mem_spec = pl.BlockSpec((nb, rows, LANES), lambda i, j: (layer * (b // nb) + i, 0, 0))
    return pl.pallas_call(
        functools.partial(_attn_kernel, n_heads=MEM_HEADS, dh=dh, row_view=layer is not None, nb=nb),
        grid=(b // nb, l // tq),
        in_specs=[pl.BlockSpec((nb, tq, d), lambda i, j: (i, j, 0)), mem_spec, mem_spec],
        out_specs=pl.BlockSpec((nb, tq, d), lambda i, j: (i, j, 0)),
        out_shape=jax.ShapeDtypeStruct((b, l, d), q.dtype),
        compiler_params=_cparams(("parallel", "arbitrary")),
        name="mem_attention",
    )(q, mem_k, mem_v)


def cache_row_view(cache):
    depth, b, m, nh, dh = cache.shape
    c = cache.reshape(depth, b, m, nh, dh // LANES, LANES).transpose(0, 1, 2, 4, 3, 5)
    return c.reshape(depth * b, m * nh * (dh // LANES), LANES)


def _state_specs(shape_tail, layer, nb):
    nd = len(shape_tail)
    return pl.BlockSpec((None, nb) + tuple(shape_tail), lambda i, j: (layer, i) + (0,) * nd)


_ANY_SPEC = pl.BlockSpec(memory_space=pl.ANY)


def _batch_view(ref, i):
    return None if ref is None else ref.at[pl.ds(i, 1)]


def _ret_kernel(*refs, C, n_pairs, has_state, n_prev, nb):
    n_in = 8 + (1 if has_state else 0)
    (proj_ref, cos_ref, sin_ref, dm_ref, xi_ref, zeta_ref, gc_ref, lnw_ref, *st), rest = \
        _split_refs(refs, n_in, has_state, n_prev)
    s0_ref = st[0] if has_state else None
    o_ref, sout_ref, S_ref = rest
    t = pl.program_id(1)

    @pl.when(t == 0)
    def _():
        for i in range(nb):
            _load_pair_state(S_ref.at[i], _batch_view(s0_ref, i), n_pairs)

    for i in range(nb):
        _ret_body(_batch_view(proj_ref, i), cos_ref, sin_ref, dm_ref, xi_ref, zeta_ref, gc_ref, lnw_ref,
                  _batch_view(o_ref, i), S_ref.at[i], C=C, n_pairs=n_pairs)

    @pl.when(t == pl.num_programs(1) - 1)
    def _():
        for i in range(nb):
            _store_pair_state(_batch_view(sout_ref, i), S_ref.at[i], n_pairs)


def _ret_body(proj_ref, cos_ref, sin_ref, dm_ref, xi_ref, zeta_ref, gc_ref, lnw_ref, o_ref, S_ref, *, C, n_pairs):
    m0 = _head0_lanes()
    first_half = (_lane_iota((1, PAIR)) & (HEAD_DIM - 1)) < HEAD_DIM // 2
    bm = _block_diag_mask()
    pmean = _group_mean_matrix(bm)
    cos = cos_ref[...]
    sin = sin_ref[...]
    lnw = lnw_ref[...]

    def rot(x):
        sw = jnp.where(first_half, pltpu.roll(x, PAIR - HEAD_DIM // 2, 1), pltpu.roll(x, HEAD_DIM // 2, 1))
        return x * cos + sw * sin

    for p in range(n_pairs):
        lanes = slice(p * PAIR, (p + 1) * PAIR)
        q = rot(proj_ref[0, :, p * PAIR:(p + 1) * PAIR])
        k = rot(proj_ref[0, :, W_RET + p * PAIR:W_RET + (p + 1) * PAIR]) * HEAD_DIM ** -0.5
        v = proj_ref[0, :, 2 * W_RET + p * PAIR:2 * W_RET + (p + 1) * PAIR]
        gate = proj_ref[0, :, 3 * W_RET + p * PAIR:3 * W_RET + (p + 1) * PAIR]
        kb = k.astype(BF16)
        vb = v.astype(BF16)
        sc = _dot_nt(_stack(q, m0).astype(BF16), kb) * dm_ref[p]
        o2 = _dot(sc.astype(BF16), vb)
        o_intra = jnp.where(m0, o2[:C], o2[C:])
        S = S_ref[p]
        o_inter = _dot(q.astype(BF16), S.astype(BF16)) * xi_ref[p]
        kv = _dot_tn((k * zeta_ref[p]).astype(BF16), vb)
        S_ref[p] = S * gc_ref[p][0:1, :] + jnp.where(bm, kv, 0.0)
        o = o_intra + o_inter
        oc = o - _dot(o.astype(BF16), pmean)
        var = _dot((oc * oc).astype(BF16), pmean)
        o_ref[0, :, lanes] = (oc * lax.rsqrt(var + EPS) * lnw * _silu(gate)).astype(o_ref.dtype)


def _ret_consts(C):
    lg = jnp.log1p(-jnp.exp2(-5.0 - jnp.arange(H_RET, dtype=F32)))
    idx = jnp.arange(C, dtype=F32)
    diff = idx[:, None] - idx[None, :]
    dmask = jnp.where(diff >= 0, jnp.exp(lg[:, None, None] * jnp.maximum(diff, 0.0)), 0.0)
    dm = dmask.reshape(H_RET // 2, 2 * C, C)
    xi = jnp.exp(lg[:, None] * (idx + 1.0))
    zeta = jnp.exp(lg[:, None] * (C - 1.0 - idx))
    gch = jnp.exp(lg * C)

    def pair_lanes(a):
        r = a.shape[1]
        a = jnp.repeat(a[:, :, None], HEAD_DIM, axis=2)
        a = a.reshape(H_RET // 2, 2, r, HEAD_DIM).transpose(0, 2, 1, 3)
        return a.reshape(H_RET // 2, r, PAIR)

    gc = pair_lanes(jnp.broadcast_to(gch[:, None], (H_RET, 8)))
    return dm, pair_lanes(xi), pair_lanes(zeta), gc


def _rope_tables(pos):
    half = HEAD_DIM // 2
    inv_freq = ROPE_BASE ** (-jnp.arange(half, dtype=F32) / half)
    ang = pos.astype(F32)[:, None] * inv_freq[None, :]
    cos, sin = jnp.cos(ang), jnp.sin(ang)
    cos_t = jnp.concatenate([cos, cos, cos, cos], axis=-1)
    sin_t = jnp.concatenate([-sin, sin, -sin, sin], axis=-1)
    return cos_t, sin_t


def retention(proj, tables, s0_all, layer, depth, prev, ln_w, C, nb, act_dtype):
    b, l, _ = proj.shape
    n_pairs = H_RET // 2
    dm, xi, zeta, gc, cos_t, sin_t = tables
    lnw = jnp.concatenate([ln_w, ln_w]).reshape(1, PAIR)
    const3 = lambda shape: pl.BlockSpec(shape, lambda i, j: (0, 0, 0))
    st_tail = (H_RET, HEAD_DIM, HEAD_DIM)
    in_specs = [pl.BlockSpec((nb, C, RET_COLS), lambda i, j: (i, j, 0)),
                pl.BlockSpec((C, PAIR), lambda i, j: (j, 0)),
                pl.BlockSpec((C, PAIR), lambda i, j: (j, 0)),
                const3((n_pairs, 2 * C, C)),
                const3((n_pairs, C, PAIR)),
                const3((n_pairs, C, PAIR)),
                const3((n_pairs, 8, PAIR)),
                pl.BlockSpec((1, PAIR), lambda i, j: (0, 0))]
    args = [proj, cos_t, sin_t, dm, xi, zeta, gc, lnw]
    if s0_all is not None:
        in_specs.append(_state_specs(st_tail, layer, nb))
        args.append(s0_all)
    aliases = {}
    if prev is not None:
        aliases[len(args)] = 1
        in_specs.append(_ANY_SPEC)
        args.append(prev)
    return pl.pallas_call(
        functools.partial(_ret_kernel, C=C, n_pairs=n_pairs, has_state=s0_all is not None,
                          n_prev=0 if prev is None else 1, nb=nb),
        grid=(b // nb, l // C),
        in_specs=in_specs,
        out_specs=[pl.BlockSpec((nb, C, W_RET), lambda i, j: (i, j, 0)), _state_specs(st_tail, layer, nb)],
        out_shape=[jax.ShapeDtypeStruct((b, l, W_RET), act_dtype),
                   jax.ShapeDtypeStruct((depth, b) + st_tail, F32)],
        scratch_shapes=[pltpu.VMEM((nb, n_pairs, PAIR, PAIR), F32)],
        input_output_aliases=aliases,
        compiler_params=_cparams(("parallel", "arbitrary")),
        name="retention",
    )(*args)


def _tri_inverse_many(a_list, c, lo, ii, jj):
    def mm(x, y):
        return _dot(x.astype(BF16), _stack(y, lo).astype(BF16))

    base = min(c, TRI_BASE)
    lb = _log2(base)
    dblk = (ii >> lb) == (jj >> lb)
    eye = jnp.where(ii == jj, 1.0, 0.0)
    xs = [jnp.where(dblk, -a, 0.0) for a in a_list]
    ts = [eye + x for x in xs]
    for _ in range(lb - 1):
        xs = [mm(x, x) for x in xs]
        ts = [t + mm(t, x) for t, x in zip(ts, xs)]
    blk = base
    while blk < c:
        inner = (ii >> _log2(blk)) == (jj >> _log2(blk))
        outer = (ii >> _log2(2 * blk)) == (jj >> _log2(2 * blk))
        sel = outer & jnp.logical_not(inner)
        ls = [jnp.where(sel, a, 0.0) for a in a_list]
        ms = [mm(l, t) for l, t in zip(ls, ts)]
        ts = [t - mm(t, m) for t, m in zip(ts, ms)]
        blk *= 2
    return ts


def _dn_kernel(*refs, TT, c, n_pairs, has_state, n_prev, nb):
    n_in = 5 + (2 if has_state else 0)
    (proj_ref, convw_ref, alog_ref, dtb_ref, nw_ref, *st), rest = _split_refs(refs, n_in, has_state, n_prev)
    s0_ref, sconv_ref = st if has_state else (None, None)
    o_ref, sout_ref, convout_ref, S_ref, tail_ref = rest
    n2 = 2 * c
    ii = _row_iota((c, n2))
    lane2 = _lane_iota((c, n2))
    t = pl.program_id(1)

    @pl.when(t == 0)
    def _():
        tail_ref[...] = jnp.zeros_like(tail_ref)
        for i in range(nb):
            _load_pair_state(S_ref.at[i], _batch_view(s0_ref, i), n_pairs)
            if has_state:
                tail_ref[i, 8 - (CONV_W - 1):8, :] = sconv_ref[i]

    items = []
    for i in range(nb):
        items.append(_dn_prepare(_batch_view(proj_ref, i), convw_ref, alog_ref, dtb_ref, tail_ref.at[i], ii, lane2,
                                 TT=TT, c=c, n_pairs=n_pairs))
    flat = [it for batch_items in items for it in batch_items]
    tinvs = _tri_inverse_many([it['a'] for it in flat], c, lane2 < c, ii, lane2 & (c - 1))
    for it, tinv in zip(flat, tinvs):
        tb = tinv.astype(BF16)
        it['u'] = _dot(tb, it['vb'])
        it['w'] = _dot(tb, it['kbe']).astype(BF16)
    for i in range(nb):
        _dn_scan(items[i], _batch_view(proj_ref, i), nw_ref, _batch_view(o_ref, i), S_ref.at[i],
                 TT=TT, c=c, n_pairs=n_pairs)

    @pl.when(t == pl.num_programs(1) - 1)
    def _():
        for i in range(nb):
            _store_pair_state(_batch_view(sout_ref, i), S_ref.at[i], n_pairs)
            convout_ref[i] = tail_ref[i, 8 - (CONV_W - 1):8, :]


def _dn_prepare(proj_ref, convw_ref, alog_ref, dtb_ref, tail_ref, ii, lane2, *, TT, c, n_pairs):
    m0 = _head0_lanes()
    pmean = _group_mean_matrix(_block_diag_mask())
    nchunks = TT // c
    lc = _log2(c)

    x = proj_ref[0, :, 0:DN_CONV_CH]
    tail = tail_ref[...]
    rows8 = _row_iota((8, DN_CONV_CH))
    acc = x * convw_ref[CONV_W - 1:CONV_W, :]
    for s in range(1, CONV_W):
        xs = pltpu.roll(x, s, 0)
        top = jnp.where(rows8 < s, pltpu.roll(tail, s, 0), xs[0:8])
        xs = top if TT == 8 else jnp.concatenate([top, xs[8:]], axis=0)
        acc = acc + xs * convw_ref[CONV_W - 1 - s:CONV_W - s, :]
    tail_ref[...] = x[TT - 8:TT]
    qkv = _silu(acc)

    misc = proj_ref[0, :, DN_CONV_CH + W_DN:DN_COLS]
    sig = jax.nn.sigmoid(misc)
    glog = -jnp.exp(alog_ref[...]) * _softplus(misc + dtb_ref[...])
    rt = _row_iota((TT, TT))
    ct = _lane_iota((TT, TT))
    tril_t = jnp.where(((rt >> lc) == (ct >> lc)) & (ct <= rt), 1.0, 0.0).astype(BF16)
    cum_tok = _dot_exact_lhs(tril_t, glog)

    n2 = 2 * c
    jj = lane2 & (c - 1)
    lo2 = lane2 < c
    incl = jj <= ii
    strict = jj < ii
    ones_c = jnp.ones((c, PAIR), BF16)
    lane128 = _lane_iota((c, PAIR))

    items = []
    for p in range(n_pairs):
        h0, h1 = 2 * p, 2 * p + 1
        q_all = qkv[:, p * PAIR:(p + 1) * PAIR]
        k_all = qkv[:, W_DN + p * PAIR:W_DN + (p + 1) * PAIR]
        v_all = qkv[:, 2 * W_DN + p * PAIR:2 * W_DN + (p + 1) * PAIR]
        q_all = q_all * lax.rsqrt(_dot((q_all * q_all).astype(BF16), pmean) * HEAD_DIM + EPS) * HEAD_DIM ** -0.5
        k_all = k_all * lax.rsqrt(_dot((k_all * k_all).astype(BF16), pmean) * HEAD_DIM + EPS)
        for ci in range(nchunks):
            rows = slice(ci * c, (ci + 1) * c)
            q, k, v = q_all[rows], k_all[rows], v_all[rows]
            b0, b1 = sig[rows, h0:h0 + 1], sig[rows, h1:h1 + 1]
            c0 = cum_tok[rows, DN_A_LANE + h0:DN_A_LANE + h0 + 1]
            c1 = cum_tok[rows, DN_A_LANE + h1:DN_A_LANE + h1 + 1]
            beta = jnp.where(m0, jnp.broadcast_to(b0, (c, PAIR)), jnp.broadcast_to(b1, (c, PAIR)))
            cum_pair = jnp.where(m0, jnp.broadcast_to(c0, (c, PAIR)), jnp.broadcast_to(c1, (c, PAIR)))
            cum_col = cum_pair if n2 == PAIR else jnp.where(lo2, jnp.broadcast_to(c0, (c, n2)),
                                                            jnp.broadcast_to(c1, (c, n2)))
            ctok = cum_tok[rows]
            cum_sel = jnp.concatenate([jnp.where(lane128 ==DN_A_LANE + h0, ctok, 0.0),
                                       jnp.where(lane128 ==DN_A_LANE + h1, ctok, 0.0)], axis=0)
            cum_row = _dot_nt_exact(ones_c, cum_sel)
            gam = jnp.where(incl, jnp.exp(jnp.minimum(cum_col - cum_row, 0.0)), 0.0)
            k_st = _stack(k, m0).astype(BF16)
            kb = k * beta
            a = jnp.where(strict, _dot_nt(kb.astype(BF16), k_st) * gam, 0.0)
            attn = _dot_nt(q.astype(BF16), k_st) * gam
            ecum = jnp.exp(cum_pair)
            last_row = cum_tok[(ci + 1) * c - 1:(ci + 1) * c, :]
            last = jnp.where(m0, jnp.broadcast_to(last_row[:, DN_A_LANE + h0:DN_A_LANE + h0 + 1], (1, PAIR)),
                             jnp.broadcast_to(last_row[:, DN_A_LANE + h1:DN_A_LANE + h1 + 1], (1, PAIR)))
            items.append(dict(p=p, ci=ci, a=a, attn=attn.astype(BF16),
                              vb=_stack(v * beta, m0).astype(BF16), kbe=_stack(kb * ecum, m0).astype(BF16),
                              q_dec=(q * ecum).astype(BF16), k_dec=(k * jnp.exp(last - cum_pair)).astype(BF16),
                              dec=jnp.exp(last)))
    return items


def _dn_scan(items, proj_ref, nw_ref, o_ref, S_ref, *, TT, c, n_pairs):
    m0 = _head0_lanes()
    bm = _block_diag_mask()
    pmean = _group_mean_matrix(bm)
    nw = nw_ref[...]
    nchunks = TT // c
    S = [S_ref[p] for p in range(n_pairs)]
    outs = [[] for _ in range(n_pairs)]
    for ci in range(nchunks):
        for p in range(n_pairs):
            it = items[p * nchunks + ci]
            Sb = S[p].astype(BF16)
            v_new = it['u'] - _dot(it['w'], Sb)
            o = _dot(it['q_dec'], Sb) + _dot(it['attn'], _stack(v_new, m0).astype(BF16))
            S[p] = S[p] * it['dec'] + jnp.where(bm, _dot_tn(it['k_dec'], v_new.astype(BF16)), 0.0)
            outs[p].append(o)
    for p in range(n_pairs):
        S_ref[p] = S[p]
        o = outs[p][0] if nchunks == 1 else jnp.concatenate(outs[p], axis=0)
        gate = proj_ref[0, :, DN_CONV_CH + p * PAIR:DN_CONV_CH + (p + 1) * PAIR]
        ms = _dot((o * o).astype(BF16), pmean)
        o_ref[0, :, p * PAIR:(p + 1) * PAIR] = (o * lax.rsqrt(ms + EPS) * nw * _silu(gate)).astype(o_ref.dtype)


def delta_net(proj, conv_w, s_conv_all, a_log, dt_bias, s0_all, layer, depth, prev, norm_w, TT, c, nb, act_dtype):
    b, l, _ = proj.shape
    n_pairs = H_DN // 2
    alog = jnp.zeros((1, PAIR), F32).at[0, DN_A_LANE:DN_A_LANE + H_DN].set(a_log)
    dtb = jnp.zeros((1, PAIR), F32).at[0, DN_A_LANE:DN_A_LANE + H_DN].set(dt_bias)
    nw = jnp.concatenate([norm_w, norm_w]).reshape(1, PAIR)
    row = lambda n: pl.BlockSpec((1, n), lambda i, j: (0, 0))
    st_tail = (H_DN, HEAD_DIM, HEAD_DIM)
    cv_tail = (CONV_W - 1, DN_CONV_CH)
    in_specs = [pl.BlockSpec((nb, TT, DN_COLS), lambda i, j: (i, j, 0)),
                pl.BlockSpec((CONV_W, DN_CONV_CH), lambda i, j: (0, 0)),
                row(PAIR), row(PAIR), row(PAIR)]
    args = [proj, conv_w, alog, dtb, nw]
    has_state = s0_all is not None
    if has_state:
        in_specs += [_state_specs(st_tail, layer, nb), _state_specs(cv_tail, layer, nb)]
        args += [s0_all, s_conv_all]
    aliases = {}
    if prev is not None:
        aliases = {len(args): 1, len(args) + 1: 2}
        in_specs += [_ANY_SPEC, _ANY_SPEC]
        args += list(prev)
    return pl.pallas_call(
        functools.partial(_dn_kernel, TT=TT, c=c, n_pairs=n_pairs, has_state=has_state,
                          n_prev=0 if prev is None else 2, nb=nb),
        grid=(b // nb, l // TT),
        in_specs=in_specs,
        out_specs=[pl.BlockSpec((nb, TT, W_DN), lambda i, j: (i, j, 0)),
                   _state_specs(st_tail, layer, nb), _state_specs(cv_tail, layer, nb)],
        out_shape=[jax.ShapeDtypeStruct((b, l, W_DN), act_dtype),
                   jax.ShapeDtypeStruct((depth, b) + st_tail, F32),
                   jax.ShapeDtypeStruct((depth, b) + cv_tail, F32)],
        scratch_shapes=[pltpu.VMEM((nb, n_pairs, PAIR, PAIR), F32), pltpu.VMEM((nb, 8, DN_CONV_CH), F32)],
        input_output_aliases=aliases,
        compiler_params=_cparams(("parallel", "arbitrary")),
        name="delta_net",
    )(*args)


def _gla_kernel(*refs, TT, m, cs, n_pairs, has_state, n_prev, nb):
    n_in = 4 + (1 if has_state else 0)
    (proj_ref, wa2_ref, ba_ref, nw_ref, *st), rest = _split_refs(refs, n_in, has_state, n_prev)
    s0_ref = st[0] if has_state else None
    o_ref, sout_ref, ST_ref = rest
    t = pl.program_id(1)

    @pl.when(t == 0)
    def _():
        for i in range(nb):
            _load_pair_state(ST_ref.at[i], _batch_view(s0_ref, i), n_pairs)
            if has_state:
                for p in range(n_pairs):
                    ST_ref[i, p] = ST_ref[i, p].T

    for i in range(nb):
        _gla_body(_batch_view(proj_ref, i), wa2_ref, ba_ref, nw_ref, _batch_view(o_ref, i), ST_ref.at[i],
                  TT=TT, m=m, cs=cs, n_pairs=n_pairs)

    @pl.when(t == pl.num_programs(1) - 1)
    def _():
        for i in range(nb):
            for p in range(n_pairs):
                s = ST_ref[i, p].T
                sout_ref[i, 2 * p] = s[0:HEAD_DIM, 0:HEAD_DIM]
                sout_ref[i, 2 * p + 1] = s[HEAD_DIM:PAIR, HEAD_DIM:PAIR]


def _gla_body(proj_ref, wa2_ref, ba_ref, nw_ref, o_ref, ST_ref, *, TT, m, cs, n_pairs):
    bm = _block_diag_mask()
    m0 = _head0_lanes()
    pmean = _group_mean_matrix(bm)
    nw = nw_ref[...]
    ng = m // cs
    n2 = 2 * m
    sh = _log2(cs)
    r2 = _row_iota((n2, n2))
    c2 = _lane_iota((n2, n2))
    incl2 = ((r2 >> sh) == (c2 >> sh)) & (c2 <= r2)
    r1 = _row_iota((m, m))
    c1 = _lane_iota((m, m))
    same1 = (r1 >> sh) == (c1 >> sh)
    tril = jnp.where(same1 & (c1 <= r1), 1.0, 0.0).astype(BF16)
    ones_blk = jnp.where(same1, 1.0, 0.0).astype(BF16)
    chunk_of_row = _row_iota((m, PAIR)) >> sh

    a1 = proj_ref[0, :, 4 * W_GLA:GLA_COLS].astype(BF16)
    for p in range(n_pairs):
        lanes = slice(p * PAIR, (p + 1) * PAIR)
        logit = _dot(a1, wa2_ref[:, lanes]) + ba_ref[:, lanes]
        g_all = -_softplus(-logit) * (1.0 / GLA_TAU)
        q_all = proj_ref[0, :, p * PAIR:(p + 1) * PAIR] * HEAD_DIM ** -0.5
        k_all = proj_ref[0, :, W_GLA + p * PAIR:W_GLA + (p + 1) * PAIR]
        v_all = proj_ref[0, :, 2 * W_GLA + p * PAIR:2 * W_GLA + (p + 1) * PAIR]
        ST = ST_ref[p]
        outs = []
        for gi in range(TT // m):
            rows = slice(gi * m, (gi + 1) * m)
            q, k, v, g = q_all[rows], k_all[rows], v_all[rows], g_all[rows]
            cum = _dot_exact_lhs(tril, g)
            last = _dot_exact_lhs(ones_blk, g)
            qt = q * jnp.exp(cum)
            kt = k * jnp.exp(-cum)
            kdec = (k * jnp.exp(last - cum)).astype(BF16)
            sc = jnp.where(incl2, _dot_nt(_stack(qt, m0).astype(BF16), _stack(kt, m0).astype(BF16)), 0.0)
            o2 = _dot(sc.astype(BF16), _stack(v, m0).astype(BF16))
            o_intra = o2[:m] + o2[m:]
            qtb = qt.astype(BF16)
            if ng == 1:
                kd_big, q_big = kdec, qtb
            else:
                kd_big = jnp.concatenate([jnp.where(chunk_of_row == n, kdec, 0) for n in range(ng)], axis=1)
                q_big = jnp.concatenate([jnp.where(chunk_of_row == n, qtb, 0) for n in range(ng)], axis=1)
            kvt = _dot_tn(v.astype(BF16), kd_big)
            sts = []
            for n in range(ng):
                sts.append(ST.astype(BF16))
                dec = jnp.exp(last[n * cs:n * cs + 1, :])
                ST = ST * dec + jnp.where(bm, kvt[:, n * PAIR:(n + 1) * PAIR], 0.0)
            st_all = sts[0] if ng == 1 else jnp.concatenate(sts, axis=1)
            outs.append(o_intra + _dot_nt(q_big, st_all))
        ST_ref[p] = ST
        o = outs[0] if len(outs) == 1 else jnp.concatenate(outs, axis=0)
        gate = proj_ref[0, :, 3 * W_GLA + p * PAIR:3 * W_GLA + (p + 1) * PAIR]
        ms = _dot((o * o).astype(BF16), pmean)
        o_ref[0, :, lanes] = (o * lax.rsqrt(ms + EPS) * nw * _silu(gate)).astype(o_ref.dtype)


def gla(proj, w_a2, b_a, s0_all, layer, depth, prev, norm_w, TT, m, cs, nb, act_dtype):
    b, l, _ = proj.shape
    n_pairs = H_GLA // 2
    wa2 = jnp.zeros((PAIR, W_GLA), F32).at[:GLA_RANK].set(w_a2).astype(BF16)
    nw = jnp.concatenate([norm_w, norm_w]).reshape(1, PAIR)
    st_tail = (H_GLA, HEAD_DIM, HEAD_DIM)
    in_specs = [pl.BlockSpec((nb, TT, GLA_COLS), lambda i, j: (i, j, 0)),
                pl.BlockSpec((PAIR, W_GLA), lambda i, j: (0, 0)),
                pl.BlockSpec((1, W_GLA), lambda i, j: (0, 0)),
                pl.BlockSpec((1, PAIR), lambda i, j: (0, 0))]
    args = [proj, wa2, b_a.reshape(1, W_GLA), nw]
    if s0_all is not None:
        in_specs.append(_state_specs(st_tail, layer, nb))
        args.append(s0_all)
    aliases = {}
    if prev is not None:
        aliases[len(args)] = 1
        in_specs.append(_ANY_SPEC)
        args.append(prev)
    return pl.pallas_call(
        functools.partial(_gla_kernel, TT=TT, m=m, cs=cs, n_pairs=n_pairs, has_state=s0_all is not None,
                          n_prev=0 if prev is None else 1, nb=nb),
        grid=(b // nb, l // TT),
        in_specs=in_specs,
        out_specs=[pl.BlockSpec((nb, TT, W_GLA), lambda i, j: (i, j, 0)), _state_specs(st_tail, layer, nb)],
        out_shape=[jax.ShapeDtypeStruct((b, l, W_GLA), act_dtype),
                   jax.ShapeDtypeStruct((depth, b) + st_tail, F32)],
        scratch_shapes=[pltpu.VMEM((nb, n_pairs, PAIR, PAIR), F32)],
        input_output_aliases=aliases,
        compiler_params=_cparams(("parallel", "arbitrary")),
        name="gla",
    )(*args)


def _proj_t_kernel(x_ref, g_ref, w_ref, o_ref, xn_ref):
    @pl.when(pl.program_id(0) == 0)
    def _():
        x = x_ref[...]
        xn = x * lax.rsqrt(jnp.mean(x * x, axis=0, keepdims=True) + EPS) * g_ref[...]
        xn_ref[...] = xn.astype(BF16)

    o_ref[...] = _dot(w_ref[...].astype(BF16), xn_ref[...])


def proj_t(x_t, g, w_t, tn):
    d, t = x_t.shape
    n = w_t.shape[0]
    return pl.pallas_call(
        _proj_t_kernel,
        grid=(n // tn,),
        in_specs=[pl.BlockSpec((d, t), lambda i: (0, 0)),
                  pl.BlockSpec((d, 1), lambda i: (0, 0)),
                  pl.BlockSpec((tn, d), lambda i: (i, 0))],
        out_specs=pl.BlockSpec((tn, t), lambda i: (i, 0)),
        out_shape=jax.ShapeDtypeStruct((n, t), F32),
        scratch_shapes=[pltpu.VMEM((d, t), BF16)],
        compiler_params=_cparams(("arbitrary",)),
        name="proj_t",
    )(x_t, g.reshape(d, 1), w_t)


SUBLANES = 8


def _row_group(ref, g, lanes):
    return ref[pl.ds(pl.multiple_of(g * SUBLANES, SUBLANES), SUBLANES), lanes]


def _pick_row(block, r):
    return jnp.sum(jnp.where(_row_iota(block.shape) == r, block, 0.0), axis=0, keepdims=True)


def _lane_linear_kernel(*refs, n_tok, nbatch, rotary, layer_norm, row_decay, n_prev):
    if rotary:
        q_ref, k_ref, v_ref, gate_ref, cos_ref, sin_ref, dec_ref, nw_ref, s0_ref = refs[:9]
        rest = refs[9 + n_prev:]
    else:
        q_ref, k_ref, v_ref, gate_ref, a1_ref, wa2_ref, ba_ref, nw_ref, s0_ref = refs[:9]
        rest = refs[9 + n_prev:]
    o_ref, sout_ref, q_scr, k_scr, dec_scr = rest
    if rotary:
        half = HEAD_DIM // 2

        def rot(x):
            sw = jnp.concatenate([x[half:], x[:half]], axis=0)
            return x * cos_ref[...] + sw * sin_ref[...]

        q_scr[...] = rot(q_ref[...])
        k_scr[...] = rot(k_ref[...]) * HEAD_DIM ** -0.5
    else:
        q_scr[...] = q_ref[...] * HEAD_DIM ** -0.5
        k_scr[...] = k_ref[...]
        logit = _dot(wa2_ref[...], a1_ref[...].astype(BF16)) + ba_ref[...]
        dec_scr[...] = jnp.exp(-_softplus(-logit) * (1.0 / GLA_TAU))
    nw = nw_ref[...]
    for t in range(n_tok):
        lanes = slice(t * nbatch, (t + 1) * nbatch)
        v_t = v_ref[:, lanes]
        src = s0_ref if t == 0 else sout_ref

        def body(g, o, src=src, lanes=lanes, v_t=v_t):
            kg = _row_group(k_scr, g, lanes)
            qg = _row_group(q_scr, g, lanes)
            dg = _row_group(dec_scr, g, lanes) if row_decay else None
            for i in range(SUBLANES):
                d = g * SUBLANES + i
                dec = dg[i:i + 1] if row_decay else dec_ref[...]
                s_d = src[d] * dec + kg[i:i + 1] * v_t
                sout_ref[d] = s_d
                o = o + qg[i:i + 1] * s_d
            return o

        o = lax.fori_loop(0, HEAD_DIM // SUBLANES, body, jnp.zeros((HEAD_DIM, nbatch), F32))
        if layer_norm:
            o = o - jnp.mean(o, axis=0, keepdims=True)
        o = o * lax.rsqrt(jnp.mean(o * o, axis=0, keepdims=True) + EPS) * nw
        o_ref[:, lanes] = o * _silu(gate_ref[:, lanes])


def _head_rows(width, col0):
    return pl.BlockSpec((HEAD_DIM, width), lambda h: (col0 // HEAD_DIM + h, 0))


def lane_linear_mixer(proj_t_all, col0, width, n_heads, kind, consts, s_all, layer, depth, prev, n_tok, nbatch):
    t_all = n_tok * nbatch
    rotary = kind == 'ret'
    hr = lambda off: _head_rows(t_all, col0 + off)
    state_spec = pl.BlockSpec((None, None, HEAD_DIM, HEAD_DIM, nbatch), lambda h: (layer, h, 0, 0, 0))
    in_specs = [hr(0), hr(width), hr(2 * width), hr(3 * width)]
    args = [proj_t_all] * 4
    if rotary:
        cos_t, sin_t, gam, nw = consts
        in_specs += [pl.BlockSpec((HEAD_DIM, t_all), lambda h: (0, 0)),
                     pl.BlockSpec((HEAD_DIM, t_all), lambda h: (0, 0)),
                     pl.BlockSpec((None, 1, nbatch), lambda h: (h, 0, 0)),
                     pl.BlockSpec((HEAD_DIM, nbatch), lambda h: (0, 0))]
        args += [cos_t, sin_t, gam, nw]
    else:
        wa2_t, ba_t, nw = consts
        in_specs += [pl.BlockSpec((LANES, t_all), lambda h: ((col0 + 4 * width) // LANES, 0)),
                     pl.BlockSpec((HEAD_DIM, LANES), lambda h: (h, 0)),
                     pl.BlockSpec((HEAD_DIM, 1), lambda h: (h, 0)),
                     pl.BlockSpec((HEAD_DIM, nbatch), lambda h: (0, 0))]
        args += [proj_t_all, wa2_t, ba_t, nw]
    in_specs.append(state_spec)
    args.append(s_all)
    aliases = {}
    if prev is not None:
        aliases[len(args)] = 1
        in_specs.append(_ANY_SPEC)
        args.append(prev)
    return pl.pallas_call(
        functools.partial(_lane_linear_kernel, n_tok=n_tok, nbatch=nbatch, rotary=rotary, layer_norm=rotary,
                          row_decay=not rotary, n_prev=0 if prev is None else 1),
        grid=(n_heads,),
        in_specs=in_specs,
        out_specs=[pl.BlockSpec((HEAD_DIM, t_all), lambda h: (h, 0)), state_spec],
        out_shape=[jax.ShapeDtypeStruct((width, t_all), F32),
                   jax.ShapeDtypeStruct((depth, n_heads, HEAD_DIM, HEAD_DIM, nbatch), F32)],
        scratch_shapes=[pltpu.VMEM((HEAD_DIM, t_all), F32)] * 3,
        input_output_aliases=aliases,
        compiler_params=_cparams(("arbitrary",)),
        name="lane_" + kind,
    )(*args)


def _lane_delta_kernel(*refs, n_tok, nbatch, n_prev):
    (qp_ref, kp_ref, vp_ref, gate_ref, misc_ref, wq_ref, wk_ref, wv_ref, cq_ref, ck_ref, cv_ref,
     alog_ref, dtb_ref, nw_ref, s0_ref) = refs[:15]
    o_ref, sout_ref, cqo_ref, cko_ref, cvo_ref, q_scr, k_scr = refs[15 + n_prev:]
    h = pl.program_id(0)

    def conv(x_ref, w_ref, c_ref, cout_ref):
        xs = [c_ref[j] for j in range(CONV_W - 1)] + [x_ref[:, t * nbatch:(t + 1) * nbatch] for t in range(n_tok)]
        outs = []
        for t in range(n_tok):
            acc = xs[t] * w_ref[0]
            for j in range(1, CONV_W):
                acc = acc + xs[t + j] * w_ref[j]
            outs.append(_silu(acc))
        for j in range(CONV_W - 1):
            cout_ref[j] = xs[n_tok + j]
        return jnp.concatenate(outs, axis=1)

    q = conv(qp_ref, wq_ref, cq_ref, cqo_ref)
    k = conv(kp_ref, wk_ref, ck_ref, cko_ref)
    v = conv(vp_ref, wv_ref, cv_ref, cvo_ref)
    q_scr[...] = q * lax.rsqrt(jnp.sum(q * q, axis=0, keepdims=True) + EPS) * HEAD_DIM ** -0.5
    k_scr[...] = k * lax.rsqrt(jnp.sum(k * k, axis=0, keepdims=True) + EPS)
    misc = misc_ref[0:2 * SUBLANES, :]
    beta = jax.nn.sigmoid(_pick_row(misc, h))
    glog = -jnp.exp(alog_ref[...]) * _softplus(_pick_row(misc, DN_A_LANE + h) + dtb_ref[...])
    a_all = jnp.exp(glog)
    nw = nw_ref[...]
    zero = jnp.zeros((HEAD_DIM, nbatch), F32)
    for t in range(n_tok):
        lanes = slice(t * nbatch, (t + 1) * nbatch)
        src = s0_ref if t == 0 else sout_ref
        a_t = a_all[:, lanes]

        def kts(g, r, src=src, lanes=lanes):
            kg = _row_group(k_scr, g, lanes)
            for i in range(SUBLANES):
                r = r + kg[i:i + 1] * src[g * SUBLANES + i]
            return r

        r = lax.fori_loop(0, HEAD_DIM // SUBLANES, kts, zero)
        w = beta[:, lanes] * (v[:, lanes] - a_t * r)

        def body(g, o, src=src, lanes=lanes, a_t=a_t, w=w):
            kg = _row_group(k_scr, g, lanes)
            qg = _row_group(q_scr, g, lanes)
            for i in range(SUBLANES):
                d = g * SUBLANES + i
                s_d = src[d] * a_t + kg[i:i + 1] * w
                sout_ref[d] = s_d
                o = o + qg[i:i + 1] * s_d
            return o

        o = lax.fori_loop(0, HEAD_DIM // SUBLANES, body, zero)
        o = o * lax.rsqrt(jnp.mean(o * o, axis=0, keepdims=True) + EPS) * nw
        o_ref[:, lanes] = o * _silu(gate_ref[:, lanes])


def lane_delta_net(proj_t_all, col0, conv_w_b, conv_parts, alog_b, dtb_b, nw, s_all, layer, depth, prev, n_tok, nbatch):
    t_all = n_tok * nbatch
    hr = lambda off: _head_rows(t_all, col0 + off)
    state_spec = pl.BlockSpec((None, None, HEAD_DIM, HEAD_DIM, nbatch), lambda h: (layer, h, 0, 0, 0))
    tap_spec = lambda i: pl.BlockSpec((None, CONV_W, HEAD_DIM, nbatch), lambda h, i=i: (i, 0, h, 0))
    cst_spec = pl.BlockSpec((None, CONV_W - 1, HEAD_DIM, nbatch), lambda h: (layer, 0, h, 0))
    head_row = pl.BlockSpec((None, 1, 1), lambda h: (h, 0, 0))
    in_specs = [hr(0), hr(W_DN), hr(2 * W_DN), hr(DN_CONV_CH),
                pl.BlockSpec((LANES, t_all), lambda h: ((col0 + DN_CONV_CH + W_DN) // LANES, 0)),
                tap_spec(0), tap_spec(1), tap_spec(2), cst_spec, cst_spec, cst_spec,
                head_row, head_row, pl.BlockSpec((HEAD_DIM, nbatch), lambda h: (0, 0)), state_spec]
    args = [proj_t_all] * 5 + [conv_w_b] * 3 + list(conv_parts) + [alog_b, dtb_b, nw, s_all]
    aliases = {}
    if prev is not None:
        aliases = {len(args) + i: 1 + i for i in range(4)}
        in_specs += [_ANY_SPEC] * 4
        args += list(prev)
    cshape = jax.ShapeDtypeStruct((depth, CONV_W - 1, W_DN, nbatch), F32)
    return pl.pallas_call(
        functools.partial(_lane_delta_kernel, n_tok=n_tok, nbatch=nbatch, n_prev=0 if prev is None else 4),
        grid=(H_DN,),
        in_specs=in_specs,
        out_specs=[pl.BlockSpec((HEAD_DIM, t_all), lambda h: (h, 0)), state_spec, cst_spec, cst_spec, cst_spec],
        out_shape=[jax.ShapeDtypeStruct((W_DN, t_all), F32),
                   jax.ShapeDtypeStruct((depth, H_DN, HEAD_DIM, HEAD_DIM, nbatch), F32), cshape, cshape, cshape],
        scratch_shapes=[pltpu.VMEM((HEAD_DIM, t_all), F32)] * 2,
        input_output_aliases=aliases,
        compiler_params=_cparams(("arbitrary",)),
        name="lane_delta",
    )(*args)


def _prep_layer_weights(li, w_in_t, w_out, w_mq, w_mo, w_ff1, w_ff2):
    wi = w_in_t[li]
    d = wi.shape[1]
    c0 = RET_COLS
    c1 = c0 + DN_CONV_CH + W_DN + 2 * H_DN
    w_ret = wi[:c0]
    w_dn = jnp.concatenate([wi[c0:c1], jnp.zeros((DN_COLS - (c1 - c0), d), F32)], axis=0)
    w_gla = jnp.concatenate([wi[c1:], jnp.zeros((GLA_COLS - (wi.shape[0] - c1), d), F32)], axis=0)
    cast = lambda a: a.astype(BF16)
    return dict(w_ret=w_ret, w_dn=w_dn, w_gla=w_gla, w_all_t=jnp.concatenate([w_ret, w_dn, w_gla], axis=0),
                w_out=cast(w_out[li]),
                w_mq=w_mq[li], w_mo=cast(w_mo[li]), w_ff1=cast(w_ff1[li]), w_ff2=cast(w_ff2[li]))


def _run_trunk(x, pos, s_ret, s_dn, s_conv, s_gla, mem_k, mem_v, mem_split, lw, p, cfg):
    b, l, d = x.shape
    t = b * l
    depth = len(lw)
    xf = x.reshape(t, d)
    lane_batch = cfg['lane_batch']
    act = cfg['act_dtype']
    if lane_batch:
        lane = _lane_setup(pos, b, s_ret, s_dn, s_conv, s_gla)
    else:
        ret_tables = _ret_consts(cfg['ret_c']) + _rope_tables(pos)
    r_s = d_s = g_s = None
    for li in range(depth):
        w = lw[li]
        g = p['norm_mix'][li]
        if lane_batch:
            o_mix, r_s, d_s, g_s = _lane_mixers(xf, b, l, li, depth, w, p, lane, r_s, d_s, g_s)
            xf = matmul_res([o_mix], w['w_out'], xf, cfg['tm'])
        else:
            pr = rms_matmul(xf, g, w['w_ret'], cfg['tm'], w_transposed=True).reshape(b, l, RET_COLS)
            pd = rms_matmul(xf, g, w['w_dn'], cfg['tm'], w_transposed=True).reshape(b, l, DN_COLS)
            pg = rms_matmul(xf, g, w['w_gla'], cfg['tm'], w_transposed=True).reshape(b, l, GLA_COLS)
            o_ret, r_s = retention(pr, ret_tables, s_ret, li, depth, r_s, p['ret_norm_w'][li], cfg['ret_c'],
                                   cfg['nb'], act)
            o_dn, *d_s = delta_net(pd, p['dn_conv_w'][li], s_conv, p['dn_a_log'][li], p['dn_dt_bias'][li], s_dn,
                                   li, depth, d_s, p['dn_norm_w'][li], cfg['dn_tt'], cfg['dn_c'], cfg['nb'], act)
            o_gla, g_s = gla(pg, p['gla_w_a2'][li], p['gla_b_a'][li], s_gla, li, depth, g_s, p['gla_norm_w'][li],
                             cfg['gla_tt'], cfg['gla_m'], cfg['gla_cs'], cfg['nb'], act)
            xf = matmul_res([o_ret.reshape(t, W_RET), o_dn.reshape(t, W_DN), o_gla.reshape(t, W_GLA)],
                            w['w_out'], xf, cfg['tm'])
        qm = rms_matmul(xf, p['norm_cross'][li], w['w_mq'], cfg['tm'], act).reshape(b, l, d)
        if mem_split:
            att = mem_attention(qm, mem_k, mem_v, cfg['tq'], cfg['attn_nb'], layer=li)
        else:
            att = mem_attention(qm, mem_k[li], mem_v[li], cfg['tq'], cfg['attn_nb'])
        xf = matmul_res([att.reshape(t, d)], w['w_mo'], xf, cfg['tm'])
        xf = ffn(xf, p['norm_ffn'][li], w['w_ff1'], w['w_ff2'], p['norm_final'], li == depth - 1,
                 cfg['ffn_tm'], cfg['ffn_tf'])
    if lane_batch:
        to_batch_major = lambda s: jnp.transpose(s, (0, 4, 1, 2, 3))
        conv = jnp.transpose(jnp.concatenate(d_s[1:], axis=2), (0, 3, 1, 2))
        return (xf.reshape(b, l, d), to_batch_major(r_s), to_batch_major(d_s[0]), conv, to_batch_major(g_s))
    return (xf.reshape(b, l, d), r_s, d_s[0], d_s[1], g_s)


def _lane_setup(pos, b, s_ret, s_dn, s_conv, s_gla):
    half = HEAD_DIM // 2
    inv_freq = ROPE_BASE ** (-jnp.arange(half, dtype=F32) / half)
    ang = pos.astype(F32)[:, None] * inv_freq[None, :]
    cos, sin = jnp.cos(ang), jnp.sin(ang)
    spread = lambda a: jnp.repeat(a.T, b, axis=1)
    lg = jnp.log1p(-jnp.exp2(-5.0 - jnp.arange(H_RET, dtype=F32)))
    to_lane = lambda s: jnp.transpose(s, (0, 2, 3, 4, 1))
    conv_t = jnp.transpose(s_conv, (0, 2, 3, 1))
    return dict(cos=spread(jnp.concatenate([cos, cos], axis=1)), sin=spread(jnp.concatenate([-sin, sin], axis=1)),
                gam=jnp.broadcast_to(jnp.exp(lg)[:, None, None], (H_RET, 1, b)),
                s_ret=to_lane(s_ret), s_dn=to_lane(s_dn), s_gla=to_lane(s_gla),
                conv=[conv_t[:, :, i * W_DN:(i + 1) * W_DN, :] for i in range(3)])


def _lane_mixers(xf, b, l, li, depth, w, p, lane, r_s, d_s, g_s):
    d = xf.shape[1]
    col = lambda v: jnp.broadcast_to(v[:, None], (v.shape[0], b))
    x_t = xf.reshape(b, l, d).transpose(2, 1, 0).reshape(d, l * b)
    pt = proj_t(x_t, p['norm_mix'][li], w['w_all_t'], 256)
    o_ret, r_s = lane_linear_mixer(pt, 0, W_RET, H_RET, 'ret',
                                   (lane['cos'], lane['sin'], lane['gam'], col(p['ret_norm_w'][li])),
                                   lane['s_ret'], li, depth, r_s, l, b)
    taps = p['dn_conv_w'][li].reshape(CONV_W, 3, W_DN).transpose(1, 0, 2)
    taps = jnp.broadcast_to(taps[..., None], taps.shape + (b,))
    o_dn, *d_s = lane_delta_net(pt, RET_COLS, taps, lane['conv'], p['dn_a_log'][li].reshape(H_DN, 1, 1),
                                p['dn_dt_bias'][li].reshape(H_DN, 1, 1), col(p['dn_norm_w'][li]),
                                lane['s_dn'], li, depth, d_s, l, b)
    wa2_t = jnp.zeros((W_GLA, LANES), F32).at[:, :GLA_RANK].set(p['gla_w_a2'][li].T).astype(BF16)
    o_gla, g_s = lane_linear_mixer(pt, RET_COLS + DN_COLS, W_GLA, H_GLA, 'gla',
                                   (wa2_t, p['gla_b_a'][li].reshape(W_GLA, 1), col(p['gla_norm_w'][li])),
                                   lane['s_gla'], li, depth, g_s, l, b)
    o_t = jnp.concatenate([o_ret, o_dn, o_gla], axis=0)
    o = o_t.reshape(o_t.shape[0], l, b).transpose(2, 1, 0).reshape(b * l, o_t.shape[0])
    return o, r_s, d_s, g_s


def kernel(x_prompt, x_sample, state_ret, state_dn, state_dn_conv, state_gla, cache_mem_k, cache_mem_v, mem_prompt, norm_mix, w_in, ret_norm_w, dn_conv_w, dn_a_log, dn_dt_bias, dn_norm_w, gla_w_a2, gla_b_a, gla_norm_w, w_out, norm_cross, norm_mem, w_mq, w_mk, w_mv, w_mo, norm_ffn, w_ff1, w_ff2, norm_final):
    b_p, l_p, d = x_prompt.shape
    b_s, l_s, _ = x_sample.shape
    depth = w_in.shape[0]
    n_mem = mem_prompt.shape[1]
    p = dict(norm_mix=norm_mix, ret_norm_w=ret_norm_w, dn_conv_w=dn_conv_w, dn_a_log=dn_a_log,
             dn_dt_bias=dn_dt_bias, dn_norm_w=dn_norm_w, gla_w_a2=gla_w_a2, gla_b_a=gla_b_a,
             gla_norm_w=gla_norm_w, norm_cross=norm_cross, norm_ffn=norm_ffn, norm_final=norm_final)
    w_in_t = jnp.transpose(w_in, (0, 2, 1))
    lw = [_prep_layer_weights(li, w_in_t, w_out, w_mq, w_mo, w_ff1, w_ff2) for li in range(depth)]

    memf = mem_prompt.reshape(b_p * n_mem, d)
    mk = [rms_matmul(memf, norm_mem[li], w_mk[li], 512).reshape(b_p, n_mem, d) for li in range(depth)]
    mv = [rms_matmul(memf, norm_mem[li], w_mv[li], 512).reshape(b_p, n_mem, d) for li in range(depth)]

    cfg_p = dict(tm=512, ret_c=256, dn_tt=256, dn_c=math.gcd(l_p, DN_CHUNK), gla_tt=256, gla_m=64,
                 gla_cs=math.gcd(l_p, GLA_CHUNK), tq=512, attn_nb=1, ffn_tm=1024, ffn_tf=1024, nb=1, act_dtype=BF16,
                 lane_batch=False)
    y_p, p_ret, p_dn, p_conv, p_gla = _run_trunk(
        x_prompt, jnp.arange(l_p, dtype=jnp.int32), None, None, None, None, mk, mv, False, lw, p, cfg_p)

    cfg_s = dict(tm=512, ret_c=l_s, dn_tt=l_s, dn_c=math.gcd(l_s, DN_CHUNK), gla_tt=l_s,
                 gla_m=math.gcd(l_s, GLA_CHUNK), gla_cs=math.gcd(l_s, GLA_CHUNK), tq=l_s, ffn_tm=1024, ffn_tf=512,
                 attn_nb=2, nb=8, act_dtype=F32, lane_batch=True)
    y_s, s_ret, s_dn, s_conv, s_gla = _run_trunk(
        x_sample, PAST_LEN + jnp.arange(l_s, dtype=jnp.int32), state_ret, state_dn, state_dn_conv, state_gla,
        cache_row_view(cache_mem_k), cache_row_view(cache_mem_v), True, lw, p, cfg_s)

    mem_shape = (depth, b_p, n_mem, MEM_HEADS, d // MEM_HEADS)
    return (y_p, y_s, p_ret, p_dn, p_conv, p_gla, jnp.stack(mk).reshape(mem_shape), jnp.stack(mv).reshape(mem_shape),
            s_ret, s_dn, s_conv, s_gla)
```

```python
import functools
import math

import jax
import jax.numpy as jnp
from jax import lax
from jax.experimental import pallas as pl
from jax.experimental.pallas import tpu as pltpu

F32 = jnp.float32
BF16 = jnp.bfloat16

LANES = 128
HEAD_DIM = 64
PAIR = 2 * HEAD_DIM
H_RET, H_DN, H_GLA = 6, 6, 4
W_RET, W_DN, W_GLA = H_RET * HEAD_DIM, H_DN * HEAD_DIM, H_GLA * HEAD_DIM
CONV_W = 4
DN_CONV_CH = 3 * W_DN
GLA_RANK = 16
GLA_TAU = 16.0
MEM_HEADS = 4
DN_CHUNK = 64
GLA_CHUNK = 16
ROPE_BASE = 10000.0
EPS = 1e-6
PAST_LEN = 16384

DN_COLS = DN_CONV_CH + W_DN + PAIR
GLA_COLS = 4 * W_GLA + PAIR
RET_COLS = 4 * W_RET
DN_A_LANE = H_DN
TRI_BASE = 16

VMEM_LIMIT = 56 * 1024 * 1024


def _cparams(sem):
    return pltpu.CompilerParams(dimension_semantics=sem, vmem_limit_bytes=VMEM_LIMIT)


def _log2(n):
    l = int(math.log2(n))
    assert (1 << l) == n, n
    return l


def _dot(a, b):
    return jnp.dot(a, b, preferred_element_type=F32)


def _dot_nt(a, b):
    return lax.dot_general(a, b, (((1,), (1,)), ((), ())), preferred_element_type=F32)


def _dot_tn(a, b):
    return lax.dot_general(a, b, (((0,), (0,)), ((), ())), preferred_element_type=F32)


def _split2(a):
    hi = a.astype(BF16)
    lo = (a - hi.astype(F32)).astype(BF16)
    return hi, lo


def _split3(a):
    hi = a.astype(BF16)
    r = a - hi.astype(F32)
    mid = r.astype(BF16)
    lo = (r - mid.astype(F32)).astype(BF16)
    return hi, mid, lo


def _dot_exact_lhs(a_bf16, b):
    hi, mid, lo = _split3(b)
    return _dot(a_bf16, hi) + _dot(a_bf16, mid) + _dot(a_bf16, lo)


def _dot_nt_exact(a_bf16, b):
    hi, mid, lo = _split3(b)
    return _dot_nt(a_bf16, hi) + _dot_nt(a_bf16, mid) + _dot_nt(a_bf16, lo)


def _silu(x):
    return x * jax.nn.sigmoid(x)


def _softplus(x):
    return jnp.maximum(x, 0.0) + jnp.log1p(jnp.exp(-jnp.abs(x)))


def _lane_iota(shape):
    return lax.broadcasted_iota(jnp.int32, shape, len(shape) - 1)


def _row_iota(shape):
    return lax.broadcasted_iota(jnp.int32, shape, 0)


def _head0_lanes():
    return _lane_iota((1, PAIR)) < HEAD_DIM


def _stack(x, lo):
    return jnp.concatenate([jnp.where(lo, x, 0.0), jnp.where(lo, 0.0, x)], axis=0)


def _block_diag_mask():
    r = _row_iota((PAIR, PAIR))
    c = _lane_iota((PAIR, PAIR))
    return (r >> 6) == (c >> 6)


def _group_mean_matrix(bm):
    return jnp.where(bm, 1.0 / HEAD_DIM, 0.0).astype(BF16)


def _load_pair_state(S_ref, s0_ref, n_pairs):
    S_ref[...] = jnp.zeros_like(S_ref)
    if s0_ref is not None:
        for p in range(n_pairs):
            S_ref[p, 0:HEAD_DIM, 0:HEAD_DIM] = s0_ref[0, 2 * p]
            S_ref[p, HEAD_DIM:PAIR, HEAD_DIM:PAIR] = s0_ref[0, 2 * p + 1]


def _store_pair_state(sout_ref, S_ref, n_pairs):
    for p in range(n_pairs):
        sout_ref[0, 2 * p] = S_ref[p, 0:HEAD_DIM, 0:HEAD_DIM]
        sout_ref[0, 2 * p + 1] = S_ref[p, HEAD_DIM:PAIR, HEAD_DIM:PAIR]


def _split_refs(refs, n_in, has_state, n_prev):
    ins = refs[:n_in]
    refs = refs[n_in:]
    return ins, refs[n_prev:]


def _rms_matmul_kernel(x_ref, g_ref, w_ref, o_ref, wb_ref, *, w_transposed):
    @pl.when(pl.program_id(0) == 0)
    def _():
        wb_ref[...] = w_ref[...].astype(BF16)

    x = x_ref[...]
    xn = (x * lax.rsqrt(jnp.mean(x * x, axis=-1, keepdims=True) + EPS) * g_ref[...]).astype(BF16)
    y = _dot_nt(xn, wb_ref[...]) if w_transposed else _dot(xn, wb_ref[...])
    o_ref[...] = y.astype(o_ref.dtype)


def rms_matmul(x, g, w, tm, out_dtype=F32, w_transposed=False):
    t, d = x.shape
    n = w.shape[0] if w_transposed else w.shape[1]
    return pl.pallas_call(
        functools.partial(_rms_matmul_kernel, w_transposed=w_transposed),
        grid=(t // tm,),
        in_specs=[pl.BlockSpec((tm, d), lambda i: (i, 0)),
                  pl.BlockSpec((1, d), lambda i: (0, 0)),
                  pl.BlockSpec(w.shape, lambda i: (0, 0))],
        out_specs=pl.BlockSpec((tm, n), lambda i: (i, 0)),
        out_shape=jax.ShapeDtypeStruct((t, n), out_dtype),
        scratch_shapes=[pltpu.VMEM(w.shape, BF16)],
        compiler_params=_cparams(("arbitrary",)),
        name="rms_matmul",
    )(x, g.reshape(1, d), w)


def _matmul_res_kernel(*refs, widths):
    n_in = len(widths)
    a_refs = refs[:n_in]
    w_ref, r_ref, o_ref = refs[n_in:]
    acc = r_ref[...]
    off = 0
    for a_ref, wk in zip(a_refs, widths):
        acc = acc + _dot(a_ref[...].astype(BF16), w_ref[off:off + wk, :])
        off += wk
    o_ref[...] = acc


def matmul_res(a_list, w_bf16, res, tm):
    t, n = res.shape
    widths = tuple(a.shape[1] for a in a_list)
    k = w_bf16.shape[0]
    assert sum(widths) == k
    in_specs = [pl.BlockSpec((tm, wk), lambda i: (i, 0)) for wk in widths]
    in_specs += [pl.BlockSpec((k, n), lambda i: (0, 0)), pl.BlockSpec((tm, n), lambda i: (i, 0))]
    return pl.pallas_call(
        functools.partial(_matmul_res_kernel, widths=widths),
        grid=(t // tm,),
        in_specs=in_specs,
        out_specs=pl.BlockSpec((tm, n), lambda i: (i, 0)),
        out_shape=jax.ShapeDtypeStruct((t, n), F32),
        compiler_params=_cparams(("parallel",)),
        name="matmul_res",
    )(*a_list, w_bf16, res)


def _ffn_kernel(x_ref, g_ref, w1_ref, w2_ref, gf_ref, o_ref, xn_ref, acc_ref, *, final_norm):
    j = pl.program_id(1)

    @pl.when(j == 0)
    def _():
        x = x_ref[...]
        xn = x * lax.rsqrt(jnp.mean(x * x, axis=-1, keepdims=True) + EPS) * g_ref[...]
        xn_ref[...] = xn.astype(BF16)
        acc_ref[...] = x

    h = _dot(xn_ref[...], w1_ref[...])
    h = jnp.square(jnp.maximum(h, 0.0))
    acc_ref[...] += _dot(h.astype(BF16), w2_ref[...])

    @pl.when(j == pl.num_programs(1) - 1)
    def _():
        y = acc_ref[...]
        if final_norm:
            y = y * lax.rsqrt(jnp.mean(y * y, axis=-1, keepdims=True) + EPS) * gf_ref[...]
        o_ref[...] = y


def ffn(x, g, w1_bf16, w2_bf16, g_final, final_norm, tm, tf):
    t, d = x.shape
    f = w1_bf16.shape[1]
    return pl.pallas_call(
        functools.partial(_ffn_kernel, final_norm=final_norm),
        grid=(t // tm, f // tf),
        in_specs=[pl.BlockSpec((tm, d), lambda i, j: (i, 0)),
                  pl.BlockSpec((1, d), lambda i, j: (0, 0)),
                  pl.BlockSpec((d, tf), lambda i, j: (0, j)),
                  pl.BlockSpec((tf, d), lambda i, j: (j, 0)),
                  pl.BlockSpec((1, d), lambda i, j: (0, 0))],
        out_specs=pl.BlockSpec((tm, d), lambda i, j: (i, 0)),
        out_shape=jax.ShapeDtypeStruct((t, d), F32),
        scratch_shapes=[pltpu.VMEM((tm, d), BF16), pltpu.VMEM((tm, d), F32)],
        compiler_params=_cparams(("parallel", "arbitrary")),
        name="ffn",
    )(x, g.reshape(1, d), w1_bf16, w2_bf16, g_final.reshape(1, d))


def _attn_kernel(q_ref, k_ref, v_ref, o_ref, *, n_heads, dh, row_view, nb):
    scale = dh ** -0.5
    lane_tiles = dh // LANES
    rows_per_token = n_heads * lane_tiles

    def head_slab(ref, i, h):
        if not row_view:
            return ref[i, :, h * dh:(h + 1) * dh].astype(BF16)
        n_mem = ref.shape[1] // rows_per_token
        parts = [ref[i, pl.ds(j * n_heads + h, n_mem, stride=rows_per_token), :] for j in range(lane_tiles)]
        return jnp.concatenate(parts, axis=1).astype(BF16)

    for i in range(nb):
        for h in range(n_heads):
            sl = slice(h * dh, (h + 1) * dh)
            q = q_ref[i, :, sl].astype(BF16)
            s = _dot_nt(q, head_slab(k_ref, i, h)) * scale
            m = jnp.max(s, axis=-1, keepdims=True)
            e = jnp.exp(s - m)
            l = jnp.sum(e, axis=-1, keepdims=True)
            o_ref[i, :, sl] = (_dot(e.astype(BF16), head_slab(v_ref, i, h)) / l).astype(o_ref.dtype)


def mem_attention(q, mem_k, mem_v, tq, nb, layer=None):
    b, l, d = q.shape
    dh = d // MEM_HEADS
    if layer is None:
        m = mem_k.shape[1]
        mem_spec = pl.BlockSpec((nb, m, d), lambda i, j: (i, 0, 0))
    else:
        rows = mem_k.shape[1]
        mem_spec = pl.BlockSpec((nb, rows, LANES), lambda i, j: (layer * (b // nb) + i, 0, 0))
    return pl.pallas_call(
        functools.partial(_attn_kernel, n_heads=MEM_HEADS, dh=dh, row_view=layer is not None, nb=nb),
        grid=(b // nb, l // tq),
        in_specs=[pl.BlockSpec((nb, tq, d), lambda i, j: (i, j, 0)), mem_spec, mem_spec],
        out_specs=pl.BlockSpec((nb, tq, d), lambda i, j: (i, j, 0)),
        out_shape=jax.ShapeDtypeStruct((b, l, d), q.dtype),
        compiler_params=_cparams(("parallel", "arbitrary")),
        name="mem_attention",
    )(q, mem_k, mem_v)


def cache_row_view(cache):
    depth, b, m, nh, dh = cache.shape
    c = cache.reshape(depth, b, m, nh, dh // LANES, LANES).transpose(0, 1, 2, 4, 3, 5)
    return c.reshape(depth * b, m * nh * (dh // LANES), LANES)


def _state_specs(shape_tail, layer, nb):
    nd = len(shape_tail)
    return pl.BlockSpec((None, nb) + tuple(shape_tail), lambda i, j: (layer, i) + (0,) * nd)


_ANY_SPEC = pl.BlockSpec(memory_space=pl.ANY)


def _batch_view(ref, i):
    return None if ref is None else ref.at[pl.ds(i, 1)]


def _ret_kernel(*refs, C, n_pairs, has_state, n_prev, nb):
    n_in = 8 + (1 if has_state else 0)
    (proj_ref, cos_ref, sin_ref, dm_ref, xi_ref, zeta_ref, gc_ref, lnw_ref, *st), rest = \
        _split_refs(refs, n_in, has_state, n_prev)
    s0_ref = st[0] if has_state else None
    o_ref, sout_ref, S_ref = rest
    t = pl.program_id(1)

    @pl.when(t == 0)
    def _():
        for i in range(nb):
            _load_pair_state(S_ref.at[i], _batch_view(s0_ref, i), n_pairs)

    for i in range(nb):
        _ret_body(_batch_view(proj_ref, i), cos_ref, sin_ref, dm_ref, xi_ref, zeta_ref, gc_ref, lnw_ref,
                  _batch_view(o_ref, i), S_ref.at[i], C=C, n_pairs=n_pairs)

    @pl.when(t == pl.num_programs(1) - 1)
    def _():
        for i in range(nb):
            _store_pair_state(_batch_view(sout_ref, i), S_ref.at[i], n_pairs)


def _ret_body(proj_ref, cos_ref, sin_ref, dm_ref, xi_ref, zeta_ref, gc_ref, lnw_ref, o_ref, S_ref, *, C, n_pairs):
    m0 = _head0_lanes()
    first_half = (_lane_iota((1, PAIR)) & (HEAD_DIM - 1)) < HEAD_DIM // 2
    bm = _block_diag_mask()
    pmean = _group_mean_matrix(bm)
    cos = cos_ref[...]
    sin = sin_ref[...]
    lnw = lnw_ref[...]

    def rot(x):
        sw = jnp.where(first_half, pltpu.roll(x, PAIR - HEAD_DIM // 2, 1), pltpu.roll(x, HEAD_DIM // 2, 1))
        return x * cos + sw * sin

    pairs = range(n_pairs)
    q = [rot(proj_ref[0, :, p * PAIR:(p + 1) * PAIR]) for p in pairs]
    k = [rot(proj_ref[0, :, W_RET + p * PAIR:W_RET + (p + 1) * PAIR]) * HEAD_DIM ** -0.5 for p in pairs]
    vb = [proj_ref[0, :, 2 * W_RET + p * PAIR:2 * W_RET + (p + 1) * PAIR].astype(BF16) for p in pairs]
    sc = [_dot_nt(_stack(q[p], m0).astype(BF16), k[p].astype(BF16)) * dm_ref[p] for p in pairs]
    S = [S_ref[p] for p in pairs]
    o_inter = [_dot(q[p].astype(BF16), S[p].astype(BF16)) * xi_ref[p] for p in pairs]
    kv = [_dot_tn((k[p] * zeta_ref[p]).astype(BF16), vb[p]) for p in pairs]
    o2 = [_dot(sc[p].astype(BF16), vb[p]) for p in pairs]
    o = [jnp.where(m0, o2[p][:C], o2[p][C:]) + o_inter[p] for p in pairs]
    oc = [o[p] - _dot(o[p].astype(BF16), pmean) for p in pairs]
    var = [_dot((oc[p] * oc[p]).astype(BF16), pmean) for p in pairs]
    for p in pairs:
        S_ref[p] = S[p] * gc_ref[p][0:1, :] + jnp.where(bm, kv[p], 0.0)
        gate = proj_ref[0, :, 3 * W_RET + p * PAIR:3 * W_RET + (p + 1) * PAIR]
        o_ref[0, :, p * PAIR:(p + 1) * PAIR] = (oc[p] * lax.rsqrt(var[p] + EPS) * lnw * _silu(gate)).astype(o_ref.dtype)


def _ret_consts(C):
    lg = jnp.log1p(-jnp.exp2(-5.0 - jnp.arange(H_RET, dtype=F32)))
    idx = jnp.arange(C, dtype=F32)
    diff = idx[:, None] - idx[None, :]
    dmask = jnp.where(diff >= 0, jnp.exp(lg[:, None, None] * jnp.maximum(diff, 0.0)), 0.0)
    dm = dmask.reshape(H_RET // 2, 2 * C, C)
    xi = jnp.exp(lg[:, None] * (idx + 1.0))
    zeta = jnp.exp(lg[:, None] * (C - 1.0 - idx))
    gch = jnp.exp(lg * C)

    def pair_lanes(a):
        r = a.shape[1]
        a = jnp.repeat(a[:, :, None], HEAD_DIM, axis=2)
        a = a.reshape(H_RET // 2, 2, r, HEAD_DIM).transpose(0, 2, 1, 3)
        return a.reshape(H_RET // 2, r, PAIR)

    gc = pair_lanes(jnp.broadcast_to(gch[:, None], (H_RET, 8)))
    return dm, pair_lanes(xi), pair_lanes(zeta), gc


def _rope_tables(pos):
    half = HEAD_DIM // 2
    inv_freq = ROPE_BASE ** (-jnp.arange(half, dtype=F32) / half)
    ang = pos.astype(F32)[:, None] * inv_freq[None, :]
    cos, sin = jnp.cos(ang), jnp.sin(ang)
    cos_t = jnp.concatenate([cos, cos, cos, cos], axis=-1)
    sin_t = jnp.concatenate([-sin, sin, -sin, sin], axis=-1)
    return cos_t, sin_t


def retention(proj, tables, s0_all, layer, depth, prev, ln_w, C, nb, act_dtype):
    b, l, _ = proj.shape
    n_pairs = H_RET // 2
    dm, xi, zeta, gc, cos_t, sin_t = tables
    lnw = jnp.concatenate([ln_w, ln_w]).reshape(1, PAIR)
    const3 = lambda shape: pl.BlockSpec(shape, lambda i, j: (0, 0, 0))
    st_tail = (H_RET, HEAD_DIM, HEAD_DIM)
    in_specs = [pl.BlockSpec((nb, C, RET_COLS), lambda i, j: (i, j, 0)),
                pl.BlockSpec((C, PAIR), lambda i, j: (j, 0)),
                pl.BlockSpec((C, PAIR), lambda i, j: (j, 0)),
                const3((n_pairs, 2 * C, C)),
                const3((n_pairs, C, PAIR)),
                const3((n_pairs, C, PAIR)),
                const3((n_pairs, 8, PAIR)),
                pl.BlockSpec((1, PAIR), lambda i, j: (0, 0))]
    args = [proj, cos_t, sin_t, dm, xi, zeta, gc, lnw]
    if s0_all is not None:
        in_specs.append(_state_specs(st_tail, layer, nb))
        args.append(s0_all)
    aliases = {}
    if prev is not None:
        aliases[len(args)] = 1
        in_specs.append(_ANY_SPEC)
        args.append(prev)
    return pl.pallas_call(
        functools.partial(_ret_kernel, C=C, n_pairs=n_pairs, has_state=s0_all is not None,
                          n_prev=0 if prev is None else 1, nb=nb),
        grid=(b // nb, l // C),
        in_specs=in_specs,
        out_specs=[pl.BlockSpec((nb, C, W_RET), lambda i, j: (i, j, 0)), _state_specs(st_tail, layer, nb)],
        out_shape=[jax.ShapeDtypeStruct((b, l, W_RET), act_dtype),
                   jax.ShapeDtypeStruct((depth, b) + st_tail, F32)],
        scratch_shapes=[pltpu.VMEM((nb, n_pairs, PAIR, PAIR), F32)],
        input_output_aliases=aliases,
        compiler_params=_cparams(("parallel", "arbitrary")),
        name="retention",
    )(*args)


def _tri_inverse_many(a_list, c, lo, ii, jj):
    def mm(x, y):
        return _dot(x.astype(BF16), _stack(y, lo).astype(BF16))

    base = min(c, TRI_BASE)
    lb = _log2(base)
    dblk = (ii >> lb) == (jj >> lb)
    eye = jnp.where(ii == jj, 1.0, 0.0)
    xs = [jnp.where(dblk, -a, 0.0) for a in a_list]
    ts = [eye + x for x in xs]
    for _ in range(lb - 1):
        xs = [mm(x, x) for x in xs]
        ts = [t + mm(t, x) for t, x in zip(ts, xs)]
    blk = base
    while blk < c:
        inner = (ii >> _log2(blk)) == (jj >> _log2(blk))
        outer = (ii >> _log2(2 * blk)) == (jj >> _log2(2 * blk))
        sel = outer & jnp.logical_not(inner)
        ls = [jnp.where(sel, a, 0.0) for a in a_list]
        ms = [mm(l, t) for l, t in zip(ls, ts)]
        ts = [t - mm(t, m) for t, m in zip(ts, ms)]
        blk *= 2
    return ts


def _dn_kernel(*refs, TT, c, n_pairs, has_state, n_prev, nb):
    n_in = 5 + (2 if has_state else 0)
    (proj_ref, convw_ref, alog_ref, dtb_ref, nw_ref, *st), rest = _split_refs(refs, n_in, has_state, n_prev)
    s0_ref, sconv_ref = st if has_state else (None, None)
    o_ref, sout_ref, convout_ref, S_ref, tail_ref = rest
    n2 = 2 * c
    ii = _row_iota((c, n2))
    lane2 = _lane_iota((c, n2))
    t = pl.program_id(1)

    @pl.when(t == 0)
    def _():
        tail_ref[...] = jnp.zeros_like(tail_ref)
        for i in range(nb):
            _load_pair_state(S_ref.at[i], _batch_view(s0_ref, i), n_pairs)
            if has_state:
                tail_ref[i, 8 - (CONV_W - 1):8, :] = sconv_ref[i]

    items = []
    for i in range(nb):
        items.append(_dn_prepare(_batch_view(proj_ref, i), convw_ref, alog_ref, dtb_ref, tail_ref.at[i], ii, lane2,
                                 TT=TT, c=c, n_pairs=n_pairs))
    flat = [it for batch_items in items for it in batch_items]
    tinvs = _tri_inverse_many([it['a'] for it in flat], c, lane2 < c, ii, lane2 & (c - 1))
    for it, tinv in zip(flat, tinvs):
        tb = tinv.astype(BF16)
        it['u'] = _dot(tb, it['vb'])
        it['w'] = _dot(tb, it['kbe']).astype(BF16)
    for i in range(nb):
        _dn_scan(items[i], _batch_view(proj_ref, i), nw_ref, _batch_view(o_ref, i), S_ref.at[i],
                 TT=TT, c=c, n_pairs=n_pairs)

    @pl.when(t == pl.num_programs(1) - 1)
    def _():
        for i in range(nb):
            _store_pair_state(_batch_view(sout_ref, i), S_ref.at[i], n_pairs)
            convout_ref[i] = tail_ref[i, 8 - (CONV_W - 1):8, :]


def _dn_prepare(proj_ref, convw_ref, alog_ref, dtb_ref, tail_ref, ii, lane2, *, TT, c, n_pairs):
    m0 = _head0_lanes()
    pmean = _group_mean_matrix(_block_diag_mask())
    nchunks = TT // c
    lc = _log2(c)

    x = proj_ref[0, :, 0:DN_CONV_CH]
    tail = tail_ref[...]
    rows8 = _row_iota((8, DN_CONV_CH))
    acc = x * convw_ref[CONV_W - 1:CONV_W, :]
    for s in range(1, CONV_W):
        xs = pltpu.roll(x, s, 0)
        top = jnp.where(rows8 < s, pltpu.roll(tail, s, 0), xs[0:8])
        xs = top if TT == 8 else jnp.concatenate([top, xs[8:]], axis=0)
        acc = acc + xs * convw_ref[CONV_W - 1 - s:CONV_W - s, :]
    tail_ref[...] = x[TT - 8:TT]
    qkv = _silu(acc)

    misc = proj_ref[0, :, DN_CONV_CH + W_DN:DN_COLS]
    sig = jax.nn.sigmoid(misc)
    glog = -jnp.exp(alog_ref[...]) * _softplus(misc + dtb_ref[...])
    rt = _row_iota((TT, TT))
    ct = _lane_iota((TT, TT))
    tril_t = jnp.where(((rt >> lc) == (ct >> lc)) & (ct <= rt), 1.0, 0.0).astype(BF16)
    cum_tok = _dot_exact_lhs(tril_t, glog)

    n2 = 2 * c
    jj = lane2 & (c - 1)
    lo2 = lane2 < c
    incl = jj <= ii
    strict = jj < ii
    ones_c = jnp.ones((c, PAIR), BF16)
    lane128 = _lane_iota((c, PAIR))

    items = []
    for p in range(n_pairs):
        h0, h1 = 2 * p, 2 * p + 1
        q_all = qkv[:, p * PAIR:(p + 1) * PAIR]
        k_all = qkv[:, W_DN + p * PAIR:W_DN + (p + 1) * PAIR]
        v_all = qkv[:, 2 * W_DN + p * PAIR:2 * W_DN + (p + 1) * PAIR]
        q_all = q_all * lax.rsqrt(_dot((q_all * q_all).astype(BF16), pmean) * HEAD_DIM + EPS) * HEAD_DIM ** -0.5
        k_all = k_all * lax.rsqrt(_dot((k_all * k_all).astype(BF16), pmean) * HEAD_DIM + EPS)
        for ci in range(nchunks):
            rows = slice(ci * c, (ci + 1) * c)
            q, k, v = q_all[rows], k_all[rows], v_all[rows]
            b0, b1 = sig[rows, h0:h0 + 1], sig[rows, h1:h1 + 1]
            c0 = cum_tok[rows, DN_A_LANE + h0:DN_A_LANE + h0 + 1]
            c1 = cum_tok[rows, DN_A_LANE + h1:DN_A_LANE + h1 + 1]
            beta = jnp.where(m0, jnp.broadcast_to(b0, (c, PAIR)), jnp.broadcast_to(b1, (c, PAIR)))
            cum_pair = jnp.where(m0, jnp.broadcast_to(c0, (c, PAIR)), jnp.broadcast_to(c1, (c, PAIR)))
            cum_col = cum_pair if n2 == PAIR else jnp.where(lo2, jnp.broadcast_to(c0, (c, n2)),
                                                            jnp.broadcast_to(c1, (c, n2)))
            ctok = cum_tok[rows]
            cum_sel = jnp.concatenate([jnp.where(lane128 ==DN_A_LANE + h0, ctok, 0.0),
                                       jnp.where(lane128 ==DN_A_LANE + h1, ctok, 0.0)], axis=0)
            kb = k * beta
            ecum = jnp.exp(cum_pair)
            last_row = cum_tok[(ci + 1) * c - 1:(ci + 1) * c, :]
            last = jnp.where(m0, jnp.broadcast_to(last_row[:, DN_A_LANE + h0:DN_A_LANE + h0 + 1], (1, PAIR)),
                             jnp.broadcast_to(last_row[:, DN_A_LANE + h1:DN_A_LANE + h1 + 1], (1, PAIR)))
            items.append(dict(p=p, ci=ci, cum_sel=cum_sel, cum_col=cum_col, k_st=_stack(k, m0).astype(BF16),
                              kb=kb.astype(BF16), q=q.astype(BF16),
                              vb=_stack(v * beta, m0).astype(BF16), kbe=_stack(kb * ecum, m0).astype(BF16),
                              q_dec=(q * ecum).astype(BF16), k_dec=(k * jnp.exp(last - cum_pair)).astype(BF16),
                              dec=jnp.exp(last)))
    for it in items:
        cum_row = _dot_nt_exact(ones_c, it.pop('cum_sel'))
        it['gam'] = jnp.where(incl, jnp.exp(jnp.minimum(it.pop('cum_col') - cum_row, 0.0)), 0.0)
    for it in items:
        k_st = it.pop('k_st')
        gam = it.pop('gam')
        it['a'] = jnp.where(strict, _dot_nt(it.pop('kb'), k_st) * gam, 0.0)
        it['attn'] = (_dot_nt(it.pop('q'), k_st) * gam).astype(BF16)
    return items


def _dn_scan(items, proj_ref, nw_ref, o_ref, S_ref, *, TT, c, n_pairs):
    m0 = _head0_lanes()
    bm = _block_diag_mask()
    pmean = _group_mean_matrix(bm)
    nw = nw_ref[...]
    nchunks = TT // c
    S = [S_ref[p] for p in range(n_pairs)]
    outs = [[] for _ in range(n_pairs)]
    for ci in range(nchunks):
        for p in range(n_pairs):
            it = items[p * nchunks + ci]
            Sb = S[p].astype(BF16)
            v_new = it['u'] - _dot(it['w'], Sb)
            o = _dot(it['q_dec'], Sb) + _dot(it['attn'], _stack(v_new, m0).astype(BF16))
            S[p] = S[p] * it['dec'] + jnp.where(bm, _dot_tn(it['k_dec'], v_new.astype(BF16)), 0.0)
            outs[p].append(o)
    for p in range(n_pairs):
        S_ref[p] = S[p]
        o = outs[p][0] if nchunks == 1 else jnp.concatenate(outs[p], axis=0)
        gate = proj_ref[0, :, DN_CONV_CH + p * PAIR:DN_CONV_CH + (p + 1) * PAIR]
        ms = _dot((o * o).astype(BF16), pmean)
        o_ref[0, :, p * PAIR:(p + 1) * PAIR] = (o * lax.rsqrt(ms + EPS) * nw * _silu(gate)).astype(o_ref.dtype)


def delta_net(proj, conv_w, s_conv_all, a_log, dt_bias, s0_all, layer, depth, prev, norm_w, TT, c, nb, act_dtype):
    b, l, _ = proj.shape
    n_pairs = H_DN // 2
    alog = jnp.zeros((1, PAIR), F32).at[0, DN_A_LANE:DN_A_LANE + H_DN].set(a_log)
    dtb = jnp.zeros((1, PAIR), F32).at[0, DN_A_LANE:DN_A_LANE + H_DN].set(dt_bias)
    nw = jnp.concatenate([norm_w, norm_w]).reshape(1, PAIR)
    row = lambda n: pl.BlockSpec((1, n), lambda i, j: (0, 0))
    st_tail = (H_DN, HEAD_DIM, HEAD_DIM)
    cv_tail = (CONV_W - 1, DN_CONV_CH)
    in_specs = [pl.BlockSpec((nb, TT, DN_COLS), lambda i, j: (i, j, 0)),
                pl.BlockSpec((CONV_W, DN_CONV_CH), lambda i, j: (0, 0)),
                row(PAIR), row(PAIR), row(PAIR)]
    args = [proj, conv_w, alog, dtb, nw]
    has_state = s0_all is not None
    if has_state:
        in_specs += [_state_specs(st_tail, layer, nb), _state_specs(cv_tail, layer, nb)]
        args += [s0_all, s_conv_all]
    aliases = {}
    if prev is not None:
        aliases = {len(args): 1, len(args) + 1: 2}
        in_specs += [_ANY_SPEC, _ANY_SPEC]
        args += list(prev)
    return pl.pallas_call(
        functools.partial(_dn_kernel, TT=TT, c=c, n_pairs=n_pairs, has_state=has_state,
                          n_prev=0 if prev is None else 2, nb=nb),
        grid=(b // nb, l // TT),
        in_specs=in_specs,
        out_specs=[pl.BlockSpec((nb, TT, W_DN), lambda i, j: (i, j, 0)),
                   _state_specs(st_tail, layer, nb), _state_specs(cv_tail, layer, nb)],
        out_shape=[jax.ShapeDtypeStruct((b, l, W_DN), act_dtype),
                   jax.ShapeDtypeStruct((depth, b) + st_tail, F32),
                   jax.ShapeDtypeStruct((depth, b) + cv_tail, F32)],
        scratch_shapes=[pltpu.VMEM((nb, n_pairs, PAIR, PAIR), F32), pltpu.VMEM((nb, 8, DN_CONV_CH), F32)],
        input_output_aliases=aliases,
        compiler_params=_cparams(("parallel", "arbitrary")),
        name="delta_net",
    )(*args)


def _gla_kernel(*refs, TT, m, cs, n_pairs, has_state, n_prev, nb):
    n_in = 4 + (1 if has_state else 0)
    (proj_ref, wa2_ref, ba_ref, nw_ref, *st), rest = _split_refs(refs, n_in, has_state, n_prev)
    s0_ref = st[0] if has_state else None
    o_ref, sout_ref, ST_ref = rest
    t = pl.program_id(1)

    @pl.when(t == 0)
    def _():
        for i in range(nb):
            _load_pair_state(ST_ref.at[i], _batch_view(s0_ref, i), n_pairs)
            if has_state:
                for p in range(n_pairs):
                    ST_ref[i, p] = ST_ref[i, p].T

    for i in range(nb):
        _gla_body(_batch_view(proj_ref, i), wa2_ref, ba_ref, nw_ref, _batch_view(o_ref, i), ST_ref.at[i],
                  TT=TT, m=m, cs=cs, n_pairs=n_pairs)

    @pl.when(t == pl.num_programs(1) - 1)
    def _():
        for i in range(nb):
            for p in range(n_pairs):
                s = ST_ref[i, p].T
                sout_ref[i, 2 * p] = s[0:HEAD_DIM, 0:HEAD_DIM]
                sout_ref[i, 2 * p + 1] = s[HEAD_DIM:PAIR, HEAD_DIM:PAIR]


def _gla_body(proj_ref, wa2_ref, ba_ref, nw_ref, o_ref, ST_ref, *, TT, m, cs, n_pairs):
    bm = _block_diag_mask()
    m0 = _head0_lanes()
    pmean = _group_mean_matrix(bm)
    nw = nw_ref[...]
    ng = m // cs
    n2 = 2 * m
    sh = _log2(cs)
    r2 = _row_iota((n2, n2))
    c2 = _lane_iota((n2, n2))
    incl2 = ((r2 >> sh) == (c2 >> sh)) & (c2 <= r2)
    rt = _row_iota((TT, TT))
    ct = _lane_iota((TT, TT))
    same_t = (rt >> sh) == (ct >> sh)
    tril = jnp.where(same_t & (ct <= rt), 1.0, 0.0).astype(BF16)
    ones_blk = jnp.where(same_t, 1.0, 0.0).astype(BF16)
    chunk_of_row = _row_iota((m, PAIR)) >> sh
    n_groups = TT // m

    a1 = proj_ref[0, :, 4 * W_GLA:GLA_COLS].astype(BF16)
    probs = []
    for p in range(n_pairs):
        lanes = slice(p * PAIR, (p + 1) * PAIR)
        logit = _dot(a1, wa2_ref[:, lanes]) + ba_ref[:, lanes]
        g_all = -_softplus(-logit) * (1.0 / GLA_TAU)
        cum_all = _dot_exact_lhs(tril, g_all)
        last_all = _dot_exact_lhs(ones_blk, g_all)
        q_all = proj_ref[0, :, p * PAIR:(p + 1) * PAIR] * HEAD_DIM ** -0.5
        k_all = proj_ref[0, :, W_GLA + p * PAIR:W_GLA + (p + 1) * PAIR]
        v_all = proj_ref[0, :, 2 * W_GLA + p * PAIR:2 * W_GLA + (p + 1) * PAIR]
        qt_all = q_all * jnp.exp(cum_all)
        kt_all = k_all * jnp.exp(-cum_all)
        kdec_all = (k_all * jnp.exp(last_all - cum_all)).astype(BF16)
        for gi in range(n_groups):
            rows = slice(gi * m, (gi + 1) * m)
            probs.append(dict(p=p, qt=qt_all[rows], kt=kt_all[rows], kdec=kdec_all[rows], v=v_all[rows],
                              last=last_all[rows]))
    for pr in probs:
        pr['sc'] = jnp.where(incl2, _dot_nt(_stack(pr['qt'], m0).astype(BF16), _stack(pr['kt'], m0).astype(BF16)), 0.0)
    for pr in probs:
        vb = pr['v'].astype(BF16)
        o2 = _dot(pr['sc'].astype(BF16), _stack(pr['v'], m0).astype(BF16))
        pr['o_intra'] = o2[:m] + o2[m:]
        qtb = pr['qt'].astype(BF16)
        if ng == 1:
            kd_big, pr['q_big'] = pr['kdec'], qtb
        else:
            kd_big = jnp.concatenate([jnp.where(chunk_of_row == n, pr['kdec'], 0) for n in range(ng)], axis=1)
            pr['q_big'] = jnp.concatenate([jnp.where(chunk_of_row == n, qtb, 0) for n in range(ng)], axis=1)
        pr['kvt'] = _dot_tn(vb, kd_big)
    ST = [ST_ref[p] for p in range(n_pairs)]
    for gi in range(n_groups):
        for p in range(n_pairs):
            pr = probs[p * n_groups + gi]
            sts = []
            for n in range(ng):
                sts.append(ST[p].astype(BF16))
                dec = jnp.exp(pr['last'][n * cs:n * cs + 1, :])
                ST[p] = ST[p] * dec + jnp.where(bm, pr['kvt'][:, n * PAIR:(n + 1) * PAIR], 0.0)
            pr['st_all'] = sts[0] if ng == 1 else jnp.concatenate(sts, axis=1)
    for pr in probs:
        pr['o'] = pr['o_intra'] + _dot_nt(pr['q_big'], pr['st_all'])
    for p in range(n_pairs):
        ST_ref[p] = ST[p]
        outs = [probs[p * n_groups + gi]['o'] for gi in range(n_groups)]
        o = outs[0] if n_groups == 1 else jnp.concatenate(outs, axis=0)
        gate = proj_ref[0, :, 3 * W_GLA + p * PAIR:3 * W_GLA + (p + 1) * PAIR]
        ms = _dot((o * o).astype(BF16), pmean)
        o_ref[0, :, p * PAIR:(p + 1) * PAIR] = (o * lax.rsqrt(ms + EPS) * nw * _silu(gate)).astype(o_ref.dtype)


def gla(proj, w_a2, b_a, s0_all, layer, depth, prev, norm_w, TT, m, cs, nb, act_dtype):
    b, l, _ = proj.shape
    n_pairs = H_GLA // 2
    wa2 = jnp.zeros((PAIR, W_GLA), F32).at[:GLA_RANK].set(w_a2).astype(BF16)
    nw = jnp.concatenate([norm_w, norm_w]).reshape(1, PAIR)
    st_tail = (H_GLA, HEAD_DIM, HEAD_DIM)
    in_specs = [pl.BlockSpec((nb, TT, GLA_COLS), lambda i, j: (i, j, 0)),
                pl.BlockSpec((PAIR, W_GLA), lambda i, j: (0, 0)),
                pl.BlockSpec((1, W_GLA), lambda i, j: (0, 0)),
                pl.BlockSpec((1, PAIR), lambda i, j: (0, 0))]
    args = [proj, wa2, b_a.reshape(1, W_GLA), nw]
    if s0_all is not None:
        in_specs.append(_state_specs(st_tail, layer, nb))
        args.append(s0_all)
    aliases = {}
    if prev is not None:
        aliases[len(args)] = 1
        in_specs.append(_ANY_SPEC)
        args.append(prev)
    return pl.pallas_call(
        functools.partial(_gla_kernel, TT=TT, m=m, cs=cs, n_pairs=n_pairs, has_state=s0_all is not None,
                          n_prev=0 if prev is None else 1, nb=nb),
        grid=(b // nb, l // TT),
        in_specs=in_specs,
        out_specs=[pl.BlockSpec((nb, TT, W_GLA), lambda i, j: (i, j, 0)), _state_specs(st_tail, layer, nb)],
        out_shape=[jax.ShapeDtypeStruct((b, l, W_GLA), act_dtype),
                   jax.ShapeDtypeStruct((depth, b) + st_tail, F32)],
        scratch_shapes=[pltpu.VMEM((nb, n_pairs, PAIR, PAIR), F32)],
        input_output_aliases=aliases,
        compiler_params=_cparams(("parallel", "arbitrary")),
        name="gla",
    )(*args)


def _proj_t_kernel(x_ref, g_ref, w_ref, o_ref, xn_ref):
    @pl.when(pl.program_id(0) == 0)
    def _():
        x = x_ref[...]
        xn = x * lax.rsqrt(jnp.mean(x * x, axis=0, keepdims=True) + EPS) * g_ref[...]
        xn_ref[...] = xn.astype(BF16)

    o_ref[...] = _dot(w_ref[...].astype(BF16), xn_ref[...])


def proj_t(x_t, g, w_t, tn):
    d, t = x_t.shape
    n = w_t.shape[0]
    return pl.pallas_call(
        _proj_t_kernel,
        grid=(n // tn,),
        in_specs=[pl.BlockSpec((d, t), lambda i: (0, 0)),
                  pl.BlockSpec((d, 1), lambda i: (0, 0)),
                  pl.BlockSpec((tn, d), lambda i: (i, 0))],
        out_specs=pl.BlockSpec((tn, t), lambda i: (i, 0)),
        out_shape=jax.ShapeDtypeStruct((n, t), F32),
        scratch_shapes=[pltpu.VMEM((d, t), BF16)],
        compiler_params=_cparams(("arbitrary",)),
        name="proj_t",
    )(x_t, g.reshape(d, 1), w_t)


SUBLANES = 8


def _row_group(ref, g, lanes):
    return ref[pl.ds(pl.multiple_of(g * SUBLANES, SUBLANES), SUBLANES), lanes]


def _pick_row(block, r):
    return jnp.sum(jnp.where(_row_iota(block.shape) == r, block, 0.0), axis=0, keepdims=True)


def _lane_linear_kernel(*refs, n_tok, nbatch, rotary, layer_norm, row_decay, n_prev):
    if rotary:
        q_ref, k_ref, v_ref, gate_ref, cos_ref, sin_ref, dec_ref, nw_ref, s0_ref = refs[:9]
        rest = refs[9 + n_prev:]
    else:
        q_ref, k_ref, v_ref, gate_ref, a1_ref, wa2_ref, ba_ref, nw_ref, s0_ref = refs[:9]
        rest = refs[9 + n_prev:]
    o_ref, sout_ref, q_scr, k_scr, dec_scr = rest
    if rotary:
        half = HEAD_DIM // 2

        def rot(x):
            sw = jnp.concatenate([x[half:], x[:half]], axis=0)
            return x * cos_ref[...] + sw * sin_ref[...]

        q_scr[...] = rot(q_ref[...])
        k_scr[...] = rot(k_ref[...]) * HEAD_DIM ** -0.5
    else:
        q_scr[...] = q_ref[...] * HEAD_DIM ** -0.5
        k_scr[...] = k_ref[...]
        logit = _dot(wa2_ref[...], a1_ref[...].astype(BF16)) + ba_ref[...]
        dec_scr[...] = jnp.exp(-_softplus(-logit) * (1.0 / GLA_TAU))
    nw = nw_ref[...]
    for t in range(n_tok):
        lanes = slice(t * nbatch, (t + 1) * nbatch)
        v_t = v_ref[:, lanes]
        src = s0_ref if t == 0 else sout_ref

        def body(g, o, src=src, lanes=lanes, v_t=v_t):
            kg = _row_group(k_scr, g, lanes)
            qg = _row_group(q_scr, g, lanes)
            dg = _row_group(dec_scr, g, lanes) if row_decay else None
            for i in range(SUBLANES):
                d = g * SUBLANES + i
                dec = dg[i:i + 1] if row_decay else dec_ref[...]
                s_d = src[d] * dec + kg[i:i + 1] * v_t
                sout_ref[d] = s_d
                o = o + qg[i:i + 1] * s_d
            return o

        o = lax.fori_loop(0, HEAD_DIM // SUBLANES, body, jnp.zeros((HEAD_DIM, nbatch), F32))
        if layer_norm:
            o = o - jnp.mean(o, axis=0, keepdims=True)
        o = o * lax.rsqrt(jnp.mean(o * o, axis=0, keepdims=True) + EPS) * nw
        o_ref[:, lanes] = o * _silu(gate_ref[:, lanes])


def _head_rows(width, col0):
    return pl.BlockSpec((HEAD_DIM, width), lambda h: (col0 // HEAD_DIM + h, 0))


def lane_linear_mixer(proj_t_all, col0, width, n_heads, kind, consts, s_all, layer, depth, prev, n_tok, nbatch):
    t_all = n_tok * nbatch
    rotary = kind == 'ret'
    hr = lambda off: _head_rows(t_all, col0 + off)
    state_spec = pl.BlockSpec((None, None, HEAD_DIM, HEAD_DIM, nbatch), lambda h: (layer, h, 0, 0, 0))
    in_specs = [hr(0), hr(width), hr(2 * width), hr(3 * width)]
    args = [proj_t_all] * 4
    if rotary:
        cos_t, sin_t, gam, nw = consts
        in_specs += [pl.BlockSpec((HEAD_DIM, t_all), lambda h: (0, 0)),
                     pl.BlockSpec((HEAD_DIM, t_all), lambda h: (0, 0)),
                     pl.BlockSpec((None, 1, nbatch), lambda h: (h, 0, 0)),
                     pl.BlockSpec((HEAD_DIM, nbatch), lambda h: (0, 0))]
        args += [cos_t, sin_t, gam, nw]
    else:
        wa2_t, ba_t, nw = consts
        in_specs += [pl.BlockSpec((LANES, t_all), lambda h: ((col0 + 4 * width) // LANES, 0)),
                     pl.BlockSpec((HEAD_DIM, LANES), lambda h: (h, 0)),
                     pl.BlockSpec((HEAD_DIM, 1), lambda h: (h, 0)),
                     pl.BlockSpec((HEAD_DIM, nbatch), lambda h: (0, 0))]
        args += [proj_t_all, wa2_t, ba_t, nw]
    in_specs.append(state_spec)
    args.append(s_all)
    aliases = {}
    if prev is not None:
        aliases[len(args)] = 1
        in_specs.append(_ANY_SPEC)
        args.append(prev)
    return pl.pallas_call(
        functools.partial(_lane_linear_kernel, n_tok=n_tok, nbatch=nbatch, rotary=rotary, layer_norm=rotary,
                          row_decay=not rotary, n_prev=0 if prev is None else 1),
        grid=(n_heads,),
        in_specs=in_specs,
        out_specs=[pl.BlockSpec((HEAD_DIM, t_all), lambda h: (h, 0)), state_spec],
        out_shape=[jax.ShapeDtypeStruct((width, t_all), F32),
                   jax.ShapeDtypeStruct((depth, n_heads, HEAD_DIM, HEAD_DIM, nbatch), F32)],
        scratch_shapes=[pltpu.VMEM((HEAD_DIM, t_all), F32)] * 3,
        input_output_aliases=aliases,
        compiler_params=_cparams(("arbitrary",)),
        name="lane_" + kind,
    )(*args)


def _lane_delta_kernel(*refs, n_tok, nbatch, n_prev):
    (qp_ref, kp_ref, vp_ref, gate_ref, misc_ref, wq_ref, wk_ref, wv_ref, cq_ref, ck_ref, cv_ref,
     alog_ref, dtb_ref, nw_ref, s0_ref) = refs[:15]
    o_ref, sout_ref, cqo_ref, cko_ref, cvo_ref, q_scr, k_scr = refs[15 + n_prev:]
    h = pl.program_id(0)

    def conv(x_ref, w_ref, c_ref, cout_ref):
        xs = [c_ref[j] for j in range(CONV_W - 1)] + [x_ref[:, t * nbatch:(t + 1) * nbatch] for t in range(n_tok)]
        outs = []
        for t in range(n_tok):
            acc = xs[t] * w_ref[0]
            for j in range(1, CONV_W):
                acc = acc + xs[t + j] * w_ref[j]
            outs.append(_silu(acc))
        for j in range(CONV_W - 1):
            cout_ref[j] = xs[n_tok + j]
        return jnp.concatenate(outs, axis=1)

    q = conv(qp_ref, wq_ref, cq_ref, cqo_ref)
    k = conv(kp_ref, wk_ref, ck_ref, cko_ref)
    v = conv(vp_ref, wv_ref, cv_ref, cvo_ref)
    q_scr[...] = q * lax.rsqrt(jnp.sum(q * q, axis=0, keepdims=True) + EPS) * HEAD_DIM ** -0.5
    k_scr[...] = k * lax.rsqrt(jnp.sum(k * k, axis=0, keepdims=True) + EPS)
    misc = misc_ref[0:2 * SUBLANES, :]
    beta = jax.nn.sigmoid(_pick_row(misc, h))
    glog = -jnp.exp(alog_ref[...]) * _softplus(_pick_row(misc, DN_A_LANE + h) + dtb_ref[...])
    a_all = jnp.exp(glog)
    nw = nw_ref[...]
    zero = jnp.zeros((HEAD_DIM, nbatch), F32)
    for t in range(n_tok):
        lanes = slice(t * nbatch, (t + 1) * nbatch)
        src = s0_ref if t == 0 else sout_ref
        a_t = a_all[:, lanes]

        def kts(g, r, src=src, lanes=lanes):
            kg = _row_group(k_scr, g, lanes)
            for i in range(SUBLANES):
                r = r + kg[i:i + 1] * src[g * SUBLANES + i]
            return r

        r = lax.fori_loop(0, HEAD_DIM // SUBLANES, kts, zero)
        w = beta[:, lanes] * (v[:, lanes] - a_t * r)

        def body(g, o, src=src, lanes=lanes, a_t=a_t, w=w):
            kg = _row_group(k_scr, g, lanes)
            qg = _row_group(q_scr, g, lanes)
            for i in range(SUBLANES):
                d = g * SUBLANES + i
                s_d = src[d] * a_t + kg[i:i + 1] * w
                sout_ref[d] = s_d
                o = o + qg[i:i + 1] * s_d
            return o

        o = lax.fori_loop(0, HEAD_DIM // SUBLANES, body, zero)
        o = o * lax.rsqrt(jnp.mean(o * o, axis=0, keepdims=True) + EPS) * nw
        o_ref[:, lanes] = o * _silu(gate_ref[:, lanes])


def lane_delta_net(proj_t_all, col0, conv_w_b, conv_parts, alog_b, dtb_b, nw, s_all, layer, depth, prev, n_tok, nbatch):
    t_all = n_tok * nbatch
    hr = lambda off: _head_rows(t_all, col0 + off)
    state_spec = pl.BlockSpec((None, None, HEAD_DIM, HEAD_DIM, nbatch), lambda h: (layer, h, 0, 0, 0))
    tap_spec = lambda i: pl.BlockSpec((None, CONV_W, HEAD_DIM, nbatch), lambda h, i=i: (i, 0, h, 0))
    cst_spec = pl.BlockSpec((None, CONV_W - 1, HEAD_DIM, nbatch), lambda h: (layer, 0, h, 0))
    head_row = pl.BlockSpec((None, 1, 1), lambda h: (h, 0, 0))
    in_specs = [hr(0), hr(W_DN), hr(2 * W_DN), hr(DN_CONV_CH),
                pl.BlockSpec((LANES, t_all), lambda h: ((col0 + DN_CONV_CH + W_DN) // LANES, 0)),
                tap_spec(0), tap_spec(1), tap_spec(2), cst_spec, cst_spec, cst_spec,
                head_row, head_row, pl.BlockSpec((HEAD_DIM, nbatch), lambda h: (0, 0)), state_spec]
    args = [proj_t_all] * 5 + [conv_w_b] * 3 + list(conv_parts) + [alog_b, dtb_b, nw, s_all]
    aliases = {}
    if prev is not None:
        aliases = {len(args) + i: 1 + i for i in range(4)}
        in_specs += [_ANY_SPEC] * 4
        args += list(prev)
    cshape = jax.ShapeDtypeStruct((depth, CONV_W - 1, W_DN, nbatch), F32)
    return pl.pallas_call(
        functools.partial(_lane_delta_kernel, n_tok=n_tok, nbatch=nbatch, n_prev=0 if prev is None else 4),
        grid=(H_DN,),
        in_specs=in_specs,
        out_specs=[pl.BlockSpec((HEAD_DIM, t_all), lambda h: (h, 0)), state_spec, cst_spec, cst_spec, cst_spec],
        out_shape=[jax.ShapeDtypeStruct((W_DN, t_all), F32),
                   jax.ShapeDtypeStruct((depth, H_DN, HEAD_DIM, HEAD_DIM, nbatch), F32), cshape, cshape, cshape],
        scratch_shapes=[pltpu.VMEM((HEAD_DIM, t_all), F32)] * 2,
        input_output_aliases=aliases,
        compiler_params=_cparams(("arbitrary",)),
        name="lane_delta",
    )(*args)


def _prep_layer_weights(li, w_in_t, w_out, w_mq, w_mo, w_ff1, w_ff2):
    wi = w_in_t[li]
    d = wi.shape[1]
    c0 = RET_COLS
    c1 = c0 + DN_CONV_CH + W_DN + 2 * H_DN
    w_ret = wi[:c0]
    w_dn = jnp.concatenate([wi[c0:c1], jnp.zeros((DN_COLS - (c1 - c0), d), F32)], axis=0)
    w_gla = jnp.concatenate([wi[c1:], jnp.zeros((GLA_COLS - (wi.shape[0] - c1), d), F32)], axis=0)
    cast = lambda a: a.astype(BF16)
    return dict(w_ret=w_ret, w_dn=w_dn, w_gla=w_gla, w_all_t=jnp.concatenate([w_ret, w_dn, w_gla], axis=0),
                w_out=cast(w_out[li]),
                w_mq=w_mq[li], w_mo=cast(w_mo[li]), w_ff1=cast(w_ff1[li]), w_ff2=cast(w_ff2[li]))


def _run_trunk(x, pos, s_ret, s_dn, s_conv, s_gla, mem_k, mem_v, mem_split, lw, p, cfg):
    b, l, d = x.shape
    t = b * l
    depth = len(lw)
    xf = x.reshape(t, d)
    lane_batch = cfg['lane_batch']
    act = cfg['act_dtype']
    if lane_batch:
        lane = _lane_setup(pos, b, s_ret, s_dn, s_conv, s_gla)
    else:
        ret_tables = _ret_consts(cfg['ret_c']) + _rope_tables(pos)
    r_s = d_s = g_s = None
    for li in range(depth):
        w = lw[li]
        g = p['norm_mix'][li]
        if lane_batch:
            o_mix, r_s, d_s, g_s = _lane_mixers(xf, b, l, li, depth, w, p, lane, r_s, d_s, g_s)
            xf = matmul_res([o_mix], w['w_out'], xf, cfg['tm'])
        else:
            pr = rms_matmul(xf, g, w['w_ret'], cfg['tm'], w_transposed=True).reshape(b, l, RET_COLS)
            pd = rms_matmul(xf, g, w['w_dn'], cfg['tm'], w_transposed=True).reshape(b, l, DN_COLS)
            pg = rms_matmul(xf, g, w['w_gla'], cfg['tm'], w_transposed=True).reshape(b, l, GLA_COLS)
            o_ret, r_s = retention(pr, ret_tables, s_ret, li, depth, r_s, p['ret_norm_w'][li], cfg['ret_c'],
                                   cfg['nb'], act)
            o_dn, *d_s = delta_net(pd, p['dn_conv_w'][li], s_conv, p['dn_a_log'][li], p['dn_dt_bias'][li], s_dn,
                                   li, depth, d_s, p['dn_norm_w'][li], cfg['dn_tt'], cfg['dn_c'], cfg['nb'], act)
            o_gla, g_s = gla(pg, p['gla_w_a2'][li], p['gla_b_a'][li], s_gla, li, depth, g_s, p['gla_norm_w'][li],
                             cfg['gla_tt'], cfg['gla_m'], cfg['gla_cs'], cfg['nb'], act)
            xf = matmul_res([o_ret.reshape(t, W_RET), o_dn.reshape(t, W_DN), o_gla.reshape(t, W_GLA)],
                            w['w_out'], xf, cfg['tm'])
        qm = rms_matmul(xf, p['norm_cross'][li], w['w_mq'], cfg['tm'], act).reshape(b, l, d)
        if mem_split:
            att = mem_attention(qm, mem_k, mem_v, cfg['tq'], cfg['attn_nb'], layer=li)
        else:
            att = mem_attention(qm, mem_k[li], mem_v[li], cfg['tq'], cfg['attn_nb'])
        xf = matmul_res([att.reshape(t, d)], w['w_mo'], xf, cfg['tm'])
        xf = ffn(xf, p['norm_ffn'][li], w['w_ff1'], w['w_ff2'], p['norm_final'], li == depth - 1,
                 cfg['ffn_tm'], cfg['ffn_tf'])
    if lane_batch:
        to_batch_major = lambda s: jnp.transpose(s, (0, 4, 1, 2, 3))
        conv = jnp.transpose(jnp.concatenate(d_s[1:], axis=2), (0, 3, 1, 2))
        return (xf.reshape(b, l, d), to_batch_major(r_s), to_batch_major(d_s[0]), conv, to_batch_major(g_s))
    return (xf.reshape(b, l, d), r_s, d_s[0], d_s[1], g_s)


def _lane_setup(pos, b, s_ret, s_dn, s_conv, s_gla):
    half = HEAD_DIM // 2
    inv_freq = ROPE_BASE ** (-jnp.arange(half, dtype=F32) / half)
    ang = pos.astype(F32)[:, None] * inv_freq[None, :]
    cos, sin = jnp.cos(ang), jnp.sin(ang)
    spread = lambda a: jnp.repeat(a.T, b, axis=1)
    lg = jnp.log1p(-jnp.exp2(-5.0 - jnp.arange(H_RET, dtype=F32)))
    to_lane = lambda s: jnp.transpose(s, (0, 2, 3, 4, 1))
    conv_t = jnp.transpose(s_conv, (0, 2, 3, 1))
    return dict(cos=spread(jnp.concatenate([cos, cos], axis=1)), sin=spread(jnp.concatenate([-sin, sin], axis=1)),
                gam=jnp.broadcast_to(jnp.exp(lg)[:, None, None], (H_RET, 1, b)),
                s_ret=to_lane(s_ret), s_dn=to_lane(s_dn), s_gla=to_lane(s_gla),
                conv=[conv_t[:, :, i * W_DN:(i + 1) * W_DN, :] for i in range(3)])


def _lane_mixers(xf, b, l, li, depth, w, p, lane, r_s, d_s, g_s):
    d = xf.shape[1]
    col = lambda v: jnp.broadcast_to(v[:, None], (v.shape[0], b))
    x_t = xf.reshape(b, l, d).transpose(2, 1, 0).reshape(d, l * b)
    pt = proj_t(x_t, p['norm_mix'][li], w['w_all_t'], 256)
    o_ret, r_s = lane_linear_mixer(pt, 0, W_RET, H_RET, 'ret',
                                   (lane['cos'], lane['sin'], lane['gam'], col(p['ret_norm_w'][li])),
                                   lane['s_ret'], li, depth, r_s, l, b)
    taps = p['dn_conv_w'][li].reshape(CONV_W, 3, W_DN).transpose(1, 0, 2)
    taps = jnp.broadcast_to(taps[..., None], taps.shape + (b,))
    o_dn, *d_s = lane_delta_net(pt, RET_COLS, taps, lane['conv'], p['dn_a_log'][li].reshape(H_DN, 1, 1),
                                p['dn_dt_bias'][li].reshape(H_DN, 1, 1), col(p['dn_norm_w'][li]),
                                lane['s_dn'], li, depth, d_s, l, b)
    wa2_t = jnp.zeros((W_GLA, LANES), F32).at[:, :GLA_RANK].set(p['gla_w_a2'][li].T).astype(BF16)
    o_gla, g_s = lane_linear_mixer(pt, RET_COLS + DN_COLS, W_GLA, H_GLA, 'gla',
                                   (wa2_t, p['gla_b_a'][li].reshape(W_GLA, 1), col(p['gla_norm_w'][li])),
                                   lane['s_gla'], li, depth, g_s, l, b)
    o_t = jnp.concatenate([o_ret, o_dn, o_gla], axis=0)
    o = o_t.reshape(o_t.shape[0], l, b).transpose(2, 1, 0).reshape(b * l, o_t.shape[0])
    return o, r_s, d_s, g_s


def kernel(x_prompt, x_sample, state_ret, state_dn, state_dn_conv, state_gla, cache_mem_k, cache_mem_v, mem_prompt, norm_mix, w_in, ret_norm_w, dn_conv_w, dn_a_log, dn_dt_bias, dn_norm_w, gla_w_a2, gla_b_a, gla_norm_w, w_out, norm_cross, norm_mem, w_mq, w_mk, w_mv, w_mo, norm_ffn, w_ff1, w_ff2, norm_final):
    b_p, l_p, d = x_prompt.shape
    b_s, l_s, _ = x_sample.shape
    depth = w_in.shape[0]
    n_mem = mem_prompt.shape[1]
    p = dict(norm_mix=norm_mix, ret_norm_w=ret_norm_w, dn_conv_w=dn_conv_w, dn_a_log=dn_a_log,
             dn_dt_bias=dn_dt_bias, dn_norm_w=dn_norm_w, gla_w_a2=gla_w_a2, gla_b_a=gla_b_a,
             gla_norm_w=gla_norm_w, norm_cross=norm_cross, norm_ffn=norm_ffn, norm_final=norm_final)
    w_in_t = jnp.transpose(w_in, (0, 2, 1))
    lw = [_prep_layer_weights(li, w_in_t, w_out, w_mq, w_mo, w_ff1, w_ff2) for li in range(depth)]

    memf = mem_prompt.reshape(b_p * n_mem, d)
    mk = [rms_matmul(memf, norm_mem[li], w_mk[li], 512).reshape(b_p, n_mem, d) for li in range(depth)]
    mv = [rms_matmul(memf, norm_mem[li], w_mv[li], 512).reshape(b_p, n_mem, d) for li in range(depth)]

    cfg_p = dict(tm=512, ret_c=256, dn_tt=256, dn_c=math.gcd(l_p, DN_CHUNK), gla_tt=256, gla_m=64,
                 gla_cs=math.gcd(l_p, GLA_CHUNK), tq=512, attn_nb=1, ffn_tm=1024, ffn_tf=1024, nb=1, act_dtype=BF16,
                 lane_batch=False)
    y_p, p_ret, p_dn, p_conv, p_gla = _run_trunk(
        x_prompt, jnp.arange(l_p, dtype=jnp.int32), None, None, None, None, mk, mv, False, lw, p, cfg_p)

    cfg_s = dict(tm=512, ret_c=l_s, dn_tt=l_s, dn_c=math.gcd(l_s, DN_CHUNK), gla_tt=l_s,
                 gla_m=math.gcd(l_s, GLA_CHUNK), gla_cs=math.gcd(l_s, GLA_CHUNK), tq=l_s, ffn_tm=1024, ffn_tf=512,
                 attn_nb=2, nb=8, act_dtype=F32, lane_batch=True)
    y_s, s_ret, s_dn, s_conv, s_gla = _run_trunk(
        x_sample, PAST_LEN + jnp.arange(l_s, dtype=jnp.int32), state_ret, state_dn, state_dn_conv, state_gla,
        cache_row_view(cache_mem_k), cache_row_view(cache_mem_v), True, lw, p, cfg_s)

    mem_shape = (depth, b_p, n_mem, MEM_HEADS, d // MEM_HEADS)
    return (y_p, y_s, p_ret, p_dn, p_conv, p_gla, jnp.stack(mk).reshape(mem_shape), jnp.stack(mv).reshape(mem_shape),
            s_ret, s_dn, s_conv, s_gla)
```

```python
import functools
import math

import jax
import jax.numpy as jnp
from jax import lax
from jax.experimental import pallas as pl
from jax.experimental.pallas import tpu as pltpu

F32 = jnp.float32
BF16 = jnp.bfloat16

LANES = 128
HEAD_DIM = 64
PAIR = 2 * HEAD_DIM
H_RET, H_DN, H_GLA = 6, 6, 4
W_RET, W_DN, W_GLA = H_RET * HEAD_DIM, H_DN * HEAD_DIM, H_GLA * HEAD_DIM
CONV_W = 4
DN_CONV_CH = 3 * W_DN
GLA_RANK = 16
GLA_TAU = 16.0
MEM_HEADS = 4
DN_CHUNK = 64
GLA_CHUNK = 16
ROPE_BASE = 10000.0
EPS = 1e-6
PAST_LEN = 16384

DN_COLS = DN_CONV_CH + W_DN + PAIR
GLA_COLS = 4 * W_GLA + PAIR
RET_COLS = 4 * W_RET
DN_A_LANE = H_DN
TRI_BASE = 16

VMEM_LIMIT = 56 * 1024 * 1024


def _cparams(sem):
    return pltpu.CompilerParams(dimension_semantics=sem, vmem_limit_bytes=VMEM_LIMIT)


def _log2(n):
    l = int(math.log2(n))
    assert (1 << l) == n, n
    return l


def _dot(a, b):
    return jnp.dot(a, b, preferred_element_type=F32)


def _dot_nt(a, b):
    return lax.dot_general(a, b, (((1,), (1,)), ((), ())), preferred_element_type=F32)


def _dot_tn(a, b):
    return lax.dot_general(a, b, (((0,), (0,)), ((), ())), preferred_element_type=F32)


def _split2(a):
    hi = a.astype(BF16)
    lo = (a - hi.astype(F32)).astype(BF16)
    return hi, lo


def _split3(a):
    hi = a.astype(BF16)
    r = a - hi.astype(F32)
    mid = r.astype(BF16)
    lo = (r - mid.astype(F32)).astype(BF16)
    return hi, mid, lo


def _dot_exact_lhs(a_bf16, b):
    hi, mid, lo = _split3(b)
    return _dot(a_bf16, hi) + _dot(a_bf16, mid) + _dot(a_bf16, lo)


def _dot_nt_exact(a_bf16, b):
    hi, mid, lo = _split3(b)
    return _dot_nt(a_bf16, hi) + _dot_nt(a_bf16, mid) + _dot_nt(a_bf16, lo)


def _silu(x):
    return x * jax.nn.sigmoid(x)


def _softplus(x):
    return jnp.maximum(x, 0.0) + jnp.log1p(jnp.exp(-jnp.abs(x)))


def _lane_iota(shape):
    return lax.broadcasted_iota(jnp.int32, shape, len(shape) - 1)


def _row_iota(shape):
    return lax.broadcasted_iota(jnp.int32, shape, 0)


def _head0_lanes():
    return _lane_iota((1, PAIR)) < HEAD_DIM


def _stack(x, lo):
    return jnp.concatenate([jnp.where(lo, x, 0.0), jnp.where(lo, 0.0, x)], axis=0)


def _block_diag_mask():
    r = _row_iota((PAIR, PAIR))
    c = _lane_iota((PAIR, PAIR))
    return (r >> 6) == (c >> 6)


def _group_mean_matrix(bm):
    return jnp.where(bm, 1.0 / HEAD_DIM, 0.0).astype(BF16)


def _load_pair_state(S_ref, s0_ref, n_pairs):
    S_ref[...] = jnp.zeros_like(S_ref)
    if s0_ref is not None:
        for p in range(n_pairs):
            S_ref[p, 0:HEAD_DIM, 0:HEAD_DIM] = s0_ref[0, 2 * p]
            S_ref[p, HEAD_DIM:PAIR, HEAD_DIM:PAIR] = s0_ref[0, 2 * p + 1]


def _store_pair_state(sout_ref, S_ref, n_pairs):
    for p in range(n_pairs):
        sout_ref[0, 2 * p] = S_ref[p, 0:HEAD_DIM, 0:HEAD_DIM]
        sout_ref[0, 2 * p + 1] = S_ref[p, HEAD_DIM:PAIR, HEAD_DIM:PAIR]


def _split_refs(refs, n_in, has_state, n_prev):
    ins = refs[:n_in]
    refs = refs[n_in:]
    return ins, refs[n_prev:]


def _rms_matmul_kernel(x_ref, g_ref, w_ref, o_ref, wb_ref, *, w_transposed):
    @pl.when(pl.program_id(0) == 0)
    def _():
        wb_ref[...] = w_ref[...].astype(BF16)

    x = x_ref[...]
    xn = (x * lax.rsqrt(jnp.mean(x * x, axis=-1, keepdims=True) + EPS) * g_ref[...]).astype(BF16)
    y = _dot_nt(xn, wb_ref[...]) if w_transposed else _dot(xn, wb_ref[...])
    o_ref[...] = y.astype(o_ref.dtype)


def rms_matmul(x, g, w, tm, out_dtype=F32, w_transposed=False):
    t, d = x.shape
    n = w.shape[0] if w_transposed else w.shape[1]
    return pl.pallas_call(
        functools.partial(_rms_matmul_kernel, w_transposed=w_transposed),
        grid=(t // tm,),
        in_specs=[pl.BlockSpec((tm, d), lambda i: (i, 0)),
                  pl.BlockSpec((1, d), lambda i: (0, 0)),
                  pl.BlockSpec(w.shape, lambda i: (0, 0))],
        out_specs=pl.BlockSpec((tm, n), lambda i: (i, 0)),
        out_shape=jax.ShapeDtypeStruct((t, n), out_dtype),
        scratch_shapes=[pltpu.VMEM(w.shape, BF16)],
        compiler_params=_cparams(("arbitrary",)),
        name="rms_matmul",
    )(x, g.reshape(1, d), w)


def _in_proj_kernel(x_ref, g_ref, wt_ref, *o_refs):
    x = x_ref[...]
    xn = (x * lax.rsqrt(jnp.mean(x * x, axis=-1, keepdims=True) + EPS) * g_ref[...]).astype(BF16)
    off = 0
    for o_ref in o_refs:
        n = o_ref.shape[1]
        o_ref[...] = _dot_nt(xn, wt_ref[off:off + n, :])
        off += n


def in_proj(x, g, wt_bf16, widths, tm):
    t, d = x.shape
    assert sum(widths) == wt_bf16.shape[0]
    return pl.pallas_call(
        _in_proj_kernel,
        grid=(t // tm,),
        in_specs=[pl.BlockSpec((tm, d), lambda i: (i, 0)),
                  pl.BlockSpec((1, d), lambda i: (0, 0)),
                  pl.BlockSpec(wt_bf16.shape, lambda i: (0, 0))],
        out_specs=[pl.BlockSpec((tm, n), lambda i: (i, 0)) for n in widths],
        out_shape=[jax.ShapeDtypeStruct((t, n), F32) for n in widths],
        compiler_params=_cparams(("parallel",)),
        name="in_proj",
    )(x, g.reshape(1, d), wt_bf16)


def _matmul_res_kernel(*refs, widths):
    n_in = len(widths)
    a_refs = refs[:n_in]
    w_ref, r_ref, o_ref = refs[n_in:]
    acc = r_ref[...]
    off = 0
    for a_ref, wk in zip(a_refs, widths):
        acc = acc + _dot(a_ref[...].astype(BF16), w_ref[off:off + wk, :])
        off += wk
    o_ref[...] = acc


def matmul_res(a_list, w_bf16, res, tm):
    t, n = res.shape
    widths = tuple(a.shape[1] for a in a_list)
    k = w_bf16.shape[0]
    assert sum(widths) == k
    in_specs = [pl.BlockSpec((tm, wk), lambda i: (i, 0)) for wk in widths]
    in_specs += [pl.BlockSpec((k, n), lambda i: (0, 0)), pl.BlockSpec((tm, n), lambda i: (i, 0))]
    return pl.pallas_call(
        functools.partial(_matmul_res_kernel, widths=widths),
        grid=(t // tm,),
        in_specs=in_specs,
        out_specs=pl.BlockSpec((tm, n), lambda i: (i, 0)),
        out_shape=jax.ShapeDtypeStruct((t, n), F32),
        compiler_params=_cparams(("parallel",)),
        name="matmul_res",
    )(*a_list, w_bf16, res)


def _ffn_kernel(x_ref, a_ref, wo_ref, g_ref, w1_ref, w2_ref, gf_ref, o_ref, xn_ref, acc_ref, *, final_norm):
    j = pl.program_id(1)

    @pl.when(j == 0)
    def _():
        x = x_ref[...] + _dot(a_ref[...].astype(BF16), wo_ref[...])
        xn = x * lax.rsqrt(jnp.mean(x * x, axis=-1, keepdims=True) + EPS) * g_ref[...]
        xn_ref[...] = xn.astype(BF16)
        acc_ref[...] = x

    h = _dot(xn_ref[...], w1_ref[...])
    h = jnp.square(jnp.maximum(h, 0.0))
    acc_ref[...] += _dot(h.astype(BF16), w2_ref[...])

    @pl.when(j == pl.num_programs(1) - 1)
    def _():
        y = acc_ref[...]
        if final_norm:
            y = y * lax.rsqrt(jnp.mean(y * y, axis=-1, keepdims=True) + EPS) * gf_ref[...]
        o_ref[...] = y


def ffn(x, a, wo_bf16, g, w1_bf16, w2_bf16, g_final, final_norm, tm, tf):
    t, d = x.shape
    f = w1_bf16.shape[1]
    ka = a.shape[1]
    return pl.pallas_call(
        functools.partial(_ffn_kernel, final_norm=final_norm),
        grid=(t // tm, f // tf),
        in_specs=[pl.BlockSpec((tm, d), lambda i, j: (i, 0)),
                  pl.BlockSpec((tm, ka), lambda i, j: (i, 0)),
                  pl.BlockSpec((ka, d), lambda i, j: (0, 0)),
                  pl.BlockSpec((1, d), lambda i, j: (0, 0)),
                  pl.BlockSpec((d, tf), lambda i, j: (0, j)),
                  pl.BlockSpec((tf, d), lambda i, j: (j, 0)),
                  pl.BlockSpec((1, d), lambda i, j: (0, 0))],
        out_specs=pl.BlockSpec((tm, d), lambda i, j: (i, 0)),
        out_shape=jax.ShapeDtypeStruct((t, d), F32),
        scratch_shapes=[pltpu.VMEM((tm, d), BF16), pltpu.VMEM((tm, d), F32)],
        compiler_params=_cparams(("parallel", "arbitrary")),
        name="ffn",
    )(x, a, wo_bf16, g.reshape(1, d), w1_bf16, w2_bf16, g_final.reshape(1, d))


def _attn_kernel(q_ref, k_ref, v_ref, o_ref, *, n_heads, dh, row_view, nb):
    scale = dh ** -0.5
    lane_tiles = dh // LANES
    rows_per_token = n_heads * lane_tiles

    def head_slab(ref, i, h):
        if not row_view:
            return ref[i, :, h * dh:(h + 1) * dh].astype(BF16)
        n_mem = ref.shape[1] // rows_per_token
        parts = [ref[i, pl.ds(j * n_heads + h, n_mem, stride=rows_per_token), :] for j in range(lane_tiles)]
        return jnp.concatenate(parts, axis=1).astype(BF16)

    for i in range(nb):
        for h in range(n_heads):
            sl = slice(h * dh, (h + 1) * dh)
            q = q_ref[i, :, sl].astype(BF16)
            s = _dot_nt(q, head_slab(k_ref, i, h)) * scale
            m = jnp.max(s, axis=-1, keepdims=True)
            e = jnp.exp(s - m)
            l = jnp.sum(e, axis=-1, keepdims=True)
            o_ref[i, :, sl] = (_dot(e.astype(BF16), head_slab(v_ref, i, h)) / l).astype(o_ref.dtype)


def mem_attention(q, mem_k, mem_v, tq, nb, layer=None):
    b, l, d = q.shape
    dh = d // MEM_HEADS
    if layer is None:
        m = mem_k.shape[1]
        mem_spec = pl.BlockSpec((nb, m, d), lambda i, j: (i, 0, 0))
    else:
        rows = mem_k.shape[1]
        mem_spec = pl.BlockSpec((nb, rows, LANES), lambda i, j: (layer * (b // nb) + i, 0, 0))
    return pl.pallas_call(
        functools.partial(_attn_kernel, n_heads=MEM_HEADS, dh=dh, row_view=layer is not None, nb=nb),
        grid=(b // nb, l // tq),
        in_specs=[pl.BlockSpec((nb, tq, d), lambda i, j: (i, j, 0)), mem_spec, mem_spec],
        out_specs=pl.BlockSpec((nb, tq, d), lambda i, j: (i, j, 0)),
        out_shape=jax.ShapeDtypeStruct((b, l, d), q.dtype),
        compiler_params=_cparams(("parallel", "arbitrary")),
        name="mem_attention",
    )(q, mem_k, mem_v)


def cache_row_view(cache):
    depth, b, m, nh, dh = cache.shape
    c = cache.reshape(depth, b, m, nh, dh // LANES, LANES).transpose(0, 1, 2, 4, 3, 5)
    return c.reshape(depth * b, m * nh * (dh // LANES), LANES)


def _state_specs(shape_tail, layer, nb):
    nd = len(shape_tail)
    return pl.BlockSpec((None, nb) + tuple(shape_tail), lambda i, j: (layer, i) + (0,) * nd)


_ANY_SPEC = pl.BlockSpec(memory_space=pl.ANY)


def _batch_view(ref, i):
    return None if ref is None else ref.at[pl.ds(i, 1)]


def _ret_kernel(*refs, C, n_pairs, has_state, n_prev, nb):
    n_in = 8 + (1 if has_state else 0)
    (proj_ref, cos_ref, sin_ref, dm_ref, xi_ref, zeta_ref, gc_ref, lnw_ref, *st), rest = \
        _split_refs(refs, n_in, has_state, n_prev)
    s0_ref = st[0] if has_state else None
    o_ref, sout_ref, S_ref = rest
    t = pl.program_id(1)

    @pl.when(t == 0)
    def _():
        for i in range(nb):
            _load_pair_state(S_ref.at[i], _batch_view(s0_ref, i), n_pairs)

    for i in range(nb):
        _ret_body(_batch_view(proj_ref, i), cos_ref, sin_ref, dm_ref, xi_ref, zeta_ref, gc_ref, lnw_ref,
                  _batch_view(o_ref, i), S_ref.at[i], C=C, n_pairs=n_pairs)

    @pl.when(t == pl.num_programs(1) - 1)
    def _():
        for i in range(nb):
            _store_pair_state(_batch_view(sout_ref, i), S_ref.at[i], n_pairs)


def _ret_body(proj_ref, cos_ref, sin_ref, dm_ref, xi_ref, zeta_ref, gc_ref, lnw_ref, o_ref, S_ref, *, C, n_pairs):
    m0 = _head0_lanes()
    first_half = (_lane_iota((1, PAIR)) & (HEAD_DIM - 1)) < HEAD_DIM // 2
    bm = _block_diag_mask()
    pmean = _group_mean_matrix(bm)
    cos = cos_ref[...]
    sin = sin_ref[...]
    lnw = lnw_ref[...]

    def rot(x):
        sw = jnp.where(first_half, pltpu.roll(x, PAIR - HEAD_DIM // 2, 1), pltpu.roll(x, HEAD_DIM // 2, 1))
        return x * cos + sw * sin

    pairs = range(n_pairs)
    q = [rot(proj_ref[0, :, p * PAIR:(p + 1) * PAIR]) for p in pairs]
    k = [rot(proj_ref[0, :, W_RET + p * PAIR:W_RET + (p + 1) * PAIR]) * HEAD_DIM ** -0.5 for p in pairs]
    vb = [proj_ref[0, :, 2 * W_RET + p * PAIR:2 * W_RET + (p + 1) * PAIR].astype(BF16) for p in pairs]
    sc = [_dot_nt(_stack(q[p], m0).astype(BF16), k[p].astype(BF16)) * dm_ref[p] for p in pairs]
    S = [S_ref[p] for p in pairs]
    o_inter = [_dot(q[p].astype(BF16), S[p].astype(BF16)) * xi_ref[p] for p in pairs]
    kv = [_dot_tn((k[p] * zeta_ref[p]).astype(BF16), vb[p]) for p in pairs]
    o2 = [_dot(sc[p].astype(BF16), vb[p]) for p in pairs]
    o = [jnp.where(m0, o2[p][:C], o2[p][C:]) + o_inter[p] for p in pairs]
    oc = [o[p] - _dot(o[p].astype(BF16), pmean) for p in pairs]
    var = [_dot((oc[p] * oc[p]).astype(BF16), pmean) for p in pairs]
    for p in pairs:
        S_ref[p] = S[p] * gc_ref[p][0:1, :] + jnp.where(bm, kv[p], 0.0)
        gate = proj_ref[0, :, 3 * W_RET + p * PAIR:3 * W_RET + (p + 1) * PAIR]
        o_ref[0, :, p * PAIR:(p + 1) * PAIR] = (oc[p] * lax.rsqrt(var[p] + EPS) * lnw * _silu(gate)).astype(o_ref.dtype)


def _ret_consts(C):
    lg = jnp.log1p(-jnp.exp2(-5.0 - jnp.arange(H_RET, dtype=F32)))
    idx = jnp.arange(C, dtype=F32)
    diff = idx[:, None] - idx[None, :]
    dmask = jnp.where(diff >= 0, jnp.exp(lg[:, None, None] * jnp.maximum(diff, 0.0)), 0.0)
    dm = dmask.reshape(H_RET // 2, 2 * C, C)
    xi = jnp.exp(lg[:, None] * (idx + 1.0))
    zeta = jnp.exp(lg[:, None] * (C - 1.0 - idx))
    gch = jnp.exp(lg * C)

    def pair_lanes(a):
        r = a.shape[1]
        a = jnp.repeat(a[:, :, None], HEAD_DIM, axis=2)
        a = a.reshape(H_RET // 2, 2, r, HEAD_DIM).transpose(0, 2, 1, 3)
        return a.reshape(H_RET // 2, r, PAIR)

    gc = pair_lanes(jnp.broadcast_to(gch[:, None], (H_RET, 8)))
    return dm, pair_lanes(xi), pair_lanes(zeta), gc


def _rope_tables(pos):
    half = HEAD_DIM // 2
    inv_freq = ROPE_BASE ** (-jnp.arange(half, dtype=F32) / half)
    ang = pos.astype(F32)[:, None] * inv_freq[None, :]
    cos, sin = jnp.cos(ang), jnp.sin(ang)
    cos_t = jnp.concatenate([cos, cos, cos, cos], axis=-1)
    sin_t = jnp.concatenate([-sin, sin, -sin, sin], axis=-1)
    return cos_t, sin_t


def retention(proj, tables, s0_all, layer, depth, prev, ln_w, C, nb, act_dtype):
    b, l, _ = proj.shape
    n_pairs = H_RET // 2
    dm, xi, zeta, gc, cos_t, sin_t = tables
    lnw = jnp.concatenate([ln_w, ln_w]).reshape(1, PAIR)
    const3 = lambda shape: pl.BlockSpec(shape, lambda i, j: (0, 0, 0))
    st_tail = (H_RET, HEAD_DIM, HEAD_DIM)
    in_specs = [pl.BlockSpec((nb, C, RET_COLS), lambda i, j: (i, j, 0)),
                pl.BlockSpec((C, PAIR), lambda i, j: (j, 0)),
                pl.BlockSpec((C, PAIR), lambda i, j: (j, 0)),
                const3((n_pairs, 2 * C, C)),
                const3((n_pairs, C, PAIR)),
                const3((n_pairs, C, PAIR)),
                const3((n_pairs, 8, PAIR)),
                pl.BlockSpec((1, PAIR), lambda i, j: (0, 0))]
    args = [proj, cos_t, sin_t, dm, xi, zeta, gc, lnw]
    if s0_all is not None:
        in_specs.append(_state_specs(st_tail, layer, nb))
        args.append(s0_all)
    aliases = {}
    if prev is not None:
        aliases[len(args)] = 1
        in_specs.append(_ANY_SPEC)
        args.append(prev)
    return pl.pallas_call(
        functools.partial(_ret_kernel, C=C, n_pairs=n_pairs, has_state=s0_all is not None,
                          n_prev=0 if prev is None else 1, nb=nb),
        grid=(b // nb, l // C),
        in_specs=in_specs,
        out_specs=[pl.BlockSpec((nb, C, W_RET), lambda i, j: (i, j, 0)), _state_specs(st_tail, layer, nb)],
        out_shape=[jax.ShapeDtypeStruct((b, l, W_RET), act_dtype),
                   jax.ShapeDtypeStruct((depth, b) + st_tail, F32)],
        scratch_shapes=[pltpu.VMEM((nb, n_pairs, PAIR, PAIR), F32)],
        input_output_aliases=aliases,
        compiler_params=_cparams(("parallel", "arbitrary")),
        name="retention",
    )(*args)


def _tri_inverse_many(a_list, c, lo, ii, jj):
    def mm(x, y):
        return _dot(x.astype(BF16), _stack(y, lo).astype(BF16))

    base = min(c, TRI_BASE)
    lb = _log2(base)
    dblk = (ii >> lb) == (jj >> lb)
    eye = jnp.where(ii == jj, 1.0, 0.0)
    xs = [jnp.where(dblk, -a, 0.0) for a in a_list]
    ts = [eye + x for x in xs]
    for _ in range(lb - 1):
        xs = [mm(x, x) for x in xs]
        ts = [t + mm(t, x) for t, x in zip(ts, xs)]
    blk = base
    while blk < c:
        inner = (ii >> _log2(blk)) == (jj >> _log2(blk))
        outer = (ii >> _log2(2 * blk)) == (jj >> _log2(2 * blk))
        sel = outer & jnp.logical_not(inner)
        ls = [jnp.where(sel, a, 0.0) for a in a_list]
        ms = [mm(l, t) for l, t in zip(ls, ts)]
        ts = [t - mm(t, m) for t, m in zip(ts, ms)]
        blk *= 2
    return ts


def _dn_kernel(*refs, TT, c, n_pairs, has_state, n_prev, nb):
    n_in = 5 + (2 if has_state else 0)
    (proj_ref, convw_ref, alog_ref, dtb_ref, nw_ref, *st), rest = _split_refs(refs, n_in, has_state, n_prev)
    s0_ref, sconv_ref = st if has_state else (None, None)
    o_ref, sout_ref, convout_ref, S_ref, tail_ref = rest
    n2 = 2 * c
    ii = _row_iota((c, n2))
    lane2 = _lane_iota((c, n2))
    t = pl.program_id(1)

    @pl.when(t == 0)
    def _():
        tail_ref[...] = jnp.zeros_like(tail_ref)
        for i in range(nb):
            _load_pair_state(S_ref.at[i], _batch_view(s0_ref, i), n_pairs)
            if has_state:
                tail_ref[i, 8 - (CONV_W - 1):8, :] = sconv_ref[i]

    items = []
    for i in range(nb):
        items.append(_dn_prepare(_batch_view(proj_ref, i), convw_ref, alog_ref, dtb_ref, tail_ref.at[i], ii, lane2,
                                 TT=TT, c=c, n_pairs=n_pairs))
    flat = [it for batch_items in items for it in batch_items]
    tinvs = _tri_inverse_many([it['a'] for it in flat], c, lane2 < c, ii, lane2 & (c - 1))
    for it, tinv in zip(flat, tinvs):
        tb = tinv.astype(BF16)
        it['u'] = _dot(tb, it['vb'])
        it['w'] = _dot(tb, it['kbe']).astype(BF16)
    for i in range(nb):
        _dn_scan(items[i], _batch_view(proj_ref, i), nw_ref, _batch_view(o_ref, i), S_ref.at[i],
                 TT=TT, c=c, n_pairs=n_pairs)

    @pl.when(t == pl.num_programs(1) - 1)
    def _():
        for i in range(nb):
            _store_pair_state(_batch_view(sout_ref, i), S_ref.at[i], n_pairs)
            convout_ref[i] = tail_ref[i, 8 - (CONV_W - 1):8, :]


def _dn_prepare(proj_ref, convw_ref, alog_ref, dtb_ref, tail_ref, ii, lane2, *, TT, c, n_pairs):
    m0 = _head0_lanes()
    pmean = _group_mean_matrix(_block_diag_mask())
    nchunks = TT // c
    lc = _log2(c)

    x = proj_ref[0, :, 0:DN_CONV_CH]
    tail = tail_ref[...]
    rows8 = _row_iota((8, DN_CONV_CH))
    acc = x * convw_ref[CONV_W - 1:CONV_W, :]
    for s in range(1, CONV_W):
        xs = pltpu.roll(x, s, 0)
        top = jnp.where(rows8 < s, pltpu.roll(tail, s, 0), xs[0:8])
        xs = top if TT == 8 else jnp.concatenate([top, xs[8:]], axis=0)
        acc = acc + xs * convw_ref[CONV_W - 1 - s:CONV_W - s, :]
    tail_ref[...] = x[TT - 8:TT]
    qkv = _silu(acc)

    misc = proj_ref[0, :, DN_CONV_CH + W_DN:DN_COLS]
    sig = jax.nn.sigmoid(misc)
    glog = -jnp.exp(alog_ref[...]) * _softplus(misc + dtb_ref[...])
    rt = _row_iota((TT, TT))
    ct = _lane_iota((TT, TT))
    tril_t = jnp.where(((rt >> lc) == (ct >> lc)) & (ct <= rt), 1.0, 0.0).astype(BF16)
    cum_tok = _dot_exact_lhs(tril_t, glog)

    n2 = 2 * c
    jj = lane2 & (c - 1)
    lo2 = lane2 < c
    incl = jj <= ii
    strict = jj < ii
    ones_c = jnp.ones((c, PAIR), BF16)
    lane128 = _lane_iota((c, PAIR))

    items = []
    for p in range(n_pairs):
        h0, h1 = 2 * p, 2 * p + 1
        q_all = qkv[:, p * PAIR:(p + 1) * PAIR]
        k_all = qkv[:, W_DN + p * PAIR:W_DN + (p + 1) * PAIR]
        v_all = qkv[:, 2 * W_DN + p * PAIR:2 * W_DN + (p + 1) * PAIR]
        q_all = q_all * lax.rsqrt(_dot((q_all * q_all).astype(BF16), pmean) * HEAD_DIM + EPS) * HEAD_DIM ** -0.5
        k_all = k_all * lax.rsqrt(_dot((k_all * k_all).astype(BF16), pmean) * HEAD_DIM + EPS)
        for ci in range(nchunks):
            rows = slice(ci * c, (ci + 1) * c)
            q, k, v = q_all[rows], k_all[rows], v_all[rows]
            b0, b1 = sig[rows, h0:h0 + 1], sig[rows, h1:h1 + 1]
            c0 = cum_tok[rows, DN_A_LANE + h0:DN_A_LANE + h0 + 1]
            c1 = cum_tok[rows, DN_A_LANE + h1:DN_A_LANE + h1 + 1]
            beta = jnp.where(m0, jnp.broadcast_to(b0, (c, PAIR)), jnp.broadcast_to(b1, (c, PAIR)))
            cum_pair = jnp.where(m0, jnp.broadcast_to(c0, (c, PAIR)), jnp.broadcast_to(c1, (c, PAIR)))
            cum_col = cum_pair if n2 == PAIR else jnp.where(lo2, jnp.broadcast_to(c0, (c, n2)),
                                                            jnp.broadcast_to(c1, (c, n2)))
            ctok = cum_tok[rows]
            cum_sel = jnp.concatenate([jnp.where(lane128 ==DN_A_LANE + h0, ctok, 0.0),
                                       jnp.where(lane128 ==DN_A_LANE + h1, ctok, 0.0)], axis=0)
            kb = k * beta
            ecum = jnp.exp(cum_pair)
            last_row = cum_tok[(ci + 1) * c - 1:(ci + 1) * c, :]
            last = jnp.where(m0, jnp.broadcast_to(last_row[:, DN_A_LANE + h0:DN_A_LANE + h0 + 1], (1, PAIR)),
                             jnp.broadcast_to(last_row[:, DN_A_LANE + h1:DN_A_LANE + h1 + 1], (1, PAIR)))
            items.append(dict(p=p, ci=ci, cum_sel=cum_sel, cum_col=cum_col, k_st=_stack(k, m0).astype(BF16),
                              kb=kb.astype(BF16), q=q.astype(BF16),
                              vb=_stack(v * beta, m0).astype(BF16), kbe=_stack(kb * ecum, m0).astype(BF16),
                              q_dec=(q * ecum).astype(BF16), k_dec=(k * jnp.exp(last - cum_pair)).astype(BF16),
                              dec=jnp.exp(last)))
    for it in items:
        cum_row = _dot_nt_exact(ones_c, it.pop('cum_sel'))
        it['gam'] = jnp.where(incl, jnp.exp(jnp.minimum(it.pop('cum_col') - cum_row, 0.0)), 0.0)
    for it in items:
        k_st = it.pop('k_st')
        gam = it.pop('gam')
        it['a'] = jnp.where(strict, _dot_nt(it.pop('kb'), k_st) * gam, 0.0)
        it['attn'] = (_dot_nt(it.pop('q'), k_st) * gam).astype(BF16)
    return items


def _dn_scan(items, proj_ref, nw_ref, o_ref, S_ref, *, TT, c, n_pairs):
    m0 = _head0_lanes()
    bm = _block_diag_mask()
    pmean = _group_mean_matrix(bm)
    nw = nw_ref[...]
    nchunks = TT // c
    S = [S_ref[p] for p in range(n_pairs)]
    outs = [[] for _ in range(n_pairs)]
    for ci in range(nchunks):
        for p in range(n_pairs):
            it = items[p * nchunks + ci]
            Sb = S[p].astype(BF16)
            v_new = it['u'] - _dot(it['w'], Sb)
            o = _dot(it['q_dec'], Sb) + _dot(it['attn'], _stack(v_new, m0).astype(BF16))
            S[p] = S[p] * it['dec'] + jnp.where(bm, _dot_tn(it['k_dec'], v_new.astype(BF16)), 0.0)
            outs[p].append(o)
    for p in range(n_pairs):
        S_ref[p] = S[p]
        o = outs[p][0] if nchunks == 1 else jnp.concatenate(outs[p], axis=0)
        gate = proj_ref[0, :, DN_CONV_CH + p * PAIR:DN_CONV_CH + (p + 1) * PAIR]
        ms = _dot((o * o).astype(BF16), pmean)
        o_ref[0, :, p * PAIR:(p + 1) * PAIR] = (o * lax.rsqrt(ms + EPS) * nw * _silu(gate)).astype(o_ref.dtype)


def delta_net(proj, conv_w, s_conv_all, a_log, dt_bias, s0_all, layer, depth, prev, norm_w, TT, c, nb, act_dtype):
    b, l, _ = proj.shape
    n_pairs = H_DN // 2
    alog = jnp.zeros((1, PAIR), F32).at[0, DN_A_LANE:DN_A_LANE + H_DN].set(a_log)
    dtb = jnp.zeros((1, PAIR), F32).at[0, DN_A_LANE:DN_A_LANE + H_DN].set(dt_bias)
    nw = jnp.concatenate([norm_w, norm_w]).reshape(1, PAIR)
    row = lambda n: pl.BlockSpec((1, n), lambda i, j: (0, 0))
    st_tail = (H_DN, HEAD_DIM, HEAD_DIM)
    cv_tail = (CONV_W - 1, DN_CONV_CH)
    in_specs = [pl.BlockSpec((nb, TT, DN_COLS), lambda i, j: (i, j, 0)),
                pl.BlockSpec((CONV_W, DN_CONV_CH), lambda i, j: (0, 0)),
                row(PAIR), row(PAIR), row(PAIR)]
    args = [proj, conv_w, alog, dtb, nw]
    has_state = s0_all is not None
    if has_state:
        in_specs += [_state_specs(st_tail, layer, nb), _state_specs(cv_tail, layer, nb)]
        args += [s0_all, s_conv_all]
    aliases = {}
    if prev is not None:
        aliases = {len(args): 1, len(args) + 1: 2}
        in_specs += [_ANY_SPEC, _ANY_SPEC]
        args += list(prev)
    return pl.pallas_call(
        functools.partial(_dn_kernel, TT=TT, c=c, n_pairs=n_pairs, has_state=has_state,
                          n_prev=0 if prev is None else 2, nb=nb),
        grid=(b // nb, l // TT),
        in_specs=in_specs,
        out_specs=[pl.BlockSpec((nb, TT, W_DN), lambda i, j: (i, j, 0)),
                   _state_specs(st_tail, layer, nb), _state_specs(cv_tail, layer, nb)],
        out_shape=[jax.ShapeDtypeStruct((b, l, W_DN), act_dtype),
                   jax.ShapeDtypeStruct((depth, b) + st_tail, F32),
                   jax.ShapeDtypeStruct((depth, b) + cv_tail, F32)],
        scratch_shapes=[pltpu.VMEM((nb, n_pairs, PAIR, PAIR), F32), pltpu.VMEM((nb, 8, DN_CONV_CH), F32)],
        input_output_aliases=aliases,
        compiler_params=_cparams(("parallel", "arbitrary")),
        name="delta_net",
    )(*args)


def _gla_kernel(*refs, TT, m, cs, n_pairs, has_state, n_prev, nb):
    n_in = 4 + (1 if has_state else 0)
    (proj_ref, wa2_ref, ba_ref, nw_ref, *st), rest = _split_refs(refs, n_in, has_state, n_prev)
    s0_ref = st[0] if has_state else None
    o_ref, sout_ref, ST_ref = rest
    t = pl.program_id(1)

    @pl.when(t == 0)
    def _():
        for i in range(nb):
            _load_pair_state(ST_ref.at[i], _batch_view(s0_ref, i), n_pairs)
            if has_state:
                for p in range(n_pairs):
                    ST_ref[i, p] = ST_ref[i, p].T

    for i in range(nb):
        _gla_body(_batch_view(proj_ref, i), wa2_ref, ba_ref, nw_ref, _batch_view(o_ref, i), ST_ref.at[i],
                  TT=TT, m=m, cs=cs, n_pairs=n_pairs)

    @pl.when(t == pl.num_programs(1) - 1)
    def _():
        for i in range(nb):
            for p in range(n_pairs):
                s = ST_ref[i, p].T
                sout_ref[i, 2 * p] = s[0:HEAD_DIM, 0:HEAD_DIM]
                sout_ref[i, 2 * p + 1] = s[HEAD_DIM:PAIR, HEAD_DIM:PAIR]


def _gla_body(proj_ref, wa2_ref, ba_ref, nw_ref, o_ref, ST_ref, *, TT, m, cs, n_pairs):
    bm = _block_diag_mask()
    m0 = _head0_lanes()
    pmean = _group_mean_matrix(bm)
    nw = nw_ref[...]
    ng = m // cs
    n2 = 2 * m
    sh = _log2(cs)
    r2 = _row_iota((n2, n2))
    c2 = _lane_iota((n2, n2))
    incl2 = ((r2 >> sh) == (c2 >> sh)) & (c2 <= r2)
    rt = _row_iota((TT, TT))
    ct = _lane_iota((TT, TT))
    same_t = (rt >> sh) == (ct >> sh)
    tril = jnp.where(same_t & (ct <= rt), 1.0, 0.0).astype(BF16)
    ones_blk = jnp.where(same_t, 1.0, 0.0).astype(BF16)
    chunk_of_row = _row_iota((m, PAIR)) >> sh
    n_groups = TT // m

    a1 = proj_ref[0, :, 4 * W_GLA:GLA_COLS].astype(BF16)
    probs = []
    for p in range(n_pairs):
        lanes = slice(p * PAIR, (p + 1) * PAIR)
        logit = _dot(a1, wa2_ref[:, lanes]) + ba_ref[:, lanes]
        g_all = -_softplus(-logit) * (1.0 / GLA_TAU)
        cum_all = _dot_exact_lhs(tril, g_all)
        last_all = _dot_exact_lhs(ones_blk, g_all)
        q_all = proj_ref[0, :, p * PAIR:(p + 1) * PAIR] * HEAD_DIM ** -0.5
        k_all = proj_ref[0, :, W_GLA + p * PAIR:W_GLA + (p + 1) * PAIR]
        v_all = proj_ref[0, :, 2 * W_GLA + p * PAIR:2 * W_GLA + (p + 1) * PAIR]
        qt_all = q_all * jnp.exp(cum_all)
        kt_all = k_all * jnp.exp(-cum_all)
        kdec_all = (k_all * jnp.exp(last_all - cum_all)).astype(BF16)
        for gi in range(n_groups):
            rows = slice(gi * m, (gi + 1) * m)
            probs.append(dict(p=p, qt=qt_all[rows], kt=kt_all[rows], kdec=kdec_all[rows], v=v_all[rows],
                              last=last_all[rows]))
    for pr in probs:
        pr['sc'] = jnp.where(incl2, _dot_nt(_stack(pr['qt'], m0).astype(BF16), _stack(pr['kt'], m0).astype(BF16)), 0.0)
    for pr in probs:
        vb = pr['v'].astype(BF16)
        o2 = _dot(pr['sc'].astype(BF16), _stack(pr['v'], m0).astype(BF16))
        pr['o_intra'] = o2[:m] + o2[m:]
        qtb = pr['qt'].astype(BF16)
        if ng == 1:
            kd_big, pr['q_big'] = pr['kdec'], qtb
        else:
            kd_big = jnp.concatenate([jnp.where(chunk_of_row == n, pr['kdec'], 0) for n in range(ng)], axis=1)
            pr['q_big'] = jnp.concatenate([jnp.where(chunk_of_row == n, qtb, 0) for n in range(ng)], axis=1)
        pr['kvt'] = _dot_tn(vb, kd_big)
    ST = [ST_ref[p] for p in range(n_pairs)]
    for gi in range(n_groups):
        for p in range(n_pairs):
            pr = probs[p * n_groups + gi]
            sts = []
            for n in range(ng):
                sts.append(ST[p].astype(BF16))
                dec = jnp.exp(pr['last'][n * cs:n * cs + 1, :])
                ST[p] = ST[p] * dec + jnp.where(bm, pr['kvt'][:, n * PAIR:(n + 1) * PAIR], 0.0)
            pr['st_all'] = sts[0] if ng == 1 else jnp.concatenate(sts, axis=1)
    for pr in probs:
        pr['o'] = pr['o_intra'] + _dot_nt(pr['q_big'], pr['st_all'])
    for p in range(n_pairs):
        ST_ref[p] = ST[p]
        outs = [probs[p * n_groups + gi]['o'] for gi in range(n_groups)]
        o = outs[0] if n_groups == 1 else jnp.concatenate(outs, axis=0)
        gate = proj_ref[0, :, 3 * W_GLA + p * PAIR:3 * W_GLA + (p + 1) * PAIR]
        ms = _dot((o * o).astype(BF16), pmean)
        o_ref[0, :, p * PAIR:(p + 1) * PAIR] = (o * lax.rsqrt(ms + EPS) * nw * _silu(gate)).astype(o_ref.dtype)


def gla(proj, w_a2, b_a, s0_all, layer, depth, prev, norm_w, TT, m, cs, nb, act_dtype):
    b, l, _ = proj.shape
    n_pairs = H_GLA // 2
    wa2 = jnp.zeros((PAIR, W_GLA), F32).at[:GLA_RANK].set(w_a2).astype(BF16)
    nw = jnp.concatenate([norm_w, norm_w]).reshape(1, PAIR)
    st_tail = (H_GLA, HEAD_DIM, HEAD_DIM)
    in_specs = [pl.BlockSpec((nb, TT, GLA_COLS), lambda i, j: (i, j, 0)),
                pl.BlockSpec((PAIR, W_GLA), lambda i, j: (0, 0)),
                pl.BlockSpec((1, W_GLA), lambda i, j: (0, 0)),
                pl.BlockSpec((1, PAIR), lambda i, j: (0, 0))]
    args = [proj, wa2, b_a.reshape(1, W_GLA), nw]
    if s0_all is not None:
        in_specs.append(_state_specs(st_tail, layer, nb))
        args.append(s0_all)
    aliases = {}
    if prev is not None:
        aliases[len(args)] = 1
        in_specs.append(_ANY_SPEC)
        args.append(prev)
    return pl.pallas_call(
        functools.partial(_gla_kernel, TT=TT, m=m, cs=cs, n_pairs=n_pairs, has_state=s0_all is not None,
                          n_prev=0 if prev is None else 1, nb=nb),
        grid=(b // nb, l // TT),
        in_specs=in_specs,
        out_specs=[pl.BlockSpec((nb, TT, W_GLA), lambda i, j: (i, j, 0)), _state_specs(st_tail, layer, nb)],
        out_shape=[jax.ShapeDtypeStruct((b, l, W_GLA), act_dtype),
                   jax.ShapeDtypeStruct((depth, b) + st_tail, F32)],
        scratch_shapes=[pltpu.VMEM((nb, n_pairs, PAIR, PAIR), F32)],
        input_output_aliases=aliases,
        compiler_params=_cparams(("parallel", "arbitrary")),
        name="gla",
    )(*args)


def _proj_t_kernel(x_ref, g_ref, w_ref, o_ref, xn_ref):
    @pl.when(pl.program_id(0) == 0)
    def _():
        x = x_ref[...]
        xn = x * lax.rsqrt(jnp.mean(x * x, axis=0, keepdims=True) + EPS) * g_ref[...]
        xn_ref[...] = xn.astype(BF16)

    o_ref[...] = _dot(w_ref[...], xn_ref[...])


def proj_t(x_t, g, w_t, tn):
    d, t = x_t.shape
    n = w_t.shape[0]
    return pl.pallas_call(
        _proj_t_kernel,
        grid=(n // tn,),
        in_specs=[pl.BlockSpec((d, t), lambda i: (0, 0)),
                  pl.BlockSpec((d, 1), lambda i: (0, 0)),
                  pl.BlockSpec((tn, d), lambda i: (i, 0))],
        out_specs=pl.BlockSpec((tn, t), lambda i: (i, 0)),
        out_shape=jax.ShapeDtypeStruct((n, t), F32),
        scratch_shapes=[pltpu.VMEM((d, t), BF16)],
        compiler_params=_cparams(("arbitrary",)),
        name="proj_t",
    )(x_t, g.reshape(d, 1), w_t)


SUBLANES = 8


def _row_group(ref, g, lanes):
    return ref[pl.ds(pl.multiple_of(g * SUBLANES, SUBLANES), SUBLANES), lanes]


def _pick_row(block, r):
    return jnp.sum(jnp.where(_row_iota(block.shape) == r, block, 0.0), axis=0, keepdims=True)


def _lane_linear_kernel(*refs, n_tok, nbatch, rotary, layer_norm, row_decay, n_prev):
    if rotary:
        q_ref, k_ref, v_ref, gate_ref, cos_ref, sin_ref, dec_ref, nw_ref, s0_ref = refs[:9]
        rest = refs[9 + n_prev:]
    else:
        q_ref, k_ref, v_ref, gate_ref, a1_ref, wa2_ref, ba_ref, nw_ref, s0_ref = refs[:9]
        rest = refs[9 + n_prev:]
    o_ref, sout_ref, q_scr, k_scr, dec_scr = rest
    if rotary:
        half = HEAD_DIM // 2

        def rot(x):
            sw = jnp.concatenate([x[half:], x[:half]], axis=0)
            return x * cos_ref[...] + sw * sin_ref[...]

        q_scr[...] = rot(q_ref[...])
        k_scr[...] = rot(k_ref[...]) * HEAD_DIM ** -0.5
    else:
        q_scr[...] = q_ref[...] * HEAD_DIM ** -0.5
        k_scr[...] = k_ref[...]
        logit = _dot(wa2_ref[...], a1_ref[...].astype(BF16)) + ba_ref[...]
        dec_scr[...] = jnp.exp(-_softplus(-logit) * (1.0 / GLA_TAU))
    nw = nw_ref[...]
    for t in range(n_tok):
        lanes = slice(t * nbatch, (t + 1) * nbatch)
        v_t = v_ref[:, lanes]
        src = s0_ref if t == 0 else sout_ref

        def body(g, o, src=src, lanes=lanes, v_t=v_t):
            kg = _row_group(k_scr, g, lanes)
            qg = _row_group(q_scr, g, lanes)
            dg = _row_group(dec_scr, g, lanes) if row_decay else None
            for i in range(SUBLANES):
                d = g * SUBLANES + i
                dec = dg[i:i + 1] if row_decay else dec_ref[...]
                s_d = src[d] * dec + kg[i:i + 1] * v_t
                sout_ref[d] = s_d
                o = o + qg[i:i + 1] * s_d
            return o

        o = lax.fori_loop(0, HEAD_DIM // SUBLANES, body, jnp.zeros((HEAD_DIM, nbatch), F32))
        if layer_norm:
            o = o - jnp.mean(o, axis=0, keepdims=True)
        o = o * lax.rsqrt(jnp.mean(o * o, axis=0, keepdims=True) + EPS) * nw
        o_ref[:, lanes] = o * _silu(gate_ref[:, lanes])


def _head_rows(width, col0):
    return pl.BlockSpec((HEAD_DIM, width), lambda h: (col0 // HEAD_DIM + h, 0))


def lane_linear_mixer(proj_t_all, col0, width, n_heads, kind, consts, s_all, layer, depth, prev, n_tok, nbatch):
    t_all = n_tok * nbatch
    rotary = kind == 'ret'
    hr = lambda off: _head_rows(t_all, col0 + off)
    state_spec = pl.BlockSpec((None, None, HEAD_DIM, HEAD_DIM, nbatch), lambda h: (layer, h, 0, 0, 0))
    in_specs = [hr(0), hr(width), hr(2 * width), hr(3 * width)]
    args = [proj_t_all] * 4
    if rotary:
        cos_t, sin_t, gam, nw = consts
        in_specs += [pl.BlockSpec((HEAD_DIM, t_all), lambda h: (0, 0)),
                     pl.BlockSpec((HEAD_DIM, t_all), lambda h: (0, 0)),
                     pl.BlockSpec((None, 1, nbatch), lambda h: (h, 0, 0)),
                     pl.BlockSpec((HEAD_DIM, nbatch), lambda h: (0, 0))]
        args += [cos_t, sin_t, gam, nw]
    else:
        wa2_t, ba_t, nw = consts
        in_specs += [pl.BlockSpec((LANES, t_all), lambda h: ((col0 + 4 * width) // LANES, 0)),
                     pl.BlockSpec((HEAD_DIM, LANES), lambda h: (h, 0)),
                     pl.BlockSpec((HEAD_DIM, 1), lambda h: (h, 0)),
                     pl.BlockSpec((HEAD_DIM, nbatch), lambda h: (0, 0))]
        args += [proj_t_all, wa2_t, ba_t, nw]
    in_specs.append(state_spec)
    args.append(s_all)
    aliases = {}
    if prev is not None:
        aliases[len(args)] = 1
        in_specs.append(_ANY_SPEC)
        args.append(prev)
    return pl.pallas_call(
        functools.partial(_lane_linear_kernel, n_tok=n_tok, nbatch=nbatch, rotary=rotary, layer_norm=rotary,
                          row_decay=not rotary, n_prev=0 if prev is None else 1),
        grid=(n_heads,),
        in_specs=in_specs,
        out_specs=[pl.BlockSpec((HEAD_DIM, t_all), lambda h: (h, 0)), state_spec],
        out_shape=[jax.ShapeDtypeStruct((width, t_all), F32),
                   jax.ShapeDtypeStruct((depth, n_heads, HEAD_DIM, HEAD_DIM, nbatch), F32)],
        scratch_shapes=[pltpu.VMEM((HEAD_DIM, t_all), F32)] * 3,
        input_output_aliases=aliases,
        compiler_params=_cparams(("arbitrary",)),
        name="lane_" + kind,
    )(*args)


def _lane_delta_kernel(*refs, n_tok, nbatch, n_prev):
    (qp_ref, kp_ref, vp_ref, gate_ref, misc_ref, wq_ref, wk_ref, wv_ref, cq_ref, ck_ref, cv_ref,
     alog_ref, dtb_ref, nw_ref, s0_ref) = refs[:15]
    o_ref, sout_ref, cqo_ref, cko_ref, cvo_ref, q_scr, k_scr = refs[15 + n_prev:]
    h = pl.program_id(0)

    def conv(x_ref, w_ref, c_ref, cout_ref):
        xs = [c_ref[j] for j in range(CONV_W - 1)] + [x_ref[:, t * nbatch:(t + 1) * nbatch] for t in range(n_tok)]
        outs = []
        for t in range(n_tok):
            acc = xs[t] * w_ref[0]
            for j in range(1, CONV_W):
                acc = acc + xs[t + j] * w_ref[j]
            outs.append(_silu(acc))
        for j in range(CONV_W - 1):
            cout_ref[j] = xs[n_tok + j]
        return jnp.concatenate(outs, axis=1)

    q = conv(qp_ref, wq_ref, cq_ref, cqo_ref)
    k = conv(kp_ref, wk_ref, ck_ref, cko_ref)
    v = conv(vp_ref, wv_ref, cv_ref, cvo_ref)
    q_scr[...] = q * lax.rsqrt(jnp.sum(q * q, axis=0, keepdims=True) + EPS) * HEAD_DIM ** -0.5
    k_scr[...] = k * lax.rsqrt(jnp.sum(k * k, axis=0, keepdims=True) + EPS)
    misc = misc_ref[0:2 * SUBLANES, :]
    beta = jax.nn.sigmoid(_pick_row(misc, h))
    glog = -jnp.exp(alog_ref[...]) * _softplus(_pick_row(misc, DN_A_LANE + h) + dtb_ref[...])
    a_all = jnp.exp(glog)
    nw = nw_ref[...]
    zero = jnp.zeros((HEAD_DIM, nbatch), F32)
    for t in range(n_tok):
        lanes = slice(t * nbatch, (t + 1) * nbatch)
        src = s0_ref if t == 0 else sout_ref
        a_t = a_all[:, lanes]

        def kts(g, r, src=src, lanes=lanes):
            kg = _row_group(k_scr, g, lanes)
            for i in range(SUBLANES):
                r = r + kg[i:i + 1] * src[g * SUBLANES + i]
            return r

        r = lax.fori_loop(0, HEAD_DIM // SUBLANES, kts, zero)
        w = beta[:, lanes] * (v[:, lanes] - a_t * r)

        def body(g, o, src=src, lanes=lanes, a_t=a_t, w=w):
            kg = _row_group(k_scr, g, lanes)
            qg = _row_group(q_scr, g, lanes)
            for i in range(SUBLANES):
                d = g * SUBLANES + i
                s_d = src[d] * a_t + kg[i:i + 1] * w
                sout_ref[d] = s_d
                o = o + qg[i:i + 1] * s_d
            return o

        o = lax.fori_loop(0, HEAD_DIM // SUBLANES, body, zero)
        o = o * lax.rsqrt(jnp.mean(o * o, axis=0, keepdims=True) + EPS) * nw
        o_ref[:, lanes] = o * _silu(gate_ref[:, lanes])


def lane_delta_net(proj_t_all, col0, conv_w_b, conv_parts, alog_b, dtb_b, nw, s_all, layer, depth, prev, n_tok, nbatch):
    t_all = n_tok * nbatch
    hr = lambda off: _head_rows(t_all, col0 + off)
    state_spec = pl.BlockSpec((None, None, HEAD_DIM, HEAD_DIM, nbatch), lambda h: (layer, h, 0, 0, 0))
    tap_spec = lambda i: pl.BlockSpec((None, CONV_W, HEAD_DIM, nbatch), lambda h, i=i: (i, 0, h, 0))
    cst_spec = pl.BlockSpec((None, CONV_W - 1, HEAD_DIM, nbatch), lambda h: (layer, 0, h, 0))
    head_row = pl.BlockSpec((None, 1, 1), lambda h: (h, 0, 0))
    in_specs = [hr(0), hr(W_DN), hr(2 * W_DN), hr(DN_CONV_CH),
                pl.BlockSpec((LANES, t_all), lambda h: ((col0 + DN_CONV_CH + W_DN) // LANES, 0)),
                tap_spec(0), tap_spec(1), tap_spec(2), cst_spec, cst_spec, cst_spec,
                head_row, head_row, pl.BlockSpec((HEAD_DIM, nbatch), lambda h: (0, 0)), state_spec]
    args = [proj_t_all] * 5 + [conv_w_b] * 3 + list(conv_parts) + [alog_b, dtb_b, nw, s_all]
    aliases = {}
    if prev is not None:
        aliases = {len(args) + i: 1 + i for i in range(4)}
        in_specs += [_ANY_SPEC] * 4
        args += list(prev)
    cshape = jax.ShapeDtypeStruct((depth, CONV_W - 1, W_DN, nbatch), F32)
    return pl.pallas_call(
        functools.partial(_lane_delta_kernel, n_tok=n_tok, nbatch=nbatch, n_prev=0 if prev is None else 4),
        grid=(H_DN,),
        in_specs=in_specs,
        out_specs=[pl.BlockSpec((HEAD_DIM, t_all), lambda h: (h, 0)), state_spec, cst_spec, cst_spec, cst_spec],
        out_shape=[jax.ShapeDtypeStruct((W_DN, t_all), F32),
                   jax.ShapeDtypeStruct((depth, H_DN, HEAD_DIM, HEAD_DIM, nbatch), F32), cshape, cshape, cshape],
        scratch_shapes=[pltpu.VMEM((HEAD_DIM, t_all), F32)] * 2,
        input_output_aliases=aliases,
        compiler_params=_cparams(("arbitrary",)),
        name="lane_delta",
    )(*args)


def _prep_layer_weights(li, w_in_t, w_out, w_mq, w_mo, w_ff1, w_ff2):
    wi = w_in_t[li]
    d = wi.shape[1]
    c0 = RET_COLS
    c1 = c0 + DN_CONV_CH + W_DN + 2 * H_DN
    w_ret = wi[:c0]
    w_dn = jnp.concatenate([wi[c0:c1], jnp.zeros((DN_COLS - (c1 - c0), d), F32)], axis=0)
    w_gla = jnp.concatenate([wi[c1:], jnp.zeros((GLA_COLS - (wi.shape[0] - c1), d), F32)], axis=0)
    cast = lambda a: a.astype(BF16)
    return dict(w_all_t=cast(jnp.concatenate([w_ret, w_dn, w_gla], axis=0)), w_out=cast(w_out[li]),
                w_mq=w_mq[li], w_mo=cast(w_mo[li]), w_ff1=cast(w_ff1[li]), w_ff2=cast(w_ff2[li]))


def _run_trunk(x, pos, s_ret, s_dn, s_conv, s_gla, mem_k, mem_v, mem_split, lw, p, cfg):
    b, l, d = x.shape
    t = b * l
    depth = len(lw)
    xf = x.reshape(t, d)
    lane_batch = cfg['lane_batch']
    act = cfg['act_dtype']
    if lane_batch:
        lane = _lane_setup(pos, b, s_ret, s_dn, s_conv, s_gla)
    else:
        ret_tables = _ret_consts(cfg['ret_c']) + _rope_tables(pos)
    r_s = d_s = g_s = None
    for li in range(depth):
        w = lw[li]
        g = p['norm_mix'][li]
        if lane_batch:
            o_mix, r_s, d_s, g_s = _lane_mixers(xf, b, l, li, depth, w, p, lane, r_s, d_s, g_s)
            xf = matmul_res([o_mix], w['w_out'], xf, cfg['tm'])
        else:
            pr, pd, pg = in_proj(xf, g, w['w_all_t'], (RET_COLS, DN_COLS, GLA_COLS), cfg['tm'])
            pr, pd, pg = pr.reshape(b, l, RET_COLS), pd.reshape(b, l, DN_COLS), pg.reshape(b, l, GLA_COLS)
            o_ret, r_s = retention(pr, ret_tables, s_ret, li, depth, r_s, p['ret_norm_w'][li], cfg['ret_c'],
                                   cfg['nb'], act)
            o_dn, *d_s = delta_net(pd, p['dn_conv_w'][li], s_conv, p['dn_a_log'][li], p['dn_dt_bias'][li], s_dn,
                                   li, depth, d_s, p['dn_norm_w'][li], cfg['dn_tt'], cfg['dn_c'], cfg['nb'], act)
            o_gla, g_s = gla(pg, p['gla_w_a2'][li], p['gla_b_a'][li], s_gla, li, depth, g_s, p['gla_norm_w'][li],
                             cfg['gla_tt'], cfg['gla_m'], cfg['gla_cs'], cfg['nb'], act)
            xf = matmul_res([o_ret.reshape(t, W_RET), o_dn.reshape(t, W_DN), o_gla.reshape(t, W_GLA)],
                            w['w_out'], xf, cfg['tm'])
        qm = rms_matmul(xf, p['norm_cross'][li], w['w_mq'], cfg['tm'], act).reshape(b, l, d)
        if mem_split:
            att = mem_attention(qm, mem_k, mem_v, cfg['tq'], cfg['attn_nb'], layer=li)
        else:
            att = mem_attention(qm, mem_k[li], mem_v[li], cfg['tq'], cfg['attn_nb'])
        xf = ffn(xf, att.reshape(t, d), w['w_mo'], p['norm_ffn'][li], w['w_ff1'], w['w_ff2'], p['norm_final'],
                 li == depth - 1, cfg['ffn_tm'], cfg['ffn_tf'])
    if lane_batch:
        to_batch_major = lambda s: jnp.transpose(s, (0, 4, 1, 2, 3))
        conv = jnp.transpose(jnp.concatenate(d_s[1:], axis=2), (0, 3, 1, 2))
        return (xf.reshape(b, l, d), to_batch_major(r_s), to_batch_major(d_s[0]), conv, to_batch_major(g_s))
    return (xf.reshape(b, l, d), r_s, d_s[0], d_s[1], g_s)


def _lane_setup(pos, b, s_ret, s_dn, s_conv, s_gla):
    half = HEAD_DIM // 2
    inv_freq = ROPE_BASE ** (-jnp.arange(half, dtype=F32) / half)
    ang = pos.astype(F32)[:, None] * inv_freq[None, :]
    cos, sin = jnp.cos(ang), jnp.sin(ang)
    spread = lambda a: jnp.repeat(a.T, b, axis=1)
    lg = jnp.log1p(-jnp.exp2(-5.0 - jnp.arange(H_RET, dtype=F32)))
    to_lane = lambda s: jnp.transpose(s, (0, 2, 3, 4, 1))
    conv_t = jnp.transpose(s_conv, (0, 2, 3, 1))
    return dict(cos=spread(jnp.concatenate([cos, cos], axis=1)), sin=spread(jnp.concatenate([-sin, sin], axis=1)),
                gam=jnp.broadcast_to(jnp.exp(lg)[:, None, None], (H_RET, 1, b)),
                s_ret=to_lane(s_ret), s_dn=to_lane(s_dn), s_gla=to_lane(s_gla),
                conv=[conv_t[:, :, i * W_DN:(i + 1) * W_DN, :] for i in range(3)])


def _lane_mixers(xf, b, l, li, depth, w, p, lane, r_s, d_s, g_s):
    d = xf.shape[1]
    col = lambda v: jnp.broadcast_to(v[:, None], (v.shape[0], b))
    x_t = xf.reshape(b, l, d).transpose(2, 1, 0).reshape(d, l * b)
    pt = proj_t(x_t, p['norm_mix'][li], w['w_all_t'], 256)
    o_ret, r_s = lane_linear_mixer(pt, 0, W_RET, H_RET, 'ret',
                                   (lane['cos'], lane['sin'], lane['gam'], col(p['ret_norm_w'][li])),
                                   lane['s_ret'], li, depth, r_s, l, b)
    taps = p['dn_conv_w'][li].reshape(CONV_W, 3, W_DN).transpose(1, 0, 2)
    taps = jnp.broadcast_to(taps[..., None], taps.shape + (b,))
    o_dn, *d_s = lane_delta_net(pt, RET_COLS, taps, lane['conv'], p['dn_a_log'][li].reshape(H_DN, 1, 1),
                                p['dn_dt_bias'][li].reshape(H_DN, 1, 1), col(p['dn_norm_w'][li]),
                                lane['s_dn'], li, depth, d_s, l, b)
    wa2_t = jnp.zeros((W_GLA, LANES), F32).at[:, :GLA_RANK].set(p['gla_w_a2'][li].T).astype(BF16)
    o_gla, g_s = lane_linear_mixer(pt, RET_COLS + DN_COLS, W_GLA, H_GLA, 'gla',
                                   (wa2_t, p['gla_b_a'][li].reshape(W_GLA, 1), col(p['gla_norm_w'][li])),
                                   lane['s_gla'], li, depth, g_s, l, b)
    o_t = jnp.concatenate([o_ret, o_dn, o_gla], axis=0)
    o = o_t.reshape(o_t.shape[0], l, b).transpose(2, 1, 0).reshape(b * l, o_t.shape[0])
    return o, r_s, d_s, g_s


def kernel(x_prompt, x_sample, state_ret, state_dn, state_dn_conv, state_gla, cache_mem_k, cache_mem_v, mem_prompt, norm_mix, w_in, ret_norm_w, dn_conv_w, dn_a_log, dn_dt_bias, dn_norm_w, gla_w_a2, gla_b_a, gla_norm_w, w_out, norm_cross, norm_mem, w_mq, w_mk, w_mv, w_mo, norm_ffn, w_ff1, w_ff2, norm_final):
    b_p, l_p, d = x_prompt.shape
    b_s, l_s, _ = x_sample.shape
    depth = w_in.shape[0]
    n_mem = mem_prompt.shape[1]
    p = dict(norm_mix=norm_mix, ret_norm_w=ret_norm_w, dn_conv_w=dn_conv_w, dn_a_log=dn_a_log,
             dn_dt_bias=dn_dt_bias, dn_norm_w=dn_norm_w, gla_w_a2=gla_w_a2, gla_b_a=gla_b_a,
             gla_norm_w=gla_norm_w, norm_cross=norm_cross, norm_ffn=norm_ffn, norm_final=norm_final)
    w_in_t = jnp.transpose(w_in, (0, 2, 1))
    lw = [_prep_layer_weights(li, w_in_t, w_out, w_mq, w_mo, w_ff1, w_ff2) for li in range(depth)]

    memf = mem_prompt.reshape(b_p * n_mem, d)
    mk = [rms_matmul(memf, norm_mem[li], w_mk[li], 512).reshape(b_p, n_mem, d) for li in range(depth)]
    mv = [rms_matmul(memf, norm_mem[li], w_mv[li], 512).reshape(b_p, n_mem, d) for li in range(depth)]

    cfg_p = dict(tm=512, ret_c=256, dn_tt=256, dn_c=math.gcd(l_p, DN_CHUNK), gla_tt=256, gla_m=64,
                 gla_cs=math.gcd(l_p, GLA_CHUNK), tq=512, attn_nb=1, ffn_tm=1024, ffn_tf=1024, nb=1, act_dtype=BF16,
                 lane_batch=False)
    y_p, p_ret, p_dn, p_conv, p_gla = _run_trunk(
        x_prompt, jnp.arange(l_p, dtype=jnp.int32), None, None, None, None, mk, mv, False, lw, p, cfg_p)

    cfg_s = dict(tm=512, ret_c=l_s, dn_tt=l_s, dn_c=math.gcd(l_s, DN_CHUNK), gla_tt=l_s,
                 gla_m=math.gcd(l_s, GLA_CHUNK), gla_cs=math.gcd(l_s, GLA_CHUNK), tq=l_s, ffn_tm=1024, ffn_tf=512,
                 attn_nb=2, nb=8, act_dtype=F32, lane_batch=True)
    y_s, s_ret, s_dn, s_conv, s_gla = _run_trunk(
        x_sample, PAST_LEN + jnp.arange(l_s, dtype=jnp.int32), state_ret, state_dn, state_dn_conv, state_gla,
        cache_row_view(cache_mem_k), cache_row_view(cache_mem_v), True, lw, p, cfg_s)

    mem_shape = (depth, b_p, n_mem, MEM_HEADS, d // MEM_HEADS)
    return (y_p, y_s, p_ret, p_dn, p_conv, p_gla, jnp.stack(mk).reshape(mem_shape), jnp.stack(mv).reshape(mem_shape),
            s_ret, s_dn, s_conv, s_gla)
```

```python
import functools
import math

import jax
import jax.numpy as jnp
from jax import lax
from jax.experimental import pallas as pl
from jax.experimental.pallas import tpu as pltpu

F32 = jnp.float32
BF16 = jnp.bfloat16

LANES = 128
HEAD_DIM = 64
PAIR = 2 * HEAD_DIM
H_RET, H_DN, H_GLA = 6, 6, 4
W_RET, W_DN, W_GLA = H_RET * HEAD_DIM, H_DN * HEAD_DIM, H_GLA * HEAD_DIM
CONV_W = 4
DN_CONV_CH = 3 * W_DN
GLA_RANK = 16
GLA_TAU = 16.0
MEM_HEADS = 4
DN_CHUNK = 64
GLA_CHUNK = 16
ROPE_BASE = 10000.0
EPS = 1e-6
PAST_LEN = 16384

DN_COLS = DN_CONV_CH + W_DN + PAIR
GLA_COLS = 4 * W_GLA + PAIR
RET_COLS = 4 * W_RET
DN_A_LANE = H_DN
TRI_BASE = 16

VMEM_LIMIT = 56 * 1024 * 1024


def _cparams(sem):
    return pltpu.CompilerParams(dimension_semantics=sem, vmem_limit_bytes=VMEM_LIMIT)


def _log2(n):
    l = int(math.log2(n))
    assert (1 << l) == n, n
    return l


def _dot(a, b):
    return jnp.dot(a, b, preferred_element_type=F32)


def _dot_nt(a, b):
    return lax.dot_general(a, b, (((1,), (1,)), ((), ())), preferred_element_type=F32)


def _dot_tn(a, b):
    return lax.dot_general(a, b, (((0,), (0,)), ((), ())), preferred_element_type=F32)


def _split2(a):
    hi = a.astype(BF16)
    lo = (a - hi.astype(F32)).astype(BF16)
    return hi, lo


def _split3(a):
    hi = a.astype(BF16)
    r = a - hi.astype(F32)
    mid = r.astype(BF16)
    lo = (r - mid.astype(F32)).astype(BF16)
    return hi, mid, lo


def _dot_exact_lhs(a_bf16, b):
    hi, mid, lo = _split3(b)
    return _dot(a_bf16, hi) + _dot(a_bf16, mid) + _dot(a_bf16, lo)


def _dot_nt_exact(a_bf16, b):
    hi, mid, lo = _split3(b)
    return _dot_nt(a_bf16, hi) + _dot_nt(a_bf16, mid) + _dot_nt(a_bf16, lo)


def _silu(x):
    return x * jax.nn.sigmoid(x)


def _softplus(x):
    return jnp.maximum(x, 0.0) + jnp.log1p(jnp.exp(-jnp.abs(x)))


def _lane_iota(shape):
    return lax.broadcasted_iota(jnp.int32, shape, len(shape) - 1)


def _row_iota(shape):
    return lax.broadcasted_iota(jnp.int32, shape, 0)


def _head0_lanes():
    return _lane_iota((1, PAIR)) < HEAD_DIM


def _stack(x, lo):
    return jnp.concatenate([jnp.where(lo, x, 0.0), jnp.where(lo, 0.0, x)], axis=0)


def _block_diag_mask():
    r = _row_iota((PAIR, PAIR))
    c = _lane_iota((PAIR, PAIR))
    return (r >> 6) == (c >> 6)


def _group_mean_matrix(bm):
    return jnp.where(bm, 1.0 / HEAD_DIM, 0.0).astype(BF16)


def _load_pair_state(S_ref, s0_ref, n_pairs):
    S_ref[...] = jnp.zeros_like(S_ref)
    if s0_ref is not None:
        for p in range(n_pairs):
            S_ref[p, 0:HEAD_DIM, 0:HEAD_DIM] = s0_ref[0, 2 * p]
            S_ref[p, HEAD_DIM:PAIR, HEAD_DIM:PAIR] = s0_ref[0, 2 * p + 1]


def _store_pair_state(sout_ref, S_ref, n_pairs):
    for p in range(n_pairs):
        sout_ref[0, 2 * p] = S_ref[p, 0:HEAD_DIM, 0:HEAD_DIM]
        sout_ref[0, 2 * p + 1] = S_ref[p, HEAD_DIM:PAIR, HEAD_DIM:PAIR]


def _split_refs(refs, n_in, has_state, n_prev):
    ins = refs[:n_in]
    refs = refs[n_in:]
    return ins, refs[n_prev:]


def _rms_matmul_kernel(x_ref, g_ref, w_ref, o_ref, wb_ref, *, w_transposed):
    @pl.when(pl.program_id(0) == 0)
    def _():
        wb_ref[...] = w_ref[...].astype(BF16)

    x = x_ref[...]
    xn = (x * lax.rsqrt(jnp.mean(x * x, axis=-1, keepdims=True) + EPS) * g_ref[...]).astype(BF16)
    y = _dot_nt(xn, wb_ref[...]) if w_transposed else _dot(xn, wb_ref[...])
    o_ref[...] = y.astype(o_ref.dtype)


def rms_matmul(x, g, w, tm, out_dtype=F32, w_transposed=False):
    t, d = x.shape
    n = w.shape[0] if w_transposed else w.shape[1]
    return pl.pallas_call(
        functools.partial(_rms_matmul_kernel, w_transposed=w_transposed),
        grid=(t // tm,),
        in_specs=[pl.BlockSpec((tm, d), lambda i: (i, 0)),
                  pl.BlockSpec((1, d), lambda i: (0, 0)),
                  pl.BlockSpec(w.shape, lambda i: (0, 0))],
        out_specs=pl.BlockSpec((tm, n), lambda i: (i, 0)),
        out_shape=jax.ShapeDtypeStruct((t, n), out_dtype),
        scratch_shapes=[pltpu.VMEM(w.shape, BF16)],
        compiler_params=_cparams(("arbitrary",)),
        name="rms_matmul",
    )(x, g.reshape(1, d), w)


def _in_proj_kernel(x_ref, g_ref, wt_ref, *o_refs):
    x = x_ref[...]
    xn = (x * lax.rsqrt(jnp.mean(x * x, axis=-1, keepdims=True) + EPS) * g_ref[...]).astype(BF16)
    off = 0
    for o_ref in o_refs:
        n = o_ref.shape[1]
        o_ref[...] = _dot_nt(xn, wt_ref[off:off + n, :])
        off += n


def in_proj(x, g, wt_bf16, widths, tm):
    t, d = x.shape
    assert sum(widths) == wt_bf16.shape[0]
    return pl.pallas_call(
        _in_proj_kernel,
        grid=(t // tm,),
        in_specs=[pl.BlockSpec((tm, d), lambda i: (i, 0)),
                  pl.BlockSpec((1, d), lambda i: (0, 0)),
                  pl.BlockSpec(wt_bf16.shape, lambda i: (0, 0))],
        out_specs=[pl.BlockSpec((tm, n), lambda i: (i, 0)) for n in widths],
        out_shape=[jax.ShapeDtypeStruct((t, n), F32) for n in widths],
        compiler_params=_cparams(("parallel",)),
        name="in_proj",
    )(x, g.reshape(1, d), wt_bf16)


def _matmul_res_kernel(*refs, widths):
    n_in = len(widths)
    a_refs = refs[:n_in]
    w_ref, r_ref, o_ref = refs[n_in:]
    acc = r_ref[...]
    off = 0
    for a_ref, wk in zip(a_refs, widths):
        acc = acc + _dot(a_ref[...].astype(BF16), w_ref[off:off + wk, :])
        off += wk
    o_ref[...] = acc


def _res_then_proj_kernel(*refs, widths):
    n_in = len(widths)
    a_refs = refs[:n_in]
    w_ref, r_ref, g_ref, wq_ref, x_ref, q_ref = refs[n_in:]
    acc = r_ref[...]
    off = 0
    for a_ref, wk in zip(a_refs, widths):
        acc = acc + _dot(a_ref[...].astype(BF16), w_ref[off:off + wk, :])
        off += wk
    x_ref[...] = acc
    xn = acc * lax.rsqrt(jnp.mean(acc * acc, axis=-1, keepdims=True) + EPS) * g_ref[...]
    q_ref[...] = _dot(xn.astype(BF16), wq_ref[...]).astype(q_ref.dtype)


def res_then_proj(a_list, w_bf16, res, g, wq_bf16, tm, q_dtype):
    t, n = res.shape
    widths = tuple(a.shape[1] for a in a_list)
    k = w_bf16.shape[0]
    nq = wq_bf16.shape[1]
    assert sum(widths) == k
    in_specs = [pl.BlockSpec((tm, wk), lambda i: (i, 0)) for wk in widths]
    in_specs += [pl.BlockSpec((k, n), lambda i: (0, 0)), pl.BlockSpec((tm, n), lambda i: (i, 0)),
                 pl.BlockSpec((1, n), lambda i: (0, 0)), pl.BlockSpec((n, nq), lambda i: (0, 0))]
    return pl.pallas_call(
        functools.partial(_res_then_proj_kernel, widths=widths),
        grid=(t // tm,),
        in_specs=in_specs,
        out_specs=[pl.BlockSpec((tm, n), lambda i: (i, 0)), pl.BlockSpec((tm, nq), lambda i: (i, 0))],
        out_shape=[jax.ShapeDtypeStruct((t, n), F32), jax.ShapeDtypeStruct((t, nq), q_dtype)],
        compiler_params=_cparams(("parallel",)),
        name="res_then_proj",
    )(*a_list, w_bf16, res, g.reshape(1, n), wq_bf16)


def matmul_res(a_list, w_bf16, res, tm):
    t, n = res.shape
    widths = tuple(a.shape[1] for a in a_list)
    k = w_bf16.shape[0]
    assert sum(widths) == k
    in_specs = [pl.BlockSpec((tm, wk), lambda i: (i, 0)) for wk in widths]
    in_specs += [pl.BlockSpec((k, n), lambda i: (0, 0)), pl.BlockSpec((tm, n), lambda i: (i, 0))]
    return pl.pallas_call(
        functools.partial(_matmul_res_kernel, widths=widths),
        grid=(t // tm,),
        in_specs=in_specs,
        out_specs=pl.BlockSpec((tm, n), lambda i: (i, 0)),
        out_shape=jax.ShapeDtypeStruct((t, n), F32),
        compiler_params=_cparams(("parallel",)),
        name="matmul_res",
    )(*a_list, w_bf16, res)


def _ffn_kernel(x_ref, a_ref, wo_ref, g_ref, w1_ref, w2_ref, gf_ref, o_ref, xn_ref, acc_ref, *, final_norm):
    j = pl.program_id(1)

    @pl.when(j == 0)
    def _():
        x = x_ref[...] + _dot(a_ref[...].astype(BF16), wo_ref[...])
        xn = x * lax.rsqrt(jnp.mean(x * x, axis=-1, keepdims=True) + EPS) * g_ref[...]
        xn_ref[...] = xn.astype(BF16)
        acc_ref[...] = x

    h = _dot(xn_ref[...], w1_ref[...])
    h = jnp.square(jnp.maximum(h, 0.0))
    acc_ref[...] += _dot(h.astype(BF16), w2_ref[...])

    @pl.when(j == pl.num_programs(1) - 1)
    def _():
        y = acc_ref[...]
        if final_norm:
            y = y * lax.rsqrt(jnp.mean(y * y, axis=-1, keepdims=True) + EPS) * gf_ref[...]
        o_ref[...] = y


def ffn(x, a, wo_bf16, g, w1_bf16, w2_bf16, g_final, final_norm, tm, tf):
    t, d = x.shape
    f = w1_bf16.shape[1]
    ka = a.shape[1]
    return pl.pallas_call(
        functools.partial(_ffn_kernel, final_norm=final_norm),
        grid=(t // tm, f // tf),
        in_specs=[pl.BlockSpec((tm, d), lambda i, j: (i, 0)),
                  pl.BlockSpec((tm, ka), lambda i, j: (i, 0)),
                  pl.BlockSpec((ka, d), lambda i, j: (0, 0)),
                  pl.BlockSpec((1, d), lambda i, j: (0, 0)),
                  pl.BlockSpec((d, tf), lambda i, j: (0, j)),
                  pl.BlockSpec((tf, d), lambda i, j: (j, 0)),
                  pl.BlockSpec((1, d), lambda i, j: (0, 0))],
        out_specs=pl.BlockSpec((tm, d), lambda i, j: (i, 0)),
        out_shape=jax.ShapeDtypeStruct((t, d), F32),
        scratch_shapes=[pltpu.VMEM((tm, d), BF16), pltpu.VMEM((tm, d), F32)],
        compiler_params=_cparams(("parallel", "arbitrary")),
        name="ffn",
    )(x, a, wo_bf16, g.reshape(1, d), w1_bf16, w2_bf16, g_final.reshape(1, d))


def _attn_kernel(q_ref, k_ref, v_ref, o_ref, *, n_heads, dh, row_view, nb):
    scale = dh ** -0.5
    lane_tiles = dh // LANES
    rows_per_token = n_heads * lane_tiles

    def head_slab(ref, i, h):
        if not row_view:
            return ref[i, :, h * dh:(h + 1) * dh].astype(BF16)
        n_mem = ref.shape[1] // rows_per_token
        parts = [ref[i, pl.ds(j * n_heads + h, n_mem, stride=rows_per_token), :] for j in range(lane_tiles)]
        return jnp.concatenate(parts, axis=1).astype(BF16)

    for i in range(nb):
        for h in range(n_heads):
            sl = slice(h * dh, (h + 1) * dh)
            q = q_ref[i, :, sl].astype(BF16)
            s = _dot_nt(q, head_slab(k_ref, i, h)) * scale
            m = jnp.max(s, axis=-1, keepdims=True)
            e = jnp.exp(s - m)
            l = jnp.sum(e, axis=-1, keepdims=True)
            o_ref[i, :, sl] = (_dot(e.astype(BF16), head_slab(v_ref, i, h)) / l).astype(o_ref.dtype)


def mem_attention(q, mem_k, mem_v, tq, nb, layer=None):
    b, l, d = q.shape
    dh = d // MEM_HEADS
    if layer is None:
        m = mem_k.shape[1]
        mem_spec = pl.BlockSpec((nb, m, d), lambda i, j: (i, 0, 0))
    else:
        rows = mem_k.shape[1]
        mem_spec = pl.BlockSpec((nb, rows, LANES), lambda i, j: (layer * (b // nb) + i, 0, 0))
    return pl.pallas_call(
        functools.partial(_attn_kernel, n_heads=MEM_HEADS, dh=dh, row_view=layer is not None, nb=nb),
        grid=(b // nb, l // tq),
        in_specs=[pl.BlockSpec((nb, tq, d), lambda i, j: (i, j, 0)), mem_spec, mem_spec],
        out_specs=pl.BlockSpec((nb, tq, d), lambda i, j: (i, j, 0)),
        out_shape=jax.ShapeDtypeStruct((b, l, d), q.dtype),
        compiler_params=_cparams(("parallel", "arbitrary")),
        name="mem_attention",
    )(q, mem_k, mem_v)


def cache_row_view(cache):
    depth, b, m, nh, dh = cache.shape
    c = cache.reshape(depth, b, m, nh, dh // LANES, LANES).transpose(0, 1, 2, 4, 3, 5)
    return c.reshape(depth * b, m * nh * (dh // LANES), LANES)


def _state_specs(shape_tail, layer, nb):
    nd = len(shape_tail)
    return pl.BlockSpec((None, nb) + tuple(shape_tail), lambda i, j: (layer, i) + (0,) * nd)


_ANY_SPEC = pl.BlockSpec(memory_space=pl.ANY)


def _batch_view(ref, i):
    return None if ref is None else ref.at[pl.ds(i, 1)]


def _ret_kernel(*refs, C, n_pairs, has_state, n_prev, nb):
    n_in = 8 + (1 if has_state else 0)
    (proj_ref, cos_ref, sin_ref, dm_ref, xi_ref, zeta_ref, gc_ref, lnw_ref, *st), rest = \
        _split_refs(refs, n_in, has_state, n_prev)
    s0_ref = st[0] if has_state else None
    o_ref, sout_ref, S_ref = rest
    t = pl.program_id(1)

    @pl.when(t == 0)
    def _():
        for i in range(nb):
            _load_pair_state(S_ref.at[i], _batch_view(s0_ref, i), n_pairs)

    for i in range(nb):
        _ret_body(_batch_view(proj_ref, i), cos_ref, sin_ref, dm_ref, xi_ref, zeta_ref, gc_ref, lnw_ref,
                  _batch_view(o_ref, i), S_ref.at[i], C=C, n_pairs=n_pairs)

    @pl.when(t == pl.num_programs(1) - 1)
    def _():
        for i in range(nb):
            _store_pair_state(_batch_view(sout_ref, i), S_ref.at[i], n_pairs)


def _ret_body(proj_ref, cos_ref, sin_ref, dm_ref, xi_ref, zeta_ref, gc_ref, lnw_ref, o_ref, S_ref, *, C, n_pairs):
    m0 = _head0_lanes()
    first_half = (_lane_iota((1, PAIR)) & (HEAD_DIM - 1)) < HEAD_DIM // 2
    bm = _block_diag_mask()
    pmean = _group_mean_matrix(bm)
    cos = cos_ref[...]
    sin = sin_ref[...]
    lnw = lnw_ref[...]

    def rot(x):
        sw = jnp.where(first_half, pltpu.roll(x, PAIR - HEAD_DIM // 2, 1), pltpu.roll(x, HEAD_DIM // 2, 1))
        return x * cos + sw * sin

    pairs = range(n_pairs)
    q = [rot(proj_ref[0, :, p * PAIR:(p + 1) * PAIR]) for p in pairs]
    k = [rot(proj_ref[0, :, W_RET + p * PAIR:W_RET + (p + 1) * PAIR]) * HEAD_DIM ** -0.5 for p in pairs]
    vb = [proj_ref[0, :, 2 * W_RET + p * PAIR:2 * W_RET + (p + 1) * PAIR].astype(BF16) for p in pairs]
    sc = [_dot_nt(_stack(q[p], m0).astype(BF16), k[p].astype(BF16)) * dm_ref[p] for p in pairs]
    S = [S_ref[p] for p in pairs]
    o_inter = [_dot(q[p].astype(BF16), S[p].astype(BF16)) * xi_ref[p] for p in pairs]
    kv = [_dot_tn((k[p] * zeta_ref[p]).astype(BF16), vb[p]) for p in pairs]
    o2 = [_dot(sc[p].astype(BF16), vb[p]) for p in pairs]
    o = [jnp.where(m0, o2[p][:C], o2[p][C:]) + o_inter[p] for p in pairs]
    oc = [o[p] - _dot(o[p].astype(BF16), pmean) for p in pairs]
    var = [_dot((oc[p] * oc[p]).astype(BF16), pmean) for p in pairs]
    for p in pairs:
        S_ref[p] = S[p] * gc_ref[p][0:1, :] + jnp.where(bm, kv[p], 0.0)
        gate = proj_ref[0, :, 3 * W_RET + p * PAIR:3 * W_RET + (p + 1) * PAIR]
        o_ref[0, :, p * PAIR:(p + 1) * PAIR] = (oc[p] * lax.rsqrt(var[p] + EPS) * lnw * _silu(gate)).astype(o_ref.dtype)


def _ret_consts(C):
    lg = jnp.log1p(-jnp.exp2(-5.0 - jnp.arange(H_RET, dtype=F32)))
    idx = jnp.arange(C, dtype=F32)
    diff = idx[:, None] - idx[None, :]
    dmask = jnp.where(diff >= 0, jnp.exp(lg[:, None, None] * jnp.maximum(diff, 0.0)), 0.0)
    dm = dmask.reshape(H_RET // 2, 2 * C, C)
    xi = jnp.exp(lg[:, None] * (idx + 1.0))
    zeta = jnp.exp(lg[:, None] * (C - 1.0 - idx))
    gch = jnp.exp(lg * C)

    def pair_lanes(a):
        r = a.shape[1]
        a = jnp.repeat(a[:, :, None], HEAD_DIM, axis=2)
        a = a.reshape(H_RET // 2, 2, r, HEAD_DIM).transpose(0, 2, 1, 3)
        return a.reshape(H_RET // 2, r, PAIR)

    gc = pair_lanes(jnp.broadcast_to(gch[:, None], (H_RET, 8)))
    return dm, pair_lanes(xi), pair_lanes(zeta), gc


def _rope_tables(pos):
    half = HEAD_DIM // 2
    inv_freq = ROPE_BASE ** (-jnp.arange(half, dtype=F32) / half)
    ang = pos.astype(F32)[:, None] * inv_freq[None, :]
    cos, sin = jnp.cos(ang), jnp.sin(ang)
    cos_t = jnp.concatenate([cos, cos, cos, cos], axis=-1)
    sin_t = jnp.concatenate([-sin, sin, -sin, sin], axis=-1)
    return cos_t, sin_t


def retention(proj, tables, s0_all, layer, depth, prev, ln_w, C, nb, act_dtype):
    b, l, _ = proj.shape
    n_pairs = H_RET // 2
    dm, xi, zeta, gc, cos_t, sin_t = tables
    lnw = jnp.concatenate([ln_w, ln_w]).reshape(1, PAIR)
    const3 = lambda shape: pl.BlockSpec(shape, lambda i, j: (0, 0, 0))
    st_tail = (H_RET, HEAD_DIM, HEAD_DIM)
    in_specs = [pl.BlockSpec((nb, C, RET_COLS), lambda i, j: (i, j, 0)),
                pl.BlockSpec((C, PAIR), lambda i, j: (j, 0)),
                pl.BlockSpec((C, PAIR), lambda i, j: (j, 0)),
                const3((n_pairs, 2 * C, C)),
                const3((n_pairs, C, PAIR)),
                const3((n_pairs, C, PAIR)),
                const3((n_pairs, 8, PAIR)),
                pl.BlockSpec((1, PAIR), lambda i, j: (0, 0))]
    args = [proj, cos_t, sin_t, dm, xi, zeta, gc, lnw]
    if s0_all is not None:
        in_specs.append(_state_specs(st_tail, layer, nb))
        args.append(s0_all)
    aliases = {}
    if prev is not None:
        aliases[len(args)] = 1
        in_specs.append(_ANY_SPEC)
        args.append(prev)
    return pl.pallas_call(
        functools.partial(_ret_kernel, C=C, n_pairs=n_pairs, has_state=s0_all is not None,
                          n_prev=0 if prev is None else 1, nb=nb),
        grid=(b // nb, l // C),
        in_specs=in_specs,
        out_specs=[pl.BlockSpec((nb, C, W_RET), lambda i, j: (i, j, 0)), _state_specs(st_tail, layer, nb)],
        out_shape=[jax.ShapeDtypeStruct((b, l, W_RET), act_dtype),
                   jax.ShapeDtypeStruct((depth, b) + st_tail, F32)],
        scratch_shapes=[pltpu.VMEM((nb, n_pairs, PAIR, PAIR), F32)],
        input_output_aliases=aliases,
        compiler_params=_cparams(("parallel", "arbitrary")),
        name="retention",
    )(*args)


def _tri_inverse_many(a_list, c, lo, ii, jj):
    def mm(x, y):
        return _dot(x.astype(BF16), _stack(y, lo).astype(BF16))

    base = min(c, TRI_BASE)
    lb = _log2(base)
    dblk = (ii >> lb) == (jj >> lb)
    eye = jnp.where(ii == jj, 1.0, 0.0)
    xs = [jnp.where(dblk, -a, 0.0) for a in a_list]
    ts = [eye + x for x in xs]
    for _ in range(lb - 1):
        xs = [mm(x, x) for x in xs]
        ts = [t + mm(t, x) for t, x in zip(ts, xs)]
    blk = base
    while blk < c:
        inner = (ii >> _log2(blk)) == (jj >> _log2(blk))
        outer = (ii >> _log2(2 * blk)) == (jj >> _log2(2 * blk))
        sel = outer & jnp.logical_not(inner)
        ls = [jnp.where(sel, a, 0.0) for a in a_list]
        ms = [mm(l, t) for l, t in zip(ls, ts)]
        ts = [t - mm(t, m) for t, m in zip(ts, ms)]
        blk *= 2
    return ts


def _dn_kernel(*refs, TT, c, n_pairs, has_state, n_prev, nb):
    n_in = 5 + (2 if has_state else 0)
    (proj_ref, convw_ref, alog_ref, dtb_ref, nw_ref, *st), rest = _split_refs(refs, n_in, has_state, n_prev)
    s0_ref, sconv_ref = st if has_state else (None, None)
    o_ref, sout_ref, convout_ref, S_ref, tail_ref = rest
    n2 = 2 * c
    ii = _row_iota((c, n2))
    lane2 = _lane_iota((c, n2))
    t = pl.program_id(1)

    @pl.when(t == 0)
    def _():
        tail_ref[...] = jnp.zeros_like(tail_ref)
        for i in range(nb):
            _load_pair_state(S_ref.at[i], _batch_view(s0_ref, i), n_pairs)
            if has_state:
                tail_ref[i, 8 - (CONV_W - 1):8, :] = sconv_ref[i]

    items = []
    for i in range(nb):
        items.append(_dn_prepare(_batch_view(proj_ref, i), convw_ref, alog_ref, dtb_ref, tail_ref.at[i], ii, lane2,
                                 TT=TT, c=c, n_pairs=n_pairs))
    flat = [it for batch_items in items for it in batch_items]
    tinvs = _tri_inverse_many([it['a'] for it in flat], c, lane2 < c, ii, lane2 & (c - 1))
    for it, tinv in zip(flat, tinvs):
        tb = tinv.astype(BF16)
        it['u'] = _dot(tb, it['vb'])
        it['w'] = _dot(tb, it['kbe']).astype(BF16)
    for i in range(nb):
        _dn_scan(items[i], _batch_view(proj_ref, i), nw_ref, _batch_view(o_ref, i), S_ref.at[i],
                 TT=TT, c=c, n_pairs=n_pairs)

    @pl.when(t == pl.num_programs(1) - 1)
    def _():
        for i in range(nb):
            _store_pair_state(_batch_view(sout_ref, i), S_ref.at[i], n_pairs)
            convout_ref[i] = tail_ref[i, 8 - (CONV_W - 1):8, :]


def _dn_prepare(proj_ref, convw_ref, alog_ref, dtb_ref, tail_ref, ii, lane2, *, TT, c, n_pairs):
    m0 = _head0_lanes()
    pmean = _group_mean_matrix(_block_diag_mask())
    nchunks = TT // c
    lc = _log2(c)

    x = proj_ref[0, :, 0:DN_CONV_CH]
    tail = tail_ref[...]
    rows8 = _row_iota((8, DN_CONV_CH))
    acc = x * convw_ref[CONV_W - 1:CONV_W, :]
    for s in range(1, CONV_W):
        xs = pltpu.roll(x, s, 0)
        top = jnp.where(rows8 < s, pltpu.roll(tail, s, 0), xs[0:8])
        xs = top if TT == 8 else jnp.concatenate([top, xs[8:]], axis=0)
        acc = acc + xs * convw_ref[CONV_W - 1 - s:CONV_W - s, :]
    tail_ref[...] = x[TT - 8:TT]
    qkv = _silu(acc)

    misc = proj_ref[0, :, DN_CONV_CH + W_DN:DN_COLS]
    sig = jax.nn.sigmoid(misc)
    glog = -jnp.exp(alog_ref[...]) * _softplus(misc + dtb_ref[...])
    rt = _row_iota((TT, TT))
    ct = _lane_iota((TT, TT))
    tril_t = jnp.where(((rt >> lc) == (ct >> lc)) & (ct <= rt), 1.0, 0.0).astype(BF16)
    cum_tok = _dot_exact_lhs(tril_t, glog)

    n2 = 2 * c
    jj = lane2 & (c - 1)
    lo2 = lane2 < c
    incl = jj <= ii
    strict = jj < ii
    ones_c = jnp.ones((c, PAIR), BF16)
    lane128 = _lane_iota((c, PAIR))

    items = []
    for p in range(n_pairs):
        h0, h1 = 2 * p, 2 * p + 1
        q_all = qkv[:, p * PAIR:(p + 1) * PAIR]
        k_all = qkv[:, W_DN + p * PAIR:W_DN + (p + 1) * PAIR]
        v_all = qkv[:, 2 * W_DN + p * PAIR:2 * W_DN + (p + 1) * PAIR]
        q_all = q_all * lax.rsqrt(_dot((q_all * q_all).astype(BF16), pmean) * HEAD_DIM + EPS) * HEAD_DIM ** -0.5
        k_all = k_all * lax.rsqrt(_dot((k_all * k_all).astype(BF16), pmean) * HEAD_DIM + EPS)
        for ci in range(nchunks):
            rows = slice(ci * c, (ci + 1) * c)
            q, k, v = q_all[rows], k_all[rows], v_all[rows]
            b0, b1 = sig[rows, h0:h0 + 1], sig[rows, h1:h1 + 1]
            c0 = cum_tok[rows, DN_A_LANE + h0:DN_A_LANE + h0 + 1]
            c1 = cum_tok[rows, DN_A_LANE + h1:DN_A_LANE + h1 + 1]
            beta = jnp.where(m0, jnp.broadcast_to(b0, (c, PAIR)), jnp.broadcast_to(b1, (c, PAIR)))
            cum_pair = jnp.where(m0, jnp.broadcast_to(c0, (c, PAIR)), jnp.broadcast_to(c1, (c, PAIR)))
            cum_col = cum_pair if n2 == PAIR else jnp.where(lo2, jnp.broadcast_to(c0, (c, n2)),
                                                            jnp.broadcast_to(c1, (c, n2)))
            ctok = cum_tok[rows]
            cum_sel = jnp.concatenate([jnp.where(lane128 ==DN_A_LANE + h0, ctok, 0.0),
                                       jnp.where(lane128 ==DN_A_LANE + h1, ctok, 0.0)], axis=0)
            kb = k * beta
            ecum = jnp.exp(cum_pair)
            last_row = cum_tok[(ci + 1) * c - 1:(ci + 1) * c, :]
            last = jnp.where(m0, jnp.broadcast_to(last_row[:, DN_A_LANE + h0:DN_A_LANE + h0 + 1], (1, PAIR)),
                             jnp.broadcast_to(last_row[:, DN_A_LANE + h1:DN_A_LANE + h1 + 1], (1, PAIR)))
            items.append(dict(p=p, ci=ci, cum_sel=cum_sel, cum_col=cum_col, k_st=_stack(k, m0).astype(BF16),
                              kb=kb.astype(BF16), q=q.astype(BF16),
                              vb=_stack(v * beta, m0).astype(BF16), kbe=_stack(kb * ecum, m0).astype(BF16),
                              q_dec=(q * ecum).astype(BF16), k_dec=(k * jnp.exp(last - cum_pair)).astype(BF16),
                              dec=jnp.exp(last)))
    for it in items:
        cum_row = _dot_nt_exact(ones_c, it.pop('cum_sel'))
        it['gam'] = jnp.where(incl, jnp.exp(jnp.minimum(it.pop('cum_col') - cum_row, 0.0)), 0.0)
    for it in items:
        k_st = it.pop('k_st')
        gam = it.pop('gam')
        it['a'] = jnp.where(strict, _dot_nt(it.pop('kb'), k_st) * gam, 0.0)
        it['attn'] = (_dot_nt(it.pop('q'), k_st) * gam).astype(BF16)
    return items


def _dn_scan(items, proj_ref, nw_ref, o_ref, S_ref, *, TT, c, n_pairs):
    m0 = _head0_lanes()
    bm = _block_diag_mask()
    pmean = _group_mean_matrix(bm)
    nw = nw_ref[...]
    nchunks = TT // c
    S = [S_ref[p] for p in range(n_pairs)]
    outs = [[] for _ in range(n_pairs)]
    for ci in range(nchunks):
        for p in range(n_pairs):
            it = items[p * nchunks + ci]
            Sb = S[p].astype(BF16)
            v_new = it['u'] - _dot(it['w'], Sb)
            o = _dot(it['q_dec'], Sb) + _dot(it['attn'], _stack(v_new, m0).astype(BF16))
            S[p] = S[p] * it['dec'] + jnp.where(bm, _dot_tn(it['k_dec'], v_new.astype(BF16)), 0.0)
            outs[p].append(o)
    for p in range(n_pairs):
        S_ref[p] = S[p]
        o = outs[p][0] if nchunks == 1 else jnp.concatenate(outs[p], axis=0)
        gate = proj_ref[0, :, DN_CONV_CH + p * PAIR:DN_CONV_CH + (p + 1) * PAIR]
        ms = _dot((o * o).astype(BF16), pmean)
        o_ref[0, :, p * PAIR:(p + 1) * PAIR] = (o * lax.rsqrt(ms + EPS) * nw * _silu(gate)).astype(o_ref.dtype)


def delta_net(proj, conv_w, s_conv_all, a_log, dt_bias, s0_all, layer, depth, prev, norm_w, TT, c, nb, act_dtype):
    b, l, _ = proj.shape
    n_pairs = H_DN // 2
    alog = jnp.zeros((1, PAIR), F32).at[0, DN_A_LANE:DN_A_LANE + H_DN].set(a_log)
    dtb = jnp.zeros((1, PAIR), F32).at[0, DN_A_LANE:DN_A_LANE + H_DN].set(dt_bias)
    nw = jnp.concatenate([norm_w, norm_w]).reshape(1, PAIR)
    row = lambda n: pl.BlockSpec((1, n), lambda i, j: (0, 0))
    st_tail = (H_DN, HEAD_DIM, HEAD_DIM)
    cv_tail = (CONV_W - 1, DN_CONV_CH)
    in_specs = [pl.BlockSpec((nb, TT, DN_COLS), lambda i, j: (i, j, 0)),
                pl.BlockSpec((CONV_W, DN_CONV_CH), lambda i, j: (0, 0)),
                row(PAIR), row(PAIR), row(PAIR)]
    args = [proj, conv_w, alog, dtb, nw]
    has_state = s0_all is not None
    if has_state:
        in_specs += [_state_specs(st_tail, layer, nb), _state_specs(cv_tail, layer, nb)]
        args += [s0_all, s_conv_all]
    aliases = {}
    if prev is not None:
        aliases = {len(args): 1, len(args) + 1: 2}
        in_specs += [_ANY_SPEC, _ANY_SPEC]
        args += list(prev)
    return pl.pallas_call(
        functools.partial(_dn_kernel, TT=TT, c=c, n_pairs=n_pairs, has_state=has_state,
                          n_prev=0 if prev is None else 2, nb=nb),
        grid=(b // nb, l // TT),
        in_specs=in_specs,
        out_specs=[pl.BlockSpec((nb, TT, W_DN), lambda i, j: (i, j, 0)),
                   _state_specs(st_tail, layer, nb), _state_specs(cv_tail, layer, nb)],
        out_shape=[jax.ShapeDtypeStruct((b, l, W_DN), act_dtype),
                   jax.ShapeDtypeStruct((depth, b) + st_tail, F32),
                   jax.ShapeDtypeStruct((depth, b) + cv_tail, F32)],
        scratch_shapes=[pltpu.VMEM((nb, n_pairs, PAIR, PAIR), F32), pltpu.VMEM((nb, 8, DN_CONV_CH), F32)],
        input_output_aliases=aliases,
        compiler_params=_cparams(("parallel", "arbitrary")),
        name="delta_net",
    )(*args)


def _gla_kernel(*refs, TT, m, cs, n_pairs, has_state, n_prev, nb):
    n_in = 4 + (1 if has_state else 0)
    (proj_ref, wa2_ref, ba_ref, nw_ref, *st), rest = _split_refs(refs, n_in, has_state, n_prev)
    s0_ref = st[0] if has_state else None
    o_ref, sout_ref, ST_ref = rest
    t = pl.program_id(1)

    @pl.when(t == 0)
    def _():
        for i in range(nb):
            _load_pair_state(ST_ref.at[i], _batch_view(s0_ref, i), n_pairs)
            if has_state:
                for p in range(n_pairs):
                    ST_ref[i, p] = ST_ref[i, p].T

    for i in range(nb):
        _gla_body(_batch_view(proj_ref, i), wa2_ref, ba_ref, nw_ref, _batch_view(o_ref, i), ST_ref.at[i],
                  TT=TT, m=m, cs=cs, n_pairs=n_pairs)

    @pl.when(t == pl.num_programs(1) - 1)
    def _():
        for i in range(nb):
            for p in range(n_pairs):
                s = ST_ref[i, p].T
                sout_ref[i, 2 * p] = s[0:HEAD_DIM, 0:HEAD_DIM]
                sout_ref[i, 2 * p + 1] = s[HEAD_DIM:PAIR, HEAD_DIM:PAIR]


def _gla_body(proj_ref, wa2_ref, ba_ref, nw_ref, o_ref, ST_ref, *, TT, m, cs, n_pairs):
    bm = _block_diag_mask()
    m0 = _head0_lanes()
    pmean = _group_mean_matrix(bm)
    nw = nw_ref[...]
    ng = m // cs
    n2 = 2 * m
    sh = _log2(cs)
    r2 = _row_iota((n2, n2))
    c2 = _lane_iota((n2, n2))
    incl2 = ((r2 >> sh) == (c2 >> sh)) & (c2 <= r2)
    rt = _row_iota((TT, TT))
    ct = _lane_iota((TT, TT))
    same_t = (rt >> sh) == (ct >> sh)
    tril = jnp.where(same_t & (ct <= rt), 1.0, 0.0).astype(BF16)
    ones_blk = jnp.where(same_t, 1.0, 0.0).astype(BF16)
    chunk_of_row = _row_iota((m, PAIR)) >> sh
    n_groups = TT // m

    a1 = proj_ref[0, :, 4 * W_GLA:GLA_COLS].astype(BF16)
    probs = []
    for p in range(n_pairs):
        lanes = slice(p * PAIR, (p + 1) * PAIR)
        logit = _dot(a1, wa2_ref[:, lanes]) + ba_ref[:, lanes]
        g_all = -_softplus(-logit) * (1.0 / GLA_TAU)
        cum_all = _dot_exact_lhs(tril, g_all)
        last_all = _dot_exact_lhs(ones_blk, g_all)
        q_all = proj_ref[0, :, p * PAIR:(p + 1) * PAIR] * HEAD_DIM ** -0.5
        k_all = proj_ref[0, :, W_GLA + p * PAIR:W_GLA + (p + 1) * PAIR]
        v_all = proj_ref[0, :, 2 * W_GLA + p * PAIR:2 * W_GLA + (p + 1) * PAIR]
        qt_all = q_all * jnp.exp(cum_all)
        kt_all = k_all * jnp.exp(-cum_all)
        kdec_all = (k_all * jnp.exp(last_all - cum_all)).astype(BF16)
        for gi in range(n_groups):
            rows = slice(gi * m, (gi + 1) * m)
            probs.append(dict(p=p, qt=qt_all[rows], kt=kt_all[rows], kdec=kdec_all[rows], v=v_all[rows],
                              last=last_all[rows]))
    for pr in probs:
        pr['sc'] = jnp.where(incl2, _dot_nt(_stack(pr['qt'], m0).astype(BF16), _stack(pr['kt'], m0).astype(BF16)), 0.0)
    for pr in probs:
        vb = pr['v'].astype(BF16)
        o2 = _dot(pr['sc'].astype(BF16), _stack(pr['v'], m0).astype(BF16))
        pr['o_intra'] = o2[:m] + o2[m:]
        qtb = pr['qt'].astype(BF16)
        if ng == 1:
            kd_big, pr['q_big'] = pr['kdec'], qtb
        else:
            kd_big = jnp.concatenate([jnp.where(chunk_of_row == n, pr['kdec'], 0) for n in range(ng)], axis=1)
            pr['q_big'] = jnp.concatenate([jnp.where(chunk_of_row == n, qtb, 0) for n in range(ng)], axis=1)
        pr['kvt'] = _dot_tn(vb, kd_big)
    ST = [ST_ref[p] for p in range(n_pairs)]
    for gi in range(n_groups):
        for p in range(n_pairs):
            pr = probs[p * n_groups + gi]
            sts = []
            for n in range(ng):
                sts.append(ST[p].astype(BF16))
                dec = jnp.exp(pr['last'][n * cs:n * cs + 1, :])
                ST[p] = ST[p] * dec + jnp.where(bm, pr['kvt'][:, n * PAIR:(n + 1) * PAIR], 0.0)
            pr['st_all'] = sts[0] if ng == 1 else jnp.concatenate(sts, axis=1)
    for pr in probs:
        pr['o'] = pr['o_intra'] + _dot_nt(pr['q_big'], pr['st_all'])
    for p in range(n_pairs):
        ST_ref[p] = ST[p]
        outs = [probs[p * n_groups + gi]['o'] for gi in range(n_groups)]
        o = outs[0] if n_groups == 1 else jnp.concatenate(outs, axis=0)
        gate = proj_ref[0, :, 3 * W_GLA + p * PAIR:3 * W_GLA + (p + 1) * PAIR]
        ms = _dot((o * o).astype(BF16), pmean)
        o_ref[0, :, p * PAIR:(p + 1) * PAIR] = (o * lax.rsqrt(ms + EPS) * nw * _silu(gate)).astype(o_ref.dtype)


def gla(proj, w_a2, b_a, s0_all, layer, depth, prev, norm_w, TT, m, cs, nb, act_dtype):
    b, l, _ = proj.shape
    n_pairs = H_GLA // 2
    wa2 = jnp.zeros((PAIR, W_GLA), F32).at[:GLA_RANK].set(w_a2).astype(BF16)
    nw = jnp.concatenate([norm_w, norm_w]).reshape(1, PAIR)
    st_tail = (H_GLA, HEAD_DIM, HEAD_DIM)
    in_specs = [pl.BlockSpec((nb, TT, GLA_COLS), lambda i, j: (i, j, 0)),
                pl.BlockSpec((PAIR, W_GLA), lambda i, j: (0, 0)),
                pl.BlockSpec((1, W_GLA), lambda i, j: (0, 0)),
                pl.BlockSpec((1, PAIR), lambda i, j: (0, 0))]
    args = [proj, wa2, b_a.reshape(1, W_GLA), nw]
    if s0_all is not None:
        in_specs.append(_state_specs(st_tail, layer, nb))
        args.append(s0_all)
    aliases = {}
    if prev is not None:
        aliases[len(args)] = 1
        in_specs.append(_ANY_SPEC)
        args.append(prev)
    return pl.pallas_call(
        functools.partial(_gla_kernel, TT=TT, m=m, cs=cs, n_pairs=n_pairs, has_state=s0_all is not None,
                          n_prev=0 if prev is None else 1, nb=nb),
        grid=(b // nb, l // TT),
        in_specs=in_specs,
        out_specs=[pl.BlockSpec((nb, TT, W_GLA), lambda i, j: (i, j, 0)), _state_specs(st_tail, layer, nb)],
        out_shape=[jax.ShapeDtypeStruct((b, l, W_GLA), act_dtype),
                   jax.ShapeDtypeStruct((depth, b) + st_tail, F32)],
        scratch_shapes=[pltpu.VMEM((nb, n_pairs, PAIR, PAIR), F32)],
        input_output_aliases=aliases,
        compiler_params=_cparams(("parallel", "arbitrary")),
        name="gla",
    )(*args)


def _proj_t_kernel(x_ref, g_ref, w_ref, o_ref, xn_ref):
    @pl.when(pl.program_id(0) == 0)
    def _():
        x = x_ref[...]
        xn = x * lax.rsqrt(jnp.mean(x * x, axis=0, keepdims=True) + EPS) * g_ref[...]
        xn_ref[...] = xn.astype(BF16)

    o_ref[...] = _dot(w_ref[...], xn_ref[...])


def proj_t(x_t, g, w_t, tn):
    d, t = x_t.shape
    n = w_t.shape[0]
    return pl.pallas_call(
        _proj_t_kernel,
        grid=(n // tn,),
        in_specs=[pl.BlockSpec((d, t), lambda i: (0, 0)),
                  pl.BlockSpec((d, 1), lambda i: (0, 0)),
                  pl.BlockSpec((tn, d), lambda i: (i, 0))],
        out_specs=pl.BlockSpec((tn, t), lambda i: (i, 0)),
        out_shape=jax.ShapeDtypeStruct((n, t), F32),
        scratch_shapes=[pltpu.VMEM((d, t), BF16)],
        compiler_params=_cparams(("arbitrary",)),
        name="proj_t",
    )(x_t, g.reshape(d, 1), w_t)


SUBLANES = 8


def _row_group(ref, g, lanes):
    return ref[pl.ds(pl.multiple_of(g * SUBLANES, SUBLANES), SUBLANES), lanes]


def _pick_row(block, r):
    return jnp.sum(jnp.where(_row_iota(block.shape) == r, block, 0.0), axis=0, keepdims=True)


def _lane_linear_kernel(*refs, n_tok, nbatch, rotary, layer_norm, row_decay, n_prev):
    if rotary:
        q_ref, k_ref, v_ref, gate_ref, cos_ref, sin_ref, dec_ref, nw_ref, s0_ref = refs[:9]
        rest = refs[9 + n_prev:]
    else:
        q_ref, k_ref, v_ref, gate_ref, a1_ref, wa2_ref, ba_ref, nw_ref, s0_ref = refs[:9]
        rest = refs[9 + n_prev:]
    o_ref, sout_ref, q_scr, k_scr, dec_scr = rest
    if rotary:
        half = HEAD_DIM // 2

        def rot(x):
            sw = jnp.concatenate([x[half:], x[:half]], axis=0)
            return x * cos_ref[...] + sw * sin_ref[...]

        q_scr[...] = rot(q_ref[...])
        k_scr[...] = rot(k_ref[...]) * HEAD_DIM ** -0.5
    else:
        q_scr[...] = q_ref[...] * HEAD_DIM ** -0.5
        k_scr[...] = k_ref[...]
        logit = _dot(wa2_ref[...], a1_ref[...].astype(BF16)) + ba_ref[...]
        dec_scr[...] = jnp.exp(-_softplus(-logit) * (1.0 / GLA_TAU))
    nw = nw_ref[...]
    for t in range(n_tok):
        lanes = slice(t * nbatch, (t + 1) * nbatch)
        v_t = v_ref[:, lanes]
        src = s0_ref if t == 0 else sout_ref

        def body(g, o, src=src, lanes=lanes, v_t=v_t):
            kg = _row_group(k_scr, g, lanes)
            qg = _row_group(q_scr, g, lanes)
            dg = _row_group(dec_scr, g, lanes) if row_decay else None
            for i in range(SUBLANES):
                d = g * SUBLANES + i
                dec = dg[i:i + 1] if row_decay else dec_ref[...]
                s_d = src[d] * dec + kg[i:i + 1] * v_t
                sout_ref[d] = s_d
                o = o + qg[i:i + 1] * s_d
            return o

        o = lax.fori_loop(0, HEAD_DIM // SUBLANES, body, jnp.zeros((HEAD_DIM, nbatch), F32))
        if layer_norm:
            o = o - jnp.mean(o, axis=0, keepdims=True)
        o = o * lax.rsqrt(jnp.mean(o * o, axis=0, keepdims=True) + EPS) * nw
        o_ref[:, lanes] = o * _silu(gate_ref[:, lanes])


def _head_rows(width, col0):
    return pl.BlockSpec((HEAD_DIM, width), lambda h: (col0 // HEAD_DIM + h, 0))


def lane_linear_mixer(proj_t_all, col0, width, n_heads, kind, consts, s_all, layer, depth, prev, n_tok, nbatch):
    t_all = n_tok * nbatch
    rotary = kind == 'ret'
    hr = lambda off: _head_rows(t_all, col0 + off)
    state_spec = pl.BlockSpec((None, None, HEAD_DIM, HEAD_DIM, nbatch), lambda h: (layer, h, 0, 0, 0))
    in_specs = [hr(0), hr(width), hr(2 * width), hr(3 * width)]
    args = [proj_t_all] * 4
    if rotary:
        cos_t, sin_t, gam, nw = consts
        in_specs += [pl.BlockSpec((HEAD_DIM, t_all), lambda h: (0, 0)),
                     pl.BlockSpec((HEAD_DIM, t_all), lambda h: (0, 0)),
                     pl.BlockSpec((None, 1, nbatch), lambda h: (h, 0, 0)),
                     pl.BlockSpec((HEAD_DIM, nbatch), lambda h: (0, 0))]
        args += [cos_t, sin_t, gam, nw]
    else:
        wa2_t, ba_t, nw = consts
        in_specs += [pl.BlockSpec((LANES, t_all), lambda h: ((col0 + 4 * width) // LANES, 0)),
                     pl.BlockSpec((HEAD_DIM, LANES), lambda h: (h, 0)),
                     pl.BlockSpec((HEAD_DIM, 1), lambda h: (h, 0)),
                     pl.BlockSpec((HEAD_DIM, nbatch), lambda h: (0, 0))]
        args += [proj_t_all, wa2_t, ba_t, nw]
    in_specs.append(state_spec)
    args.append(s_all)
    aliases = {}
    if prev is not None:
        aliases[len(args)] = 1
        in_specs.append(_ANY_SPEC)
        args.append(prev)
    return pl.pallas_call(
        functools.partial(_lane_linear_kernel, n_tok=n_tok, nbatch=nbatch, rotary=rotary, layer_norm=rotary,
                          row_decay=not rotary, n_prev=0 if prev is None else 1),
        grid=(n_heads,),
        in_specs=in_specs,
        out_specs=[pl.BlockSpec((HEAD_DIM, t_all), lambda h: (h, 0)), state_spec],
        out_shape=[jax.ShapeDtypeStruct((width, t_all), F32),
                   jax.ShapeDtypeStruct((depth, n_heads, HEAD_DIM, HEAD_DIM, nbatch), F32)],
        scratch_shapes=[pltpu.VMEM((HEAD_DIM, t_all), F32)] * 3,
        input_output_aliases=aliases,
        compiler_params=_cparams(("arbitrary",)),
        name="lane_" + kind,
    )(*args)


def _lane_delta_kernel(*refs, n_tok, nbatch, n_prev):
    (qp_ref, kp_ref, vp_ref, gate_ref, misc_ref, wq_ref, wk_ref, wv_ref, cq_ref, ck_ref, cv_ref,
     alog_ref, dtb_ref, nw_ref, s0_ref) = refs[:15]
    o_ref, sout_ref, cqo_ref, cko_ref, cvo_ref, q_scr, k_scr = refs[15 + n_prev:]
    h = pl.program_id(0)

    def conv(x_ref, w_ref, c_ref, cout_ref):
        xs = [c_ref[j] for j in range(CONV_W - 1)] + [x_ref[:, t * nbatch:(t + 1) * nbatch] for t in range(n_tok)]
        outs = []
        for t in range(n_tok):
            acc = xs[t] * w_ref[0]
            for j in range(1, CONV_W):
                acc = acc + xs[t + j] * w_ref[j]
            outs.append(_silu(acc))
        for j in range(CONV_W - 1):
            cout_ref[j] = xs[n_tok + j]
        return jnp.concatenate(outs, axis=1)

    q = conv(qp_ref, wq_ref, cq_ref, cqo_ref)
    k = conv(kp_ref, wk_ref, ck_ref, cko_ref)
    v = conv(vp_ref, wv_ref, cv_ref, cvo_ref)
    q_scr[...] = q * lax.rsqrt(jnp.sum(q * q, axis=0, keepdims=True) + EPS) * HEAD_DIM ** -0.5
    k_scr[...] = k * lax.rsqrt(jnp.sum(k * k, axis=0, keepdims=True) + EPS)
    misc = misc_ref[0:2 * SUBLANES, :]
    beta = jax.nn.sigmoid(_pick_row(misc, h))
    glog = -jnp.exp(alog_ref[...]) * _softplus(_pick_row(misc, DN_A_LANE + h) + dtb_ref[...])
    a_all = jnp.exp(glog)
    nw = nw_ref[...]
    zero = jnp.zeros((HEAD_DIM, nbatch), F32)
    for t in range(n_tok):
        lanes = slice(t * nbatch, (t + 1) * nbatch)
        src = s0_ref if t == 0 else sout_ref
        a_t = a_all[:, lanes]

        def kts(g, r, src=src, lanes=lanes):
            kg = _row_group(k_scr, g, lanes)
            for i in range(SUBLANES):
                r = r + kg[i:i + 1] * src[g * SUBLANES + i]
            return r

        r = lax.fori_loop(0, HEAD_DIM // SUBLANES, kts, zero)
        w = beta[:, lanes] * (v[:, lanes] - a_t * r)

        def body(g, o, src=src, lanes=lanes, a_t=a_t, w=w):
            kg = _row_group(k_scr, g, lanes)
            qg = _row_group(q_scr, g, lanes)
            for i in range(SUBLANES):
                d = g * SUBLANES + i
                s_d = src[d] * a_t + kg[i:i + 1] * w
                sout_ref[d] = s_d
                o = o + qg[i:i + 1] * s_d
            return o

        o = lax.fori_loop(0, HEAD_DIM // SUBLANES, body, zero)
        o = o * lax.rsqrt(jnp.mean(o * o, axis=0, keepdims=True) + EPS) * nw
        o_ref[:, lanes] = o * _silu(gate_ref[:, lanes])


def lane_delta_net(proj_t_all, col0, conv_w_b, conv_parts, alog_b, dtb_b, nw, s_all, layer, depth, prev, n_tok, nbatch):
    t_all = n_tok * nbatch
    hr = lambda off: _head_rows(t_all, col0 + off)
    state_spec = pl.BlockSpec((None, None, HEAD_DIM, HEAD_DIM, nbatch), lambda h: (layer, h, 0, 0, 0))
    tap_spec = lambda i: pl.BlockSpec((None, CONV_W, HEAD_DIM, nbatch), lambda h, i=i: (i, 0, h, 0))
    cst_spec = pl.BlockSpec((None, CONV_W - 1, HEAD_DIM, nbatch), lambda h: (layer, 0, h, 0))
    head_row = pl.BlockSpec((None, 1, 1), lambda h: (h, 0, 0))
    in_specs = [hr(0), hr(W_DN), hr(2 * W_DN), hr(DN_CONV_CH),
                pl.BlockSpec((LANES, t_all), lambda h: ((col0 + DN_CONV_CH + W_DN) // LANES, 0)),
                tap_spec(0), tap_spec(1), tap_spec(2), cst_spec, cst_spec, cst_spec,
                head_row, head_row, pl.BlockSpec((HEAD_DIM, nbatch), lambda h: (0, 0)), state_spec]
    args = [proj_t_all] * 5 + [conv_w_b] * 3 + list(conv_parts) + [alog_b, dtb_b, nw, s_all]
    aliases = {}
    if prev is not None:
        aliases = {len(args) + i: 1 + i for i in range(4)}
        in_specs += [_ANY_SPEC] * 4
        args += list(prev)
    cshape = jax.ShapeDtypeStruct((depth, CONV_W - 1, W_DN, nbatch), F32)
    return pl.pallas_call(
        functools.partial(_lane_delta_kernel, n_tok=n_tok, nbatch=nbatch, n_prev=0 if prev is None else 4),
        grid=(H_DN,),
        in_specs=in_specs,
        out_specs=[pl.BlockSpec((HEAD_DIM, t_all), lambda h: (h, 0)), state_spec, cst_spec, cst_spec, cst_spec],
        out_shape=[jax.ShapeDtypeStruct((W_DN, t_all), F32),
                   jax.ShapeDtypeStruct((depth, H_DN, HEAD_DIM, HEAD_DIM, nbatch), F32), cshape, cshape, cshape],
        scratch_shapes=[pltpu.VMEM((HEAD_DIM, t_all), F32)] * 2,
        input_output_aliases=aliases,
        compiler_params=_cparams(("arbitrary",)),
        name="lane_delta",
    )(*args)


def _prep_layer_weights(li, w_in_t, w_out, w_mq, w_mo, w_ff1, w_ff2):
    wi = w_in_t[li]
    d = wi.shape[1]
    c0 = RET_COLS
    c1 = c0 + DN_CONV_CH + W_DN + 2 * H_DN
    w_ret = wi[:c0]
    w_dn = jnp.concatenate([wi[c0:c1], jnp.zeros((DN_COLS - (c1 - c0), d), F32)], axis=0)
    w_gla = jnp.concatenate([wi[c1:], jnp.zeros((GLA_COLS - (wi.shape[0] - c1), d), F32)], axis=0)
    cast = lambda a: a.astype(BF16)
    return dict(w_all_t=cast(jnp.concatenate([w_ret, w_dn, w_gla], axis=0)), w_out=cast(w_out[li]),
                w_mq=cast(w_mq[li]), w_mo=cast(w_mo[li]), w_ff1=cast(w_ff1[li]), w_ff2=cast(w_ff2[li]))


def _run_trunk(x, pos, s_ret, s_dn, s_conv, s_gla, mem_k, mem_v, mem_split, lw, p, cfg):
    b, l, d = x.shape
    t = b * l
    depth = len(lw)
    xf = x.reshape(t, d)
    lane_batch = cfg['lane_batch']
    act = cfg['act_dtype']
    if lane_batch:
        lane = _lane_setup(pos, b, s_ret, s_dn, s_conv, s_gla)
    else:
        ret_tables = _ret_consts(cfg['ret_c']) + _rope_tables(pos)
    r_s = d_s = g_s = None
    for li in range(depth):
        w = lw[li]
        g = p['norm_mix'][li]
        if lane_batch:
            o_mix, r_s, d_s, g_s = _lane_mixers(xf, b, l, li, depth, w, p, lane, r_s, d_s, g_s)
            mixed = [o_mix]
        else:
            pr, pd, pg = in_proj(xf, g, w['w_all_t'], (RET_COLS, DN_COLS, GLA_COLS), cfg['tm'])
            pr, pd, pg = pr.reshape(b, l, RET_COLS), pd.reshape(b, l, DN_COLS), pg.reshape(b, l, GLA_COLS)
            o_ret, r_s = retention(pr, ret_tables, s_ret, li, depth, r_s, p['ret_norm_w'][li], cfg['ret_c'],
                                   cfg['nb'], act)
            o_dn, *d_s = delta_net(pd, p['dn_conv_w'][li], s_conv, p['dn_a_log'][li], p['dn_dt_bias'][li], s_dn,
                                   li, depth, d_s, p['dn_norm_w'][li], cfg['dn_tt'], cfg['dn_c'], cfg['nb'], act)
            o_gla, g_s = gla(pg, p['gla_w_a2'][li], p['gla_b_a'][li], s_gla, li, depth, g_s, p['gla_norm_w'][li],
                             cfg['gla_tt'], cfg['gla_m'], cfg['gla_cs'], cfg['nb'], act)
            mixed = [o_ret.reshape(t, W_RET), o_dn.reshape(t, W_DN), o_gla.reshape(t, W_GLA)]
        xf, qm = res_then_proj(mixed, w['w_out'], xf, p['norm_cross'][li], w['w_mq'], cfg['tm'], act)
        qm = qm.reshape(b, l, d)
        if mem_split:
            att = mem_attention(qm, mem_k, mem_v, cfg['tq'], cfg['attn_nb'], layer=li)
        else:
            att = mem_attention(qm, mem_k[li], mem_v[li], cfg['tq'], cfg['attn_nb'])
        xf = ffn(xf, att.reshape(t, d), w['w_mo'], p['norm_ffn'][li], w['w_ff1'], w['w_ff2'], p['norm_final'],
                 li == depth - 1, cfg['ffn_tm'], cfg['ffn_tf'])
    if lane_batch:
        to_batch_major = lambda s: jnp.transpose(s, (0, 4, 1, 2, 3))
        conv = jnp.transpose(jnp.concatenate(d_s[1:], axis=2), (0, 3, 1, 2))
        return (xf.reshape(b, l, d), to_batch_major(r_s), to_batch_major(d_s[0]), conv, to_batch_major(g_s))
    return (xf.reshape(b, l, d), r_s, d_s[0], d_s[1], g_s)


def _lane_setup(pos, b, s_ret, s_dn, s_conv, s_gla):
    half = HEAD_DIM // 2
    inv_freq = ROPE_BASE ** (-jnp.arange(half, dtype=F32) / half)
    ang = pos.astype(F32)[:, None] * inv_freq[None, :]
    cos, sin = jnp.cos(ang), jnp.sin(ang)
    spread = lambda a: jnp.repeat(a.T, b, axis=1)
    lg = jnp.log1p(-jnp.exp2(-5.0 - jnp.arange(H_RET, dtype=F32)))
    to_lane = lambda s: jnp.transpose(s, (0, 2, 3, 4, 1))
    conv_t = jnp.transpose(s_conv, (0, 2, 3, 1))
    return dict(cos=spread(jnp.concatenate([cos, cos], axis=1)), sin=spread(jnp.concatenate([-sin, sin], axis=1)),
                gam=jnp.broadcast_to(jnp.exp(lg)[:, None, None], (H_RET, 1, b)),
                s_ret=to_lane(s_ret), s_dn=to_lane(s_dn), s_gla=to_lane(s_gla),
                conv=[conv_t[:, :, i * W_DN:(i + 1) * W_DN, :] for i in range(3)])


def _lane_mixers(xf, b, l, li, depth, w, p, lane, r_s, d_s, g_s):
    d = xf.shape[1]
    col = lambda v: jnp.broadcast_to(v[:, None], (v.shape[0], b))
    x_t = xf.reshape(b, l, d).transpose(2, 1, 0).reshape(d, l * b)
    pt = proj_t(x_t, p['norm_mix'][li], w['w_all_t'], 256)
    o_ret, r_s = lane_linear_mixer(pt, 0, W_RET, H_RET, 'ret',
                                   (lane['cos'], lane['sin'], lane['gam'], col(p['ret_norm_w'][li])),
                                   lane['s_ret'], li, depth, r_s, l, b)
    taps = p['dn_conv_w'][li].reshape(CONV_W, 3, W_DN).transpose(1, 0, 2)
    taps = jnp.broadcast_to(taps[..., None], taps.shape + (b,))
    o_dn, *d_s = lane_delta_net(pt, RET_COLS, taps, lane['conv'], p['dn_a_log'][li].reshape(H_DN, 1, 1),
                                p['dn_dt_bias'][li].reshape(H_DN, 1, 1), col(p['dn_norm_w'][li]),
                                lane['s_dn'], li, depth, d_s, l, b)
    wa2_t = jnp.zeros((W_GLA, LANES), F32).at[:, :GLA_RANK].set(p['gla_w_a2'][li].T).astype(BF16)
    o_gla, g_s = lane_linear_mixer(pt, RET_COLS + DN_COLS, W_GLA, H_GLA, 'gla',
                                   (wa2_t, p['gla_b_a'][li].reshape(W_GLA, 1), col(p['gla_norm_w'][li])),
                                   lane['s_gla'], li, depth, g_s, l, b)
    o_t = jnp.concatenate([o_ret, o_dn, o_gla], axis=0)
    o = o_t.reshape(o_t.shape[0], l, b).transpose(2, 1, 0).reshape(b * l, o_t.shape[0])
    return o, r_s, d_s, g_s


def kernel(x_prompt, x_sample, state_ret, state_dn, state_dn_conv, state_gla, cache_mem_k, cache_mem_v, mem_prompt, norm_mix, w_in, ret_norm_w, dn_conv_w, dn_a_log, dn_dt_bias, dn_norm_w, gla_w_a2, gla_b_a, gla_norm_w, w_out, norm_cross, norm_mem, w_mq, w_mk, w_mv, w_mo, norm_ffn, w_ff1, w_ff2, norm_final):
    b_p, l_p, d = x_prompt.shape
    b_s, l_s, _ = x_sample.shape
    depth = w_in.shape[0]
    n_mem = mem_prompt.shape[1]
    p = dict(norm_mix=norm_mix, ret_norm_w=ret_norm_w, dn_conv_w=dn_conv_w, dn_a_log=dn_a_log,
             dn_dt_bias=dn_dt_bias, dn_norm_w=dn_norm_w, gla_w_a2=gla_w_a2, gla_b_a=gla_b_a,
             gla_norm_w=gla_norm_w, norm_cross=norm_cross, norm_ffn=norm_ffn, norm_final=norm_final)
    w_in_t = jnp.transpose(w_in, (0, 2, 1))
    lw = [_prep_layer_weights(li, w_in_t, w_out, w_mq, w_mo, w_ff1, w_ff2) for li in range(depth)]

    memf = mem_prompt.reshape(b_p * n_mem, d)
    mk = [rms_matmul(memf, norm_mem[li], w_mk[li], 512).reshape(b_p, n_mem, d) for li in range(depth)]
    mv = [rms_matmul(memf, norm_mem[li], w_mv[li], 512).reshape(b_p, n_mem, d) for li in range(depth)]

    cfg_p = dict(tm=512, ret_c=256, dn_tt=256, dn_c=math.gcd(l_p, DN_CHUNK), gla_tt=256, gla_m=64,
                 gla_cs=math.gcd(l_p, GLA_CHUNK), tq=512, attn_nb=1, ffn_tm=1024, ffn_tf=1024, nb=1, act_dtype=BF16,
                 lane_batch=False)
    y_p, p_ret, p_dn, p_conv, p_gla = _run_trunk(
        x_prompt, jnp.arange(l_p, dtype=jnp.int32), None, None, None, None, mk, mv, False, lw, p, cfg_p)

    cfg_s = dict(tm=512, ret_c=l_s, dn_tt=l_s, dn_c=math.gcd(l_s, DN_CHUNK), gla_tt=l_s,
                 gla_m=math.gcd(l_s, GLA_CHUNK), gla_cs=math.gcd(l_s, GLA_CHUNK), tq=l_s, ffn_tm=1024, ffn_tf=512,
                 attn_nb=2, nb=8, act_dtype=F32, lane_batch=True)
    y_s, s_ret, s_dn, s_conv, s_gla = _run_trunk(
        x_sample, PAST_LEN + jnp.arange(l_s, dtype=jnp.int32), state_ret, state_dn, state_dn_conv, state_gla,
        cache_row_view(cache_mem_k), cache_row_view(cache_mem_v), True, lw, p, cfg_s)

    mem_shape = (depth, b_p, n_mem, MEM_HEADS, d // MEM_HEADS)
    return (y_p, y_s, p_ret, p_dn, p_conv, p_gla, jnp.stack(mk).reshape(mem_shape), jnp.stack(mv).reshape(mem_shape),
            s_ret, s_dn, s_conv, s_gla)
```

```python
import functools
import math

import jax
import jax.numpy as jnp
from jax import lax
from jax.experimental import pallas as pl
from jax.experimental.pallas import tpu as pltpu

F32 = jnp.float32
BF16 = jnp.bfloat16

LANES = 128
HEAD_DIM = 64
PAIR = 2 * HEAD_DIM
H_RET, H_DN, H_GLA = 6, 6, 4
W_RET, W_DN, W_GLA = H_RET * HEAD_DIM, H_DN * HEAD_DIM, H_GLA * HEAD_DIM
CONV_W = 4
DN_CONV_CH = 3 * W_DN
GLA_RANK = 16
GLA_TAU = 16.0
MEM_HEADS = 4
DN_CHUNK = 64
GLA_CHUNK = 16
ROPE_BASE = 10000.0
EPS = 1e-6
PAST_LEN = 16384

DN_COLS = DN_CONV_CH + W_DN + PAIR
GLA_COLS = 4 * W_GLA + PAIR
RET_COLS = 4 * W_RET
DN_A_LANE = H_DN
TRI_BASE = 16

VMEM_LIMIT = 56 * 1024 * 1024


def _cparams(sem):
    return pltpu.CompilerParams(dimension_semantics=sem, vmem_limit_bytes=VMEM_LIMIT)


def _log2(n):
    l = int(math.log2(n))
    assert (1 << l) == n, n
    return l


def _dot(a, b):
    return jnp.dot(a, b, preferred_element_type=F32)


def _dot_nt(a, b):
    return lax.dot_general(a, b, (((1,), (1,)), ((), ())), preferred_element_type=F32)


def _dot_tn(a, b):
    return lax.dot_general(a, b, (((0,), (0,)), ((), ())), preferred_element_type=F32)


def _split2(a):
    hi = a.astype(BF16)
    lo = (a - hi.astype(F32)).astype(BF16)
    return hi, lo


def _split3(a):
    hi = a.astype(BF16)
    r = a - hi.astype(F32)
    mid = r.astype(BF16)
    lo = (r - mid.astype(F32)).astype(BF16)
    return hi, mid, lo


def _dot_exact_lhs(a_bf16, b):
    hi, mid, lo = _split3(b)
    return _dot(a_bf16, hi) + _dot(a_bf16, mid) + _dot(a_bf16, lo)


def _dot_nt_exact(a_bf16, b):
    hi, mid, lo = _split3(b)
    return _dot_nt(a_bf16, hi) + _dot_nt(a_bf16, mid) + _dot_nt(a_bf16, lo)


def _silu(x):
    return x * jax.nn.sigmoid(x)


def _softplus(x):
    return jnp.maximum(x, 0.0) + jnp.log1p(jnp.exp(-jnp.abs(x)))


def _lane_iota(shape):
    return lax.broadcasted_iota(jnp.int32, shape, len(shape) - 1)


def _row_iota(shape):
    return lax.broadcasted_iota(jnp.int32, shape, 0)


def _head0_lanes():
    return _lane_iota((1, PAIR)) < HEAD_DIM


def _stack(x, lo):
    return jnp.concatenate([jnp.where(lo, x, 0.0), jnp.where(lo, 0.0, x)], axis=0)


def _block_diag_mask():
    r = _row_iota((PAIR, PAIR))
    c = _lane_iota((PAIR, PAIR))
    return (r >> 6) == (c >> 6)


def _group_mean_matrix(bm):
    return jnp.where(bm, 1.0 / HEAD_DIM, 0.0).astype(BF16)


def _load_pair_state(S_ref, s0_ref, n_pairs):
    S_ref[...] = jnp.zeros_like(S_ref)
    if s0_ref is not None:
        for p in range(n_pairs):
            S_ref[p, 0:HEAD_DIM, 0:HEAD_DIM] = s0_ref[0, 2 * p]
            S_ref[p, HEAD_DIM:PAIR, HEAD_DIM:PAIR] = s0_ref[0, 2 * p + 1]


def _store_pair_state(sout_ref, S_ref, n_pairs):
    for p in range(n_pairs):
        sout_ref[0, 2 * p] = S_ref[p, 0:HEAD_DIM, 0:HEAD_DIM]
        sout_ref[0, 2 * p + 1] = S_ref[p, HEAD_DIM:PAIR, HEAD_DIM:PAIR]


def _split_refs(refs, n_in, has_state, n_prev):
    ins = refs[:n_in]
    refs = refs[n_in:]
    return ins, refs[n_prev:]


def _rms_matmul_kernel(x_ref, g_ref, w_ref, o_ref, wb_ref, *, w_transposed):
    @pl.when(pl.program_id(0) == 0)
    def _():
        wb_ref[...] = w_ref[...].astype(BF16)

    x = x_ref[...]
    xn = (x * lax.rsqrt(jnp.mean(x * x, axis=-1, keepdims=True) + EPS) * g_ref[...]).astype(BF16)
    y = _dot_nt(xn, wb_ref[...]) if w_transposed else _dot(xn, wb_ref[...])
    o_ref[...] = y.astype(o_ref.dtype)


def rms_matmul(x, g, w, tm, out_dtype=F32, w_transposed=False):
    t, d = x.shape
    n = w.shape[0] if w_transposed else w.shape[1]
    return pl.pallas_call(
        functools.partial(_rms_matmul_kernel, w_transposed=w_transposed),
        grid=(t // tm,),
        in_specs=[pl.BlockSpec((tm, d), lambda i: (i, 0)),
                  pl.BlockSpec((1, d), lambda i: (0, 0)),
                  pl.BlockSpec(w.shape, lambda i: (0, 0))],
        out_specs=pl.BlockSpec((tm, n), lambda i: (i, 0)),
        out_shape=jax.ShapeDtypeStruct((t, n), out_dtype),
        scratch_shapes=[pltpu.VMEM(w.shape, BF16)],
        compiler_params=_cparams(("arbitrary",)),
        name="rms_matmul",
    )(x, g.reshape(1, d), w)


def _in_proj_kernel(x_ref, g_ref, wt_ref, *o_refs):
    x = x_ref[...]
    xn = (x * lax.rsqrt(jnp.mean(x * x, axis=-1, keepdims=True) + EPS) * g_ref[...]).astype(BF16)
    off = 0
    for o_ref in o_refs:
        n = o_ref.shape[1]
        o_ref[...] = _dot_nt(xn, wt_ref[off:off + n, :])
        off += n


def in_proj(x, g, wt_bf16, widths, tm):
    t, d = x.shape
    assert sum(widths) == wt_bf16.shape[0]
    return pl.pallas_call(
        _in_proj_kernel,
        grid=(t // tm,),
        in_specs=[pl.BlockSpec((tm, d), lambda i: (i, 0)),
                  pl.BlockSpec((1, d), lambda i: (0, 0)),
                  pl.BlockSpec(wt_bf16.shape, lambda i: (0, 0))],
        out_specs=[pl.BlockSpec((tm, n), lambda i: (i, 0)) for n in widths],
        out_shape=[jax.ShapeDtypeStruct((t, n), F32) for n in widths],
        compiler_params=_cparams(("parallel",)),
        name="in_proj",
    )(x, g.reshape(1, d), wt_bf16)


def _matmul_res_kernel(*refs, widths):
    n_in = len(widths)
    a_refs = refs[:n_in]
    w_ref, r_ref, o_ref = refs[n_in:]
    acc = r_ref[...]
    off = 0
    for a_ref, wk in zip(a_refs, widths):
        acc = acc + _dot(a_ref[...].astype(BF16), w_ref[off:off + wk, :])
        off += wk
    o_ref[...] = acc


def _res_then_proj_kernel(*refs, widths):
    n_in = len(widths)
    a_refs = refs[:n_in]
    w_ref, r_ref, g_ref, wq_ref, x_ref, q_ref = refs[n_in:]
    acc = r_ref[...]
    off = 0
    for a_ref, wk in zip(a_refs, widths):
        acc = acc + _dot(a_ref[...].astype(BF16), w_ref[off:off + wk, :])
        off += wk
    x_ref[...] = acc
    xn = acc * lax.rsqrt(jnp.mean(acc * acc, axis=-1, keepdims=True) + EPS) * g_ref[...]
    q_ref[...] = _dot(xn.astype(BF16), wq_ref[...]).astype(q_ref.dtype)


def res_then_proj(a_list, w_bf16, res, g, wq_bf16, tm, q_dtype):
    t, n = res.shape
    widths = tuple(a.shape[1] for a in a_list)
    k = w_bf16.shape[0]
    nq = wq_bf16.shape[1]
    assert sum(widths) == k
    in_specs = [pl.BlockSpec((tm, wk), lambda i: (i, 0)) for wk in widths]
    in_specs += [pl.BlockSpec((k, n), lambda i: (0, 0)), pl.BlockSpec((tm, n), lambda i: (i, 0)),
                 pl.BlockSpec((1, n), lambda i: (0, 0)), pl.BlockSpec((n, nq), lambda i: (0, 0))]
    return pl.pallas_call(
        functools.partial(_res_then_proj_kernel, widths=widths),
        grid=(t // tm,),
        in_specs=in_specs,
        out_specs=[pl.BlockSpec((tm, n), lambda i: (i, 0)), pl.BlockSpec((tm, nq), lambda i: (i, 0))],
        out_shape=[jax.ShapeDtypeStruct((t, n), F32), jax.ShapeDtypeStruct((t, nq), q_dtype)],
        compiler_params=_cparams(("parallel",)),
        name="res_then_proj",
    )(*a_list, w_bf16, res, g.reshape(1, n), wq_bf16)


def matmul_res(a_list, w_bf16, res, tm):
    t, n = res.shape
    widths = tuple(a.shape[1] for a in a_list)
    k = w_bf16.shape[0]
    assert sum(widths) == k
    in_specs = [pl.BlockSpec((tm, wk), lambda i: (i, 0)) for wk in widths]
    in_specs += [pl.BlockSpec((k, n), lambda i: (0, 0)), pl.BlockSpec((tm, n), lambda i: (i, 0))]
    return pl.pallas_call(
        functools.partial(_matmul_res_kernel, widths=widths),
        grid=(t // tm,),
        in_specs=in_specs,
        out_specs=pl.BlockSpec((tm, n), lambda i: (i, 0)),
        out_shape=jax.ShapeDtypeStruct((t, n), F32),
        compiler_params=_cparams(("parallel",)),
        name="matmul_res",
    )(*a_list, w_bf16, res)


def _ffn_kernel(x_ref, a_ref, wo_ref, g_ref, w1_ref, w2_ref, gf_ref, o_ref, xn_ref, acc_ref, *, final_norm):
    j = pl.program_id(1)

    @pl.when(j == 0)
    def _():
        x = x_ref[...] + _dot(a_ref[...].astype(BF16), wo_ref[...])
        xn = x * lax.rsqrt(jnp.mean(x * x, axis=-1, keepdims=True) + EPS) * g_ref[...]
        xn_ref[...] = xn.astype(BF16)
        acc_ref[...] = x

    h = _dot(xn_ref[...], w1_ref[...])
    h = jnp.square(jnp.maximum(h, 0.0))
    acc_ref[...] += _dot(h.astype(BF16), w2_ref[...])

    @pl.when(j == pl.num_programs(1) - 1)
    def _():
        y = acc_ref[...]
        if final_norm:
            y = y * lax.rsqrt(jnp.mean(y * y, axis=-1, keepdims=True) + EPS) * gf_ref[...]
        o_ref[...] = y


def ffn(x, a, wo_bf16, g, w1_bf16, w2_bf16, g_final, final_norm, tm, tf):
    t, d = x.shape
    f = w1_bf16.shape[1]
    ka = a.shape[1]
    return pl.pallas_call(
        functools.partial(_ffn_kernel, final_norm=final_norm),
        grid=(t // tm, f // tf),
        in_specs=[pl.BlockSpec((tm, d), lambda i, j: (i, 0)),
                  pl.BlockSpec((tm, ka), lambda i, j: (i, 0)),
                  pl.BlockSpec((ka, d), lambda i, j: (0, 0)),
                  pl.BlockSpec((1, d), lambda i, j: (0, 0)),
                  pl.BlockSpec((d, tf), lambda i, j: (0, j)),
                  pl.BlockSpec((tf, d), lambda i, j: (j, 0)),
                  pl.BlockSpec((1, d), lambda i, j: (0, 0))],
        out_specs=pl.BlockSpec((tm, d), lambda i, j: (i, 0)),
        out_shape=jax.ShapeDtypeStruct((t, d), F32),
        scratch_shapes=[pltpu.VMEM((tm, d), BF16), pltpu.VMEM((tm, d), F32)],
        compiler_params=_cparams(("parallel", "arbitrary")),
        name="ffn",
    )(x, a, wo_bf16, g.reshape(1, d), w1_bf16, w2_bf16, g_final.reshape(1, d))


def _attn_kernel(q_ref, k_ref, v_ref, o_ref, *, n_heads, dh, row_view, nb):
    scale = dh ** -0.5
    lane_tiles = dh // LANES
    rows_per_token = n_heads * lane_tiles

    def head_slab(ref, i, h):
        if not row_view:
            return ref[i, :, h * dh:(h + 1) * dh].astype(BF16)
        n_mem = ref.shape[1] // rows_per_token
        parts = [ref[i, pl.ds(j * n_heads + h, n_mem, stride=rows_per_token), :] for j in range(lane_tiles)]
        return jnp.concatenate(parts, axis=1).astype(BF16)

    probs = [(i, h) for i in range(nb) for h in range(n_heads)]
    s = [_dot_nt(q_ref[i, :, h * dh:(h + 1) * dh].astype(BF16), head_slab(k_ref, i, h)) * scale for i, h in probs]
    e = [jnp.exp(x - jnp.max(x, axis=-1, keepdims=True)) for x in s]
    l = [jnp.sum(x, axis=-1, keepdims=True) for x in e]
    o = [_dot(x.astype(BF16), head_slab(v_ref, i, h)) for x, (i, h) in zip(e, probs)]
    for (i, h), x, y in zip(probs, o, l):
        o_ref[i, :, h * dh:(h + 1) * dh] = (x / y).astype(o_ref.dtype)


def mem_attention(q, mem_k, mem_v, tq, nb, layer=None):
    b, l, d = q.shape
    dh = d // MEM_HEADS
    if layer is None:
        m = mem_k.shape[1]
        mem_spec = pl.BlockSpec((nb, m, d), lambda i, j: (i, 0, 0))
    else:
        rows = mem_k.shape[1]
        mem_spec = pl.BlockSpec((nb, rows, LANES), lambda i, j: (layer * (b // nb) + i, 0, 0))
    return pl.pallas_call(
        functools.partial(_attn_kernel, n_heads=MEM_HEADS, dh=dh, row_view=layer is not None, nb=nb),
        grid=(b // nb, l // tq),
        in_specs=[pl.BlockSpec((nb, tq, d), lambda i, j: (i, j, 0)), mem_spec, mem_spec],
        out_specs=pl.BlockSpec((nb, tq, d), lambda i, j: (i, j, 0)),
        out_shape=jax.ShapeDtypeStruct((b, l, d), q.dtype),
        compiler_params=_cparams(("parallel", "arbitrary")),
        name="mem_attention",
    )(q, mem_k, mem_v)


def cache_row_view(cache):
    depth, b, m, nh, dh = cache.shape
    c = cache.reshape(depth, b, m, nh, dh // LANES, LANES).transpose(0, 1, 2, 4, 3, 5)
    return c.reshape(depth * b, m * nh * (dh // LANES), LANES)


def _state_specs(shape_tail, layer, nb):
    nd = len(shape_tail)
    return pl.BlockSpec((None, nb) + tuple(shape_tail), lambda i, j: (layer, i) + (0,) * nd)


_ANY_SPEC = pl.BlockSpec(memory_space=pl.ANY)


def _batch_view(ref, i):
    return None if ref is None else ref.at[pl.ds(i, 1)]


def _ret_kernel(*refs, C, n_pairs, has_state, n_prev, nb):
    n_in = 8 + (1 if has_state else 0)
    (proj_ref, cos_ref, sin_ref, dm_ref, xi_ref, zeta_ref, gc_ref, lnw_ref, *st), rest = \
        _split_refs(refs, n_in, has_state, n_prev)
    s0_ref = st[0] if has_state else None
    o_ref, sout_ref, S_ref = rest
    t = pl.program_id(1)

    @pl.when(t == 0)
    def _():
        for i in range(nb):
            _load_pair_state(S_ref.at[i], _batch_view(s0_ref, i), n_pairs)

    for i in range(nb):
        _ret_body(_batch_view(proj_ref, i), cos_ref, sin_ref, dm_ref, xi_ref, zeta_ref, gc_ref, lnw_ref,
                  _batch_view(o_ref, i), S_ref.at[i], C=C, n_pairs=n_pairs)

    @pl.when(t == pl.num_programs(1) - 1)
    def _():
        for i in range(nb):
            _store_pair_state(_batch_view(sout_ref, i), S_ref.at[i], n_pairs)


def _ret_body(proj_ref, cos_ref, sin_ref, dm_ref, xi_ref, zeta_ref, gc_ref, lnw_ref, o_ref, S_ref, *, C, n_pairs):
    m0 = _head0_lanes()
    first_half = (_lane_iota((1, PAIR)) & (HEAD_DIM - 1)) < HEAD_DIM // 2
    bm = _block_diag_mask()
    pmean = _group_mean_matrix(bm)
    cos = cos_ref[...]
    sin = sin_ref[...]
    lnw = lnw_ref[...]

    def rot(x):
        sw = jnp.where(first_half, pltpu.roll(x, PAIR - HEAD_DIM // 2, 1), pltpu.roll(x, HEAD_DIM // 2, 1))
        return x * cos + sw * sin

    pairs = range(n_pairs)
    q = [rot(proj_ref[0, :, p * PAIR:(p + 1) * PAIR]) for p in pairs]
    k = [rot(proj_ref[0, :, W_RET + p * PAIR:W_RET + (p + 1) * PAIR]) * HEAD_DIM ** -0.5 for p in pairs]
    vb = [proj_ref[0, :, 2 * W_RET + p * PAIR:2 * W_RET + (p + 1) * PAIR].astype(BF16) for p in pairs]
    sc = [_dot_nt(_stack(q[p], m0).astype(BF16), k[p].astype(BF16)) * dm_ref[p] for p in pairs]
    S = [S_ref[p] for p in pairs]
    o_inter = [_dot(q[p].astype(BF16), S[p].astype(BF16)) * xi_ref[p] for p in pairs]
    kv = [_dot_tn((k[p] * zeta_ref[p]).astype(BF16), vb[p]) for p in pairs]
    o2 = [_dot(sc[p].astype(BF16), vb[p]) for p in pairs]
    o = [jnp.where(m0, o2[p][:C], o2[p][C:]) + o_inter[p] for p in pairs]
    oc = [o[p] - _dot(o[p].astype(BF16), pmean) for p in pairs]
    var = [_dot((oc[p] * oc[p]).astype(BF16), pmean) for p in pairs]
    for p in pairs:
        S_ref[p] = S[p] * gc_ref[p][0:1, :] + jnp.where(bm, kv[p], 0.0)
        gate = proj_ref[0, :, 3 * W_RET + p * PAIR:3 * W_RET + (p + 1) * PAIR]
        o_ref[0, :, p * PAIR:(p + 1) * PAIR] = (oc[p] * lax.rsqrt(var[p] + EPS) * lnw * _silu(gate)).astype(o_ref.dtype)


def _ret_consts(C):
    lg = jnp.log1p(-jnp.exp2(-5.0 - jnp.arange(H_RET, dtype=F32)))
    idx = jnp.arange(C, dtype=F32)
    diff = idx[:, None] - idx[None, :]
    dmask = jnp.where(diff >= 0, jnp.exp(lg[:, None, None] * jnp.maximum(diff, 0.0)), 0.0)
    dm = dmask.reshape(H_RET // 2, 2 * C, C)
    xi = jnp.exp(lg[:, None] * (idx + 1.0))
    zeta = jnp.exp(lg[:, None] * (C - 1.0 - idx))
    gch = jnp.exp(lg * C)

    def pair_lanes(a):
        r = a.shape[1]
        a = jnp.repeat(a[:, :, None], HEAD_DIM, axis=2)
        a = a.reshape(H_RET // 2, 2, r, HEAD_DIM).transpose(0, 2, 1, 3)
        return a.reshape(H_RET // 2, r, PAIR)

    gc = pair_lanes(jnp.broadcast_to(gch[:, None], (H_RET, 8)))
    return dm, pair_lanes(xi), pair_lanes(zeta), gc


def _rope_tables(pos):
    half = HEAD_DIM // 2
    inv_freq = ROPE_BASE ** (-jnp.arange(half, dtype=F32) / half)
    ang = pos.astype(F32)[:, None] * inv_freq[None, :]
    cos, sin = jnp.cos(ang), jnp.sin(ang)
    cos_t = jnp.concatenate([cos, cos, cos, cos], axis=-1)
    sin_t = jnp.concatenate([-sin, sin, -sin, sin], axis=-1)
    return cos_t, sin_t


def retention(proj, tables, s0_all, layer, depth, prev, ln_w, C, nb, act_dtype):
    b, l, _ = proj.shape
    n_pairs = H_RET // 2
    dm, xi, zeta, gc, cos_t, sin_t = tables
    lnw = jnp.concatenate([ln_w, ln_w]).reshape(1, PAIR)
    const3 = lambda shape: pl.BlockSpec(shape, lambda i, j: (0, 0, 0))
    st_tail = (H_RET, HEAD_DIM, HEAD_DIM)
    in_specs = [pl.BlockSpec((nb, C, RET_COLS), lambda i, j: (i, j, 0)),
                pl.BlockSpec((C, PAIR), lambda i, j: (j, 0)),
                pl.BlockSpec((C, PAIR), lambda i, j: (j, 0)),
                const3((n_pairs, 2 * C, C)),
                const3((n_pairs, C, PAIR)),
                const3((n_pairs, C, PAIR)),
                const3((n_pairs, 8, PAIR)),
                pl.BlockSpec((1, PAIR), lambda i, j: (0, 0))]
    args = [proj, cos_t, sin_t, dm, xi, zeta, gc, lnw]
    if s0_all is not None:
        in_specs.append(_state_specs(st_tail, layer, nb))
        args.append(s0_all)
    aliases = {}
    if prev is not None:
        aliases[len(args)] = 1
        in_specs.append(_ANY_SPEC)
        args.append(prev)
    return pl.pallas_call(
        functools.partial(_ret_kernel, C=C, n_pairs=n_pairs, has_state=s0_all is not None,
                          n_prev=0 if prev is None else 1, nb=nb),
        grid=(b // nb, l // C),
        in_specs=in_specs,
        out_specs=[pl.BlockSpec((nb, C, W_RET), lambda i, j: (i, j, 0)), _state_specs(st_tail, layer, nb)],
        out_shape=[jax.ShapeDtypeStruct((b, l, W_RET), act_dtype),
                   jax.ShapeDtypeStruct((depth, b) + st_tail, F32)],
        scratch_shapes=[pltpu.VMEM((nb, n_pairs, PAIR, PAIR), F32)],
        input_output_aliases=aliases,
        compiler_params=_cparams(("parallel", "arbitrary")),
        name="retention",
    )(*args)


def _tri_inverse_many(a_list, c, lo, ii, jj):
    def mm(x, y):
        return _dot(x.astype(BF16), _stack(y, lo).astype(BF16))

    base = min(c, TRI_BASE)
    lb = _log2(base)
    dblk = (ii >> lb) == (jj >> lb)
    eye = jnp.where(ii == jj, 1.0, 0.0)
    xs = [jnp.where(dblk, -a, 0.0) for a in a_list]
    ts = [eye + x for x in xs]
    for _ in range(lb - 1):
        xs = [mm(x, x) for x in xs]
        ts = [t + mm(t, x) for t, x in zip(ts, xs)]
    blk = base
    while blk < c:
        inner = (ii >> _log2(blk)) == (jj >> _log2(blk))
        outer = (ii >> _log2(2 * blk)) == (jj >> _log2(2 * blk))
        sel = outer & jnp.logical_not(inner)
        ls = [jnp.where(sel, a, 0.0) for a in a_list]
        ms = [mm(l, t) for l, t in zip(ls, ts)]
        ts = [t - mm(t, m) for t, m in zip(ts, ms)]
        blk *= 2
    return ts


def _dn_kernel(*refs, TT, c, n_pairs, has_state, n_prev, nb):
    n_in = 5 + (2 if has_state else 0)
    (proj_ref, convw_ref, alog_ref, dtb_ref, nw_ref, *st), rest = _split_refs(refs, n_in, has_state, n_prev)
    s0_ref, sconv_ref = st if has_state else (None, None)
    o_ref, sout_ref, convout_ref, S_ref, tail_ref = rest
    n2 = 2 * c
    ii = _row_iota((c, n2))
    lane2 = _lane_iota((c, n2))
    t = pl.program_id(1)

    @pl.when(t == 0)
    def _():
        tail_ref[...] = jnp.zeros_like(tail_ref)
        for i in range(nb):
            _load_pair_state(S_ref.at[i], _batch_view(s0_ref, i), n_pairs)
            if has_state:
                tail_ref[i, 8 - (CONV_W - 1):8, :] = sconv_ref[i]

    items = []
    for i in range(nb):
        items.append(_dn_prepare(_batch_view(proj_ref, i), convw_ref, alog_ref, dtb_ref, tail_ref.at[i], ii, lane2,
                                 TT=TT, c=c, n_pairs=n_pairs))
    flat = [it for batch_items in items for it in batch_items]
    tinvs = _tri_inverse_many([it['a'] for it in flat], c, lane2 < c, ii, lane2 & (c - 1))
    for it, tinv in zip(flat, tinvs):
        tb = tinv.astype(BF16)
        it['u'] = _dot(tb, it['vb'])
        it['w'] = _dot(tb, it['kbe']).astype(BF16)
    for i in range(nb):
        _dn_scan(items[i], _batch_view(proj_ref, i), nw_ref, _batch_view(o_ref, i), S_ref.at[i],
                 TT=TT, c=c, n_pairs=n_pairs)

    @pl.when(t == pl.num_programs(1) - 1)
    def _():
        for i in range(nb):
            _store_pair_state(_batch_view(sout_ref, i), S_ref.at[i], n_pairs)
            convout_ref[i] = tail_ref[i, 8 - (CONV_W - 1):8, :]


def _dn_prepare(proj_ref, convw_ref, alog_ref, dtb_ref, tail_ref, ii, lane2, *, TT, c, n_pairs):
    m0 = _head0_lanes()
    pmean = _group_mean_matrix(_block_diag_mask())
    nchunks = TT // c
    lc = _log2(c)

    x = proj_ref[0, :, 0:DN_CONV_CH]
    tail = tail_ref[...]
    rows8 = _row_iota((8, DN_CONV_CH))
    acc = x * convw_ref[CONV_W - 1:CONV_W, :]
    for s in range(1, CONV_W):
        xs = pltpu.roll(x, s, 0)
        top = jnp.where(rows8 < s, pltpu.roll(tail, s, 0), xs[0:8])
        xs = top if TT == 8 else jnp.concatenate([top, xs[8:]], axis=0)
        acc = acc + xs * convw_ref[CONV_W - 1 - s:CONV_W - s, :]
    tail_ref[...] = x[TT - 8:TT]
    qkv = _silu(acc)

    misc = proj_ref[0, :, DN_CONV_CH + W_DN:DN_COLS]
    sig = jax.nn.sigmoid(misc)
    glog = -jnp.exp(alog_ref[...]) * _softplus(misc + dtb_ref[...])
    rt = _row_iota((TT, TT))
    ct = _lane_iota((TT, TT))
    tril_t = jnp.where(((rt >> lc) == (ct >> lc)) & (ct <= rt), 1.0, 0.0).astype(BF16)
    cum_tok = _dot_exact_lhs(tril_t, glog)

    n2 = 2 * c
    jj = lane2 & (c - 1)
    lo2 = lane2 < c
    incl = jj <= ii
    strict = jj < ii
    ones_c = jnp.ones((c, PAIR), BF16)
    lane128 = _lane_iota((c, PAIR))

    items = []
    for p in range(n_pairs):
        h0, h1 = 2 * p, 2 * p + 1
        q_all = qkv[:, p * PAIR:(p + 1) * PAIR]
        k_all = qkv[:, W_DN + p * PAIR:W_DN + (p + 1) * PAIR]
        v_all = qkv[:, 2 * W_DN + p * PAIR:2 * W_DN + (p + 1) * PAIR]
        q_all = q_all * lax.rsqrt(_dot((q_all * q_all).astype(BF16), pmean) * HEAD_DIM + EPS) * HEAD_DIM ** -0.5
        k_all = k_all * lax.rsqrt(_dot((k_all * k_all).astype(BF16), pmean) * HEAD_DIM + EPS)
        for ci in range(nchunks):
            rows = slice(ci * c, (ci + 1) * c)
            q, k, v = q_all[rows], k_all[rows], v_all[rows]
            b0, b1 = sig[rows, h0:h0 + 1], sig[rows, h1:h1 + 1]
            c0 = cum_tok[rows, DN_A_LANE + h0:DN_A_LANE + h0 + 1]
            c1 = cum_tok[rows, DN_A_LANE + h1:DN_A_LANE + h1 + 1]
            beta = jnp.where(m0, jnp.broadcast_to(b0, (c, PAIR)), jnp.broadcast_to(b1, (c, PAIR)))
            cum_pair = jnp.where(m0, jnp.broadcast_to(c0, (c, PAIR)), jnp.broadcast_to(c1, (c, PAIR)))
            cum_col = cum_pair if n2 == PAIR else jnp.where(lo2, jnp.broadcast_to(c0, (c, n2)),
                                                            jnp.broadcast_to(c1, (c, n2)))
            ctok = cum_tok[rows]
            cum_sel = jnp.concatenate([jnp.where(lane128 ==DN_A_LANE + h0, ctok, 0.0),
                                       jnp.where(lane128 ==DN_A_LANE + h1, ctok, 0.0)], axis=0)
            kb = k * beta
            ecum = jnp.exp(cum_pair)
            last_row = cum_tok[(ci + 1) * c - 1:(ci + 1) * c, :]
            last = jnp.where(m0, jnp.broadcast_to(last_row[:, DN_A_LANE + h0:DN_A_LANE + h0 + 1], (1, PAIR)),
                             jnp.broadcast_to(last_row[:, DN_A_LANE + h1:DN_A_LANE + h1 + 1], (1, PAIR)))
            items.append(dict(p=p, ci=ci, cum_sel=cum_sel, cum_col=cum_col, k_st=_stack(k, m0).astype(BF16),
                              kb=kb.astype(BF16), q=q.astype(BF16),
                              vb=_stack(v * beta, m0).astype(BF16), kbe=_stack(kb * ecum, m0).astype(BF16),
                              q_dec=(q * ecum).astype(BF16), k_dec=(k * jnp.exp(last - cum_pair)).astype(BF16),
                              dec=jnp.exp(last)))
    for it in items:
        cum_row = _dot_nt_exact(ones_c, it.pop('cum_sel'))
        it['gam'] = jnp.where(incl, jnp.exp(jnp.minimum(it.pop('cum_col') - cum_row, 0.0)), 0.0)
    for it in items:
        k_st = it.pop('k_st')
        gam = it.pop('gam')
        it['a'] = jnp.where(strict, _dot_nt(it.pop('kb'), k_st) * gam, 0.0)
        it['attn'] = (_dot_nt(it.pop('q'), k_st) * gam).astype(BF16)
    return items


def _dn_scan(items, proj_ref, nw_ref, o_ref, S_ref, *, TT, c, n_pairs):
    m0 = _head0_lanes()
    bm = _block_diag_mask()
    pmean = _group_mean_matrix(bm)
    nw = nw_ref[...]
    nchunks = TT // c
    S = [S_ref[p] for p in range(n_pairs)]
    outs = [[] for _ in range(n_pairs)]
    for ci in range(nchunks):
        for p in range(n_pairs):
            it = items[p * nchunks + ci]
            Sb = S[p].astype(BF16)
            v_new = it['u'] - _dot(it['w'], Sb)
            o = _dot(it['q_dec'], Sb) + _dot(it['attn'], _stack(v_new, m0).astype(BF16))
            S[p] = S[p] * it['dec'] + jnp.where(bm, _dot_tn(it['k_dec'], v_new.astype(BF16)), 0.0)
            outs[p].append(o)
    for p in range(n_pairs):
        S_ref[p] = S[p]
        o = outs[p][0] if nchunks == 1 else jnp.concatenate(outs[p], axis=0)
        gate = proj_ref[0, :, DN_CONV_CH + p * PAIR:DN_CONV_CH + (p + 1) * PAIR]
        ms = _dot((o * o).astype(BF16), pmean)
        o_ref[0, :, p * PAIR:(p + 1) * PAIR] = (o * lax.rsqrt(ms + EPS) * nw * _silu(gate)).astype(o_ref.dtype)


def delta_net(proj, conv_w, s_conv_all, a_log, dt_bias, s0_all, layer, depth, prev, norm_w, TT, c, nb, act_dtype):
    b, l, _ = proj.shape
    n_pairs = H_DN // 2
    alog = jnp.zeros((1, PAIR), F32).at[0, DN_A_LANE:DN_A_LANE + H_DN].set(a_log)
    dtb = jnp.zeros((1, PAIR), F32).at[0, DN_A_LANE:DN_A_LANE + H_DN].set(dt_bias)
    nw = jnp.concatenate([norm_w, norm_w]).reshape(1, PAIR)
    row = lambda n: pl.BlockSpec((1, n), lambda i, j: (0, 0))
    st_tail = (H_DN, HEAD_DIM, HEAD_DIM)
    cv_tail = (CONV_W - 1, DN_CONV_CH)
    in_specs = [pl.BlockSpec((nb, TT, DN_COLS), lambda i, j: (i, j, 0)),
                pl.BlockSpec((CONV_W, DN_CONV_CH), lambda i, j: (0, 0)),
                row(PAIR), row(PAIR), row(PAIR)]
    args = [proj, conv_w, alog, dtb, nw]
    has_state = s0_all is not None
    if has_state:
        in_specs += [_state_specs(st_tail, layer, nb), _state_specs(cv_tail, layer, nb)]
        args += [s0_all, s_conv_all]
    aliases = {}
    if prev is not None:
        aliases = {len(args): 1, len(args) + 1: 2}
        in_specs += [_ANY_SPEC, _ANY_SPEC]
        args += list(prev)
    return pl.pallas_call(
        functools.partial(_dn_kernel, TT=TT, c=c, n_pairs=n_pairs, has_state=has_state,
                          n_prev=0 if prev is None else 2, nb=nb),
        grid=(b // nb, l // TT),
        in_specs=in_specs,
        out_specs=[pl.BlockSpec((nb, TT, W_DN), lambda i, j: (i, j, 0)),
                   _state_specs(st_tail, layer, nb), _state_specs(cv_tail, layer, nb)],
        out_shape=[jax.ShapeDtypeStruct((b, l, W_DN), act_dtype),
                   jax.ShapeDtypeStruct((depth, b) + st_tail, F32),
                   jax.ShapeDtypeStruct((depth, b) + cv_tail, F32)],
        scratch_shapes=[pltpu.VMEM((nb, n_pairs, PAIR, PAIR), F32), pltpu.VMEM((nb, 8, DN_CONV_CH), F32)],
        input_output_aliases=aliases,
        compiler_params=_cparams(("parallel", "arbitrary")),
        name="delta_net",
    )(*args)


def _gla_kernel(*refs, TT, m, cs, n_pairs, has_state, n_prev, nb):
    n_in = 4 + (1 if has_state else 0)
    (proj_ref, wa2_ref, ba_ref, nw_ref, *st), rest = _split_refs(refs, n_in, has_state, n_prev)
    s0_ref = st[0] if has_state else None
    o_ref, sout_ref, ST_ref = rest
    t = pl.program_id(1)

    @pl.when(t == 0)
    def _():
        for i in range(nb):
            _load_pair_state(ST_ref.at[i], _batch_view(s0_ref, i), n_pairs)
            if has_state:
                for p in range(n_pairs):
                    ST_ref[i, p] = ST_ref[i, p].T

    for i in range(nb):
        _gla_body(_batch_view(proj_ref, i), wa2_ref, ba_ref, nw_ref, _batch_view(o_ref, i), ST_ref.at[i],
                  TT=TT, m=m, cs=cs, n_pairs=n_pairs)

    @pl.when(t == pl.num_programs(1) - 1)
    def _():
        for i in range(nb):
            for p in range(n_pairs):
                s = ST_ref[i, p].T
                sout_ref[i, 2 * p] = s[0:HEAD_DIM, 0:HEAD_DIM]
                sout_ref[i, 2 * p + 1] = s[HEAD_DIM:PAIR, HEAD_DIM:PAIR]


def _gla_body(proj_ref, wa2_ref, ba_ref, nw_ref, o_ref, ST_ref, *, TT, m, cs, n_pairs):
    bm = _block_diag_mask()
    m0 = _head0_lanes()
    pmean = _group_mean_matrix(bm)
    nw = nw_ref[...]
    ng = m // cs
    n2 = 2 * m
    sh = _log2(cs)
    r2 = _row_iota((n2, n2))
    c2 = _lane_iota((n2, n2))
    incl2 = ((r2 >> sh) == (c2 >> sh)) & (c2 <= r2)
    rt = _row_iota((TT, TT))
    ct = _lane_iota((TT, TT))
    same_t = (rt >> sh) == (ct >> sh)
    tril = jnp.where(same_t & (ct <= rt), 1.0, 0.0).astype(BF16)
    ones_blk = jnp.where(same_t, 1.0, 0.0).astype(BF16)
    chunk_of_row = _row_iota((m, PAIR)) >> sh
    n_groups = TT // m

    a1 = proj_ref[0, :, 4 * W_GLA:GLA_COLS].astype(BF16)
    probs = []
    for p in range(n_pairs):
        lanes = slice(p * PAIR, (p + 1) * PAIR)
        logit = _dot(a1, wa2_ref[:, lanes]) + ba_ref[:, lanes]
        g_all = -_softplus(-logit) * (1.0 / GLA_TAU)
        cum_all = _dot_exact_lhs(tril, g_all)
        last_all = _dot_exact_lhs(ones_blk, g_all)
        q_all = proj_ref[0, :, p * PAIR:(p + 1) * PAIR] * HEAD_DIM ** -0.5
        k_all = proj_ref[0, :, W_GLA + p * PAIR:W_GLA + (p + 1) * PAIR]
        v_all = proj_ref[0, :, 2 * W_GLA + p * PAIR:2 * W_GLA + (p + 1) * PAIR]
        qt_all = q_all * jnp.exp(cum_all)
        kt_all = k_all * jnp.exp(-cum_all)
        kdec_all = (k_all * jnp.exp(last_all - cum_all)).astype(BF16)
        for gi in range(n_groups):
            rows = slice(gi * m, (gi + 1) * m)
            probs.append(dict(p=p, qt=qt_all[rows], kt=kt_all[rows], kdec=kdec_all[rows], v=v_all[rows],
                              last=last_all[rows]))
    for pr in probs:
        pr['sc'] = jnp.where(incl2, _dot_nt(_stack(pr['qt'], m0).astype(BF16), _stack(pr['kt'], m0).astype(BF16)), 0.0)
    for pr in probs:
        vb = pr['v'].astype(BF16)
        o2 = _dot(pr['sc'].astype(BF16), _stack(pr['v'], m0).astype(BF16))
        pr['o_intra'] = o2[:m] + o2[m:]
        qtb = pr['qt'].astype(BF16)
        if ng == 1:
            kd_big, pr['q_big'] = pr['kdec'], qtb
        else:
            kd_big = jnp.concatenate([jnp.where(chunk_of_row == n, pr['kdec'], 0) for n in range(ng)], axis=1)
            pr['q_big'] = jnp.concatenate([jnp.where(chunk_of_row == n, qtb, 0) for n in range(ng)], axis=1)
        pr['kvt'] = _dot_tn(vb, kd_big)
    ST = [ST_ref[p] for p in range(n_pairs)]
    for gi in range(n_groups):
        for p in range(n_pairs):
            pr = probs[p * n_groups + gi]
            sts = []
            for n in range(ng):
                sts.append(ST[p].astype(BF16))
                dec = jnp.exp(pr['last'][n * cs:n * cs + 1, :])
                ST[p] = ST[p] * dec + jnp.where(bm, pr['kvt'][:, n * PAIR:(n + 1) * PAIR], 0.0)
            pr['st_all'] = sts[0] if ng == 1 else jnp.concatenate(sts, axis=1)
    for pr in probs:
        pr['o'] = pr['o_intra'] + _dot_nt(pr['q_big'], pr['st_all'])
    for p in range(n_pairs):
        ST_ref[p] = ST[p]
        outs = [probs[p * n_groups + gi]['o'] for gi in range(n_groups)]
        o = outs[0] if n_groups == 1 else jnp.concatenate(outs, axis=0)
        gate = proj_ref[0, :, 3 * W_GLA + p * PAIR:3 * W_GLA + (p + 1) * PAIR]
        ms = _dot((o * o).astype(BF16), pmean)
        o_ref[0, :, p * PAIR:(p + 1) * PAIR] = (o * lax.rsqrt(ms + EPS) * nw * _silu(gate)).astype(o_ref.dtype)


def gla(proj, w_a2, b_a, s0_all, layer, depth, prev, norm_w, TT, m, cs, nb, act_dtype):
    b, l, _ = proj.shape
    n_pairs = H_GLA // 2
    wa2 = jnp.zeros((PAIR, W_GLA), F32).at[:GLA_RANK].set(w_a2).astype(BF16)
    nw = jnp.concatenate([norm_w, norm_w]).reshape(1, PAIR)
    st_tail = (H_GLA, HEAD_DIM, HEAD_DIM)
    in_specs = [pl.BlockSpec((nb, TT, GLA_COLS), lambda i, j: (i, j, 0)),
                pl.BlockSpec((PAIR, W_GLA), lambda i, j: (0, 0)),
                pl.BlockSpec((1, W_GLA), lambda i, j: (0, 0)),
                pl.BlockSpec((1, PAIR), lambda i, j: (0, 0))]
    args = [proj, wa2, b_a.reshape(1, W_GLA), nw]
    if s0_all is not None:
        in_specs.append(_state_specs(st_tail, layer, nb))
        args.append(s0_all)
    aliases = {}
    if prev is not None:
        aliases[len(args)] = 1
        in_specs.append(_ANY_SPEC)
        args.append(prev)
    return pl.pallas_call(
        functools.partial(_gla_kernel, TT=TT, m=m, cs=cs, n_pairs=n_pairs, has_state=s0_all is not None,
                          n_prev=0 if prev is None else 1, nb=nb),
        grid=(b // nb, l // TT),
        in_specs=in_specs,
        out_specs=[pl.BlockSpec((nb, TT, W_GLA), lambda i, j: (i, j, 0)), _state_specs(st_tail, layer, nb)],
        out_shape=[jax.ShapeDtypeStruct((b, l, W_GLA), act_dtype),
                   jax.ShapeDtypeStruct((depth, b) + st_tail, F32)],
        scratch_shapes=[pltpu.VMEM((nb, n_pairs, PAIR, PAIR), F32)],
        input_output_aliases=aliases,
        compiler_params=_cparams(("parallel", "arbitrary")),
        name="gla",
    )(*args)


def _proj_t_kernel(x_ref, g_ref, w_ref, o_ref, xn_ref):
    @pl.when(pl.program_id(0) == 0)
    def _():
        x = x_ref[...]
        xn = x * lax.rsqrt(jnp.mean(x * x, axis=0, keepdims=True) + EPS) * g_ref[...]
        xn_ref[...] = xn.astype(BF16)

    o_ref[...] = _dot(w_ref[...], xn_ref[...])


def proj_t(x_t, g, w_t, tn):
    d, t = x_t.shape
    n = w_t.shape[0]
    return pl.pallas_call(
        _proj_t_kernel,
        grid=(n // tn,),
        in_specs=[pl.BlockSpec((d, t), lambda i: (0, 0)),
                  pl.BlockSpec((d, 1), lambda i: (0, 0)),
                  pl.BlockSpec((tn, d), lambda i: (i, 0))],
        out_specs=pl.BlockSpec((tn, t), lambda i: (i, 0)),
        out_shape=jax.ShapeDtypeStruct((n, t), F32),
        scratch_shapes=[pltpu.VMEM((d, t), BF16)],
        compiler_params=_cparams(("arbitrary",)),
        name="proj_t",
    )(x_t, g.reshape(d, 1), w_t)


SUBLANES = 8


def _row_group(ref, g, lanes):
    return ref[pl.ds(pl.multiple_of(g * SUBLANES, SUBLANES), SUBLANES), lanes]


def _pick_row(block, r):
    return jnp.sum(jnp.where(_row_iota(block.shape) == r, block, 0.0), axis=0, keepdims=True)


def _lane_linear_kernel(*refs, n_tok, nbatch, rotary, layer_norm, row_decay, n_prev):
    if rotary:
        q_ref, k_ref, v_ref, gate_ref, cos_ref, sin_ref, dec_ref, nw_ref, s0_ref = refs[:9]
        rest = refs[9 + n_prev:]
    else:
        q_ref, k_ref, v_ref, gate_ref, a1_ref, wa2_ref, ba_ref, nw_ref, s0_ref = refs[:9]
        rest = refs[9 + n_prev:]
    o_ref, sout_ref, q_scr, k_scr, dec_scr = rest
    if rotary:
        half = HEAD_DIM // 2

        def rot(x):
            sw = jnp.concatenate([x[half:], x[:half]], axis=0)
            return x * cos_ref[...] + sw * sin_ref[...]

        q_scr[...] = rot(q_ref[...])
        k_scr[...] = rot(k_ref[...]) * HEAD_DIM ** -0.5
    else:
        q_scr[...] = q_ref[...] * HEAD_DIM ** -0.5
        k_scr[...] = k_ref[...]
        logit = _dot(wa2_ref[...], a1_ref[...].astype(BF16)) + ba_ref[...]
        dec_scr[...] = jnp.exp(-_softplus(-logit) * (1.0 / GLA_TAU))
    nw = nw_ref[...]
    for t in range(n_tok):
        lanes = slice(t * nbatch, (t + 1) * nbatch)
        v_t = v_ref[:, lanes]
        src = s0_ref if t == 0 else sout_ref

        def body(g, o, src=src, lanes=lanes, v_t=v_t):
            kg = _row_group(k_scr, g, lanes)
            qg = _row_group(q_scr, g, lanes)
            dg = _row_group(dec_scr, g, lanes) if row_decay else None
            for i in range(SUBLANES):
                d = g * SUBLANES + i
                dec = dg[i:i + 1] if row_decay else dec_ref[...]
                s_d = src[d] * dec + kg[i:i + 1] * v_t
                sout_ref[d] = s_d
                o = o + qg[i:i + 1] * s_d
            return o

        o = lax.fori_loop(0, HEAD_DIM // SUBLANES, body, jnp.zeros((HEAD_DIM, nbatch), F32))
        if layer_norm:
            o = o - jnp.mean(o, axis=0, keepdims=True)
        o = o * lax.rsqrt(jnp.mean(o * o, axis=0, keepdims=True) + EPS) * nw
        o_ref[:, lanes] = o * _silu(gate_ref[:, lanes])


def _head_rows(width, col0):
    return pl.BlockSpec((HEAD_DIM, width), lambda h: (col0 // HEAD_DIM + h, 0))


def lane_linear_mixer(proj_t_all, col0, width, n_heads, kind, consts, s_all, layer, depth, prev, n_tok, nbatch):
    t_all = n_tok * nbatch
    rotary = kind == 'ret'
    hr = lambda off: _head_rows(t_all, col0 + off)
    state_spec = pl.BlockSpec((None, None, HEAD_DIM, HEAD_DIM, nbatch), lambda h: (layer, h, 0, 0, 0))
    in_specs = [hr(0), hr(width), hr(2 * width), hr(3 * width)]
    args = [proj_t_all] * 4
    if rotary:
        cos_t, sin_t, gam, nw = consts
        in_specs += [pl.BlockSpec((HEAD_DIM, t_all), lambda h: (0, 0)),
                     pl.BlockSpec((HEAD_DIM, t_all), lambda h: (0, 0)),
                     pl.BlockSpec((None, 1, nbatch), lambda h: (h, 0, 0)),
                     pl.BlockSpec((HEAD_DIM, nbatch), lambda h: (0, 0))]
        args += [cos_t, sin_t, gam, nw]
    else:
        wa2_t, ba_t, nw = consts
        in_specs += [pl.BlockSpec((LANES, t_all), lambda h: ((col0 + 4 * width) // LANES, 0)),
                     pl.BlockSpec((HEAD_DIM, LANES), lambda h: (h, 0)),
                     pl.BlockSpec((HEAD_DIM, 1), lambda h: (h, 0)),
                     pl.BlockSpec((HEAD_DIM, nbatch), lambda h: (0, 0))]
        args += [proj_t_all, wa2_t, ba_t, nw]
    in_specs.append(state_spec)
    args.append(s_all)
    aliases = {}
    if prev is not None:
        aliases[len(args)] = 1
        in_specs.append(_ANY_SPEC)
        args.append(prev)
    return pl.pallas_call(
        functools.partial(_lane_linear_kernel, n_tok=n_tok, nbatch=nbatch, rotary=rotary, layer_norm=rotary,
                          row_decay=not rotary, n_prev=0 if prev is None else 1),
        grid=(n_heads,),
        in_specs=in_specs,
        out_specs=[pl.BlockSpec((HEAD_DIM, t_all), lambda h: (h, 0)), state_spec],
        out_shape=[jax.ShapeDtypeStruct((width, t_all), F32),
                   jax.ShapeDtypeStruct((depth, n_heads, HEAD_DIM, HEAD_DIM, nbatch), F32)],
        scratch_shapes=[pltpu.VMEM((HEAD_DIM, t_all), F32)] * 3,
        input_output_aliases=aliases,
        compiler_params=_cparams(("arbitrary",)),
        name="lane_" + kind,
    )(*args)


def _lane_delta_kernel(*refs, n_tok, nbatch, n_prev):
    (qp_ref, kp_ref, vp_ref, gate_ref, misc_ref, wq_ref, wk_ref, wv_ref, cq_ref, ck_ref, cv_ref,
     alog_ref, dtb_ref, nw_ref, s0_ref) = refs[:15]
    o_ref, sout_ref, cqo_ref, cko_ref, cvo_ref, q_scr, k_scr = refs[15 + n_prev:]
    h = pl.program_id(0)

    def conv(x_ref, w_ref, c_ref, cout_ref):
        xs = [c_ref[j] for j in range(CONV_W - 1)] + [x_ref[:, t * nbatch:(t + 1) * nbatch] for t in range(n_tok)]
        outs = []
        for t in range(n_tok):
            acc = xs[t] * w_ref[0]
            for j in range(1, CONV_W):
                acc = acc + xs[t + j] * w_ref[j]
            outs.append(_silu(acc))
        for j in range(CONV_W - 1):
            cout_ref[j] = xs[n_tok + j]
        return jnp.concatenate(outs, axis=1)

    q = conv(qp_ref, wq_ref, cq_ref, cqo_ref)
    k = conv(kp_ref, wk_ref, ck_ref, cko_ref)
    v = conv(vp_ref, wv_ref, cv_ref, cvo_ref)
    q_scr[...] = q * lax.rsqrt(jnp.sum(q * q, axis=0, keepdims=True) + EPS) * HEAD_DIM ** -0.5
    k_scr[...] = k * lax.rsqrt(jnp.sum(k * k, axis=0, keepdims=True) + EPS)
    misc = misc_ref[0:2 * SUBLANES, :]
    beta = jax.nn.sigmoid(_pick_row(misc, h))
    glog = -jnp.exp(alog_ref[...]) * _softplus(_pick_row(misc, DN_A_LANE + h) + dtb_ref[...])
    a_all = jnp.exp(glog)
    nw = nw_ref[...]
    zero = jnp.zeros((HEAD_DIM, nbatch), F32)
    for t in range(n_tok):
        lanes = slice(t * nbatch, (t + 1) * nbatch)
        src = s0_ref if t == 0 else sout_ref
        a_t = a_all[:, lanes]

        def kts(g, r, src=src, lanes=lanes):
            kg = _row_group(k_scr, g, lanes)
            for i in range(SUBLANES):
                r = r + kg[i:i + 1] * src[g * SUBLANES + i]
            return r

        r = lax.fori_loop(0, HEAD_DIM // SUBLANES, kts, zero)
        w = beta[:, lanes] * (v[:, lanes] - a_t * r)

        def body(g, o, src=src, lanes=lanes, a_t=a_t, w=w):
            kg = _row_group(k_scr, g, lanes)
            qg = _row_group(q_scr, g, lanes)
            for i in range(SUBLANES):
                d = g * SUBLANES + i
                s_d = src[d] * a_t + kg[i:i + 1] * w
                sout_ref[d] = s_d
                o = o + qg[i:i + 1] * s_d
            return o

        o = lax.fori_loop(0, HEAD_DIM // SUBLANES, body, zero)
        o = o * lax.rsqrt(jnp.mean(o * o, axis=0, keepdims=True) + EPS) * nw
        o_ref[:, lanes] = o * _silu(gate_ref[:, lanes])


def lane_delta_net(proj_t_all, col0, conv_w_b, conv_parts, alog_b, dtb_b, nw, s_all, layer, depth, prev, n_tok, nbatch):
    t_all = n_tok * nbatch
    hr = lambda off: _head_rows(t_all, col0 + off)
    state_spec = pl.BlockSpec((None, None, HEAD_DIM, HEAD_DIM, nbatch), lambda h: (layer, h, 0, 0, 0))
    tap_spec = lambda i: pl.BlockSpec((None, CONV_W, HEAD_DIM, nbatch), lambda h, i=i: (i, 0, h, 0))
    cst_spec = pl.BlockSpec((None, CONV_W - 1, HEAD_DIM, nbatch), lambda h: (layer, 0, h, 0))
    head_row = pl.BlockSpec((None, 1, 1), lambda h: (h, 0, 0))
    in_specs = [hr(0), hr(W_DN), hr(2 * W_DN), hr(DN_CONV_CH),
                pl.BlockSpec((LANES, t_all), lambda h: ((col0 + DN_CONV_CH + W_DN) // LANES, 0)),
                tap_spec(0), tap_spec(1), tap_spec(2), cst_spec, cst_spec, cst_spec,
                head_row, head_row, pl.BlockSpec((HEAD_DIM, nbatch), lambda h: (0, 0)), state_spec]
    args = [proj_t_all] * 5 + [conv_w_b] * 3 + list(conv_parts) + [alog_b, dtb_b, nw, s_all]
    aliases = {}
    if prev is not None:
        aliases = {len(args) + i: 1 + i for i in range(4)}
        in_specs += [_ANY_SPEC] * 4
        args += list(prev)
    cshape = jax.ShapeDtypeStruct((depth, CONV_W - 1, W_DN, nbatch), F32)
    return pl.pallas_call(
        functools.partial(_lane_delta_kernel, n_tok=n_tok, nbatch=nbatch, n_prev=0 if prev is None else 4),
        grid=(H_DN,),
        in_specs=in_specs,
        out_specs=[pl.BlockSpec((HEAD_DIM, t_all), lambda h: (h, 0)), state_spec, cst_spec, cst_spec, cst_spec],
        out_shape=[jax.ShapeDtypeStruct((W_DN, t_all), F32),
                   jax.ShapeDtypeStruct((depth, H_DN, HEAD_DIM, HEAD_DIM, nbatch), F32), cshape, cshape, cshape],
        scratch_shapes=[pltpu.VMEM((HEAD_DIM, t_all), F32)] * 2,
        input_output_aliases=aliases,
        compiler_params=_cparams(("arbitrary",)),
        name="lane_delta",
    )(*args)


def _prep_layer_weights(li, w_in_t, w_out, w_mq, w_mo, w_ff1, w_ff2):
    wi = w_in_t[li]
    d = wi.shape[1]
    c0 = RET_COLS
    c1 = c0 + DN_CONV_CH + W_DN + 2 * H_DN
    w_ret = wi[:c0]
    w_dn = jnp.concatenate([wi[c0:c1], jnp.zeros((DN_COLS - (c1 - c0), d), F32)], axis=0)
    w_gla = jnp.concatenate([wi[c1:], jnp.zeros((GLA_COLS - (wi.shape[0] - c1), d), F32)], axis=0)
    cast = lambda a: a.astype(BF16)
    return dict(w_all_t=cast(jnp.concatenate([w_ret, w_dn, w_gla], axis=0)), w_out=cast(w_out[li]),
                w_mq=cast(w_mq[li]), w_mo=cast(w_mo[li]), w_ff1=cast(w_ff1[li]), w_ff2=cast(w_ff2[li]))


def _run_trunk(x, pos, s_ret, s_dn, s_conv, s_gla, mem_k, mem_v, mem_split, lw, p, cfg):
    b, l, d = x.shape
    t = b * l
    depth = len(lw)
    xf = x.reshape(t, d)
    lane_batch = cfg['lane_batch']
    act = cfg['act_dtype']
    if lane_batch:
        lane = _lane_setup(pos, b, s_ret, s_dn, s_conv, s_gla)
    else:
        ret_tables = _ret_consts(cfg['ret_c']) + _rope_tables(pos)
    r_s = d_s = g_s = None
    for li in range(depth):
        w = lw[li]
        g = p['norm_mix'][li]
        if lane_batch:
            o_mix, r_s, d_s, g_s = _lane_mixers(xf, b, l, li, depth, w, p, lane, r_s, d_s, g_s)
            mixed = [o_mix]
        else:
            pr, pd, pg = in_proj(xf, g, w['w_all_t'], (RET_COLS, DN_COLS, GLA_COLS), cfg['tm'])
            pr, pd, pg = pr.reshape(b, l, RET_COLS), pd.reshape(b, l, DN_COLS), pg.reshape(b, l, GLA_COLS)
            o_ret, r_s = retention(pr, ret_tables, s_ret, li, depth, r_s, p['ret_norm_w'][li], cfg['ret_c'],
                                   cfg['nb'], act)
            o_dn, *d_s = delta_net(pd, p['dn_conv_w'][li], s_conv, p['dn_a_log'][li], p['dn_dt_bias'][li], s_dn,
                                   li, depth, d_s, p['dn_norm_w'][li], cfg['dn_tt'], cfg['dn_c'], cfg['nb'], act)
            o_gla, g_s = gla(pg, p['gla_w_a2'][li], p['gla_b_a'][li], s_gla, li, depth, g_s, p['gla_norm_w'][li],
                             cfg['gla_tt'], cfg['gla_m'], cfg['gla_cs'], cfg['nb'], act)
            mixed = [o_ret.reshape(t, W_RET), o_dn.reshape(t, W_DN), o_gla.reshape(t, W_GLA)]
        xf, qm = res_then_proj(mixed, w['w_out'], xf, p['norm_cross'][li], w['w_mq'], cfg['tm'], act)
        qm = qm.reshape(b, l, d)
        if mem_split:
            att = mem_attention(qm, mem_k, mem_v, cfg['tq'], cfg['attn_nb'], layer=li)
        else:
            att = mem_attention(qm, mem_k[li], mem_v[li], cfg['tq'], cfg['attn_nb'])
        xf = ffn(xf, att.reshape(t, d), w['w_mo'], p['norm_ffn'][li], w['w_ff1'], w['w_ff2'], p['norm_final'],
                 li == depth - 1, cfg['ffn_tm'], cfg['ffn_tf'])
    if lane_batch:
        to_batch_major = lambda s: jnp.transpose(s, (0, 4, 1, 2, 3))
        conv = jnp.transpose(jnp.concatenate(d_s[1:], axis=2), (0, 3, 1, 2))
        return (xf.reshape(b, l, d), to_batch_major(r_s), to_batch_major(d_s[0]), conv, to_batch_major(g_s))
    return (xf.reshape(b, l, d), r_s, d_s[0], d_s[1], g_s)


def _lane_setup(pos, b, s_ret, s_dn, s_conv, s_gla):
    half = HEAD_DIM // 2
    inv_freq = ROPE_BASE ** (-jnp.arange(half, dtype=F32) / half)
    ang = pos.astype(F32)[:, None] * inv_freq[None, :]
    cos, sin = jnp.cos(ang), jnp.sin(ang)
    spread = lambda a: jnp.repeat(a.T, b, axis=1)
    lg = jnp.log1p(-jnp.exp2(-5.0 - jnp.arange(H_RET, dtype=F32)))
    to_lane = lambda s: jnp.transpose(s, (0, 2, 3, 4, 1))
    conv_t = jnp.transpose(s_conv, (0, 2, 3, 1))
    return dict(cos=spread(jnp.concatenate([cos, cos], axis=1)), sin=spread(jnp.concatenate([-sin, sin], axis=1)),
                gam=jnp.broadcast_to(jnp.exp(lg)[:, None, None], (H_RET, 1, b)),
                s_ret=to_lane(s_ret), s_dn=to_lane(s_dn), s_gla=to_lane(s_gla),
                conv=[conv_t[:, :, i * W_DN:(i + 1) * W_DN, :] for i in range(3)])


def _lane_mixers(xf, b, l, li, depth, w, p, lane, r_s, d_s, g_s):
    d = xf.shape[1]
    col = lambda v: jnp.broadcast_to(v[:, None], (v.shape[0], b))
    x_t = xf.reshape(b, l, d).transpose(2, 1, 0).reshape(d, l * b)
    pt = proj_t(x_t, p['norm_mix'][li], w['w_all_t'], 256)
    o_ret, r_s = lane_linear_mixer(pt, 0, W_RET, H_RET, 'ret',
                                   (lane['cos'], lane['sin'], lane['gam'], col(p['ret_norm_w'][li])),
                                   lane['s_ret'], li, depth, r_s, l, b)
    taps = p['dn_conv_w'][li].reshape(CONV_W, 3, W_DN).transpose(1, 0, 2)
    taps = jnp.broadcast_to(taps[..., None], taps.shape + (b,))
    o_dn, *d_s = lane_delta_net(pt, RET_COLS, taps, lane['conv'], p['dn_a_log'][li].reshape(H_DN, 1, 1),
                                p['dn_dt_bias'][li].reshape(H_DN, 1, 1), col(p['dn_norm_w'][li]),
                                lane['s_dn'], li, depth, d_s, l, b)
    wa2_t = jnp.zeros((W_GLA, LANES), F32).at[:, :GLA_RANK].set(p['gla_w_a2'][li].T).astype(BF16)
    o_gla, g_s = lane_linear_mixer(pt, RET_COLS + DN_COLS, W_GLA, H_GLA, 'gla',
                                   (wa2_t, p['gla_b_a'][li].reshape(W_GLA, 1), col(p['gla_norm_w'][li])),
                                   lane['s_gla'], li, depth, g_s, l, b)
    o_t = jnp.concatenate([o_ret, o_dn, o_gla], axis=0)
    o = o_t.reshape(o_t.shape[0], l, b).transpose(2, 1, 0).reshape(b * l, o_t.shape[0])
    return o, r_s, d_s, g_s


def kernel(x_prompt, x_sample, state_ret, state_dn, state_dn_conv, state_gla, cache_mem_k, cache_mem_v, mem_prompt, norm_mix, w_in, ret_norm_w, dn_conv_w, dn_a_log, dn_dt_bias, dn_norm_w, gla_w_a2, gla_b_a, gla_norm_w, w_out, norm_cross, norm_mem, w_mq, w_mk, w_mv, w_mo, norm_ffn, w_ff1, w_ff2, norm_final):
    b_p, l_p, d = x_prompt.shape
    b_s, l_s, _ = x_sample.shape
    depth = w_in.shape[0]
    n_mem = mem_prompt.shape[1]
    p = dict(norm_mix=norm_mix, ret_norm_w=ret_norm_w, dn_conv_w=dn_conv_w, dn_a_log=dn_a_log,
             dn_dt_bias=dn_dt_bias, dn_norm_w=dn_norm_w, gla_w_a2=gla_w_a2, gla_b_a=gla_b_a,
             gla_norm_w=gla_norm_w, norm_cross=norm_cross, norm_ffn=norm_ffn, norm_final=norm_final)
    w_in_t = jnp.transpose(w_in, (0, 2, 1))
    lw = [_prep_layer_weights(li, w_in_t, w_out, w_mq, w_mo, w_ff1, w_ff2) for li in range(depth)]

    memf = mem_prompt.reshape(b_p * n_mem, d)
    mk = [rms_matmul(memf, norm_mem[li], w_mk[li], 512).reshape(b_p, n_mem, d) for li in range(depth)]
    mv = [rms_matmul(memf, norm_mem[li], w_mv[li], 512).reshape(b_p, n_mem, d) for li in range(depth)]

    cfg_p = dict(tm=512, ret_c=256, dn_tt=256, dn_c=math.gcd(l_p, DN_CHUNK), gla_tt=256, gla_m=64,
                 gla_cs=math.gcd(l_p, GLA_CHUNK), tq=512, attn_nb=1, ffn_tm=1024, ffn_tf=1024, nb=1, act_dtype=BF16,
                 lane_batch=False)
    y_p, p_ret, p_dn, p_conv, p_gla = _run_trunk(
        x_prompt, jnp.arange(l_p, dtype=jnp.int32), None, None, None, None, mk, mv, False, lw, p, cfg_p)

    cfg_s = dict(tm=512, ret_c=l_s, dn_tt=l_s, dn_c=math.gcd(l_s, DN_CHUNK), gla_tt=l_s,
                 gla_m=math.gcd(l_s, GLA_CHUNK), gla_cs=math.gcd(l_s, GLA_CHUNK), tq=l_s, ffn_tm=1024, ffn_tf=512,
                 attn_nb=2, nb=8, act_dtype=F32, lane_batch=True)
    y_s, s_ret, s_dn, s_conv, s_gla = _run_trunk(
        x_sample, PAST_LEN + jnp.arange(l_s, dtype=jnp.int32), state_ret, state_dn, state_dn_conv, state_gla,
        cache_row_view(cache_mem_k), cache_row_view(cache_mem_v), True, lw, p, cfg_s)

    mem_shape = (depth, b_p, n_mem, MEM_HEADS, d // MEM_HEADS)
    return (y_p, y_s, p_ret, p_dn, p_conv, p_gla, jnp.stack(mk).reshape(mem_shape), jnp.stack(mv).reshape(mem_shape),
            s_ret, s_dn, s_conv, s_gla)
```
